```python
import jax, jax.numpy as jnp
from jax import lax
import numpy as np

D_MODEL = 1024
BATCH = 8
SEQ = 2048
DEPTH = 2
DEC_BATCH = 8
DEC_SEQ = 64
PAST_LEN = 2048

CHUNK = 64
N_MIXERS = 2
N_CONV_LAYERS = (DEPTH + N_MIXERS - 1) // N_MIXERS
N_ATTN_LAYERS = DEPTH // N_MIXERS
CONV_WIDTH = 3
N_HEADS = 16
HEAD_DIM = D_MODEL // N_HEADS
D_FF = 2816
Q_BLOCK = 128
FFN_RESIDUAL = 0.5
NORM_EPS = 1e-6
FORGET_BIAS_MEAN = 3.0
NEG_INF = -1e30

kernel_name = "conv_fox_macaron_stream_step"


def rms_norm(x, g):
    xf = x.astype(jnp.float32)
    y = xf * lax.rsqrt(jnp.mean(xf * xf, axis=-1, keepdims=True) + NORM_EPS)
    return (y * g.astype(jnp.float32)).astype(x.dtype)


def swiglu_ffn(x, g, w_in, w_out):
    a, b = jnp.split(rms_norm(x, g) @ w_in, 2, axis=-1)
    return (jax.nn.silu(a) * b) @ w_out


def conv_mixer(xn, hist, w_in, w_k, w_out):
    gate_b, gate_c, h = jnp.split(xn @ w_in, 3, axis=-1)
    u = gate_c * h
    u_full = jnp.concatenate([hist.astype(u.dtype), u], axis=1)
    t_len = u.shape[1]
    conv = w_k[0] * u_full[:, 0:t_len]
    for j in range(1, CONV_WIDTH):
        conv = conv + w_k[j] * u_full[:, j:j + t_len]
    y = (gate_b * conv) @ w_out
    return y, u_full[:, t_len:]


def fox_project(xn, w_in, b_f, gq, gk):
    b, t, _ = xn.shape
    proj = xn @ w_in
    q = proj[..., :D_MODEL].reshape(b, t, N_HEADS, HEAD_DIM)
    k = proj[..., D_MODEL:2 * D_MODEL].reshape(b, t, N_HEADS, HEAD_DIM)
    v = proj[..., 2 * D_MODEL:3 * D_MODEL].reshape(b, t, N_HEADS, HEAD_DIM)
    f_logit = proj[..., 3 * D_MODEL:].astype(jnp.float32) + b_f.astype(jnp.float32)
    q = rms_norm(q, gq)
    k = rms_norm(k, gk)
    logf = jax.nn.log_sigmoid(f_logit)
    return q, k, v, logf


def fox_attend(q, k, v, c_q, c_k, q_pos, k_pos):
    s = jnp.einsum('bqhd,bkhd->bhqk', q, k, preferred_element_type=jnp.float32) * (HEAD_DIM ** -0.5)
    decay = jnp.transpose(c_q, (0, 2, 1))[..., :, None] - jnp.transpose(c_k, (0, 2, 1))[..., None, :]
    mask = k_pos[None, :] <= q_pos[:, None]
    s = jnp.where(mask, s + decay, NEG_INF)
    p = jax.nn.softmax(s, axis=-1)
    return jnp.einsum('bhqk,bkhd->bqhd', p.astype(v.dtype), v)


def fox_prompt(q, k, v, logf):
    b, s_len, h, d = q.shape
    nb = s_len // Q_BLOCK
    c = jnp.cumsum(logf, axis=1)
    qb = jnp.transpose(q.reshape(b, nb, Q_BLOCK, h, d), (1, 0, 2, 3, 4))
    cb = jnp.transpose(c.reshape(b, nb, Q_BLOCK, h), (1, 0, 2, 3))
    k_pos = jnp.arange(s_len)

    def block(args):
        idx, q_blk, c_blk = args
        q_pos = idx * Q_BLOCK + jnp.arange(Q_BLOCK)
        return fox_attend(q_blk, k, v, c_blk, c, q_pos, k_pos)

    out = lax.map(block, (jnp.arange(nb), qb, cb))
    return jnp.transpose(out, (1, 0, 2, 3, 4)).reshape(b, s_len, h * d)


def fox_sample(q, k, v, logf, past_k, past_v, past_logf):
    b, t_len, h, d = q.shape
    past = past_k.shape[1]
    k_all = jnp.concatenate([past_k.astype(k.dtype), k], axis=1)
    v_all = jnp.concatenate([past_v.astype(v.dtype), v], axis=1)
    c_all = jnp.cumsum(jnp.concatenate([past_logf.astype(jnp.float32), logf], axis=1), axis=1)
    q_pos = past + jnp.arange(t_len)
    k_pos = jnp.arange(past + t_len)
    out = fox_attend(q, k_all, v_all, c_all[:, past:], c_all, q_pos, k_pos)
    return out.reshape(b, t_len, h * d)


def setup_inputs(seed: int = 0) -> dict:
    key = jax.random.key(seed)
    ks = jax.random.split(key, 18)
    nrm = lambda k, shape: jax.random.normal(k, shape, jnp.float32)
    D, H, F = D_MODEL, N_HEADS, D_FF
    return {
        "x_prompt": nrm(ks[0], (BATCH, SEQ, D)),
        "x_sample": nrm(ks[1], (DEC_BATCH, DEC_SEQ, D)),
        "state_conv": nrm(ks[2], (N_CONV_LAYERS, DEC_BATCH, CONV_WIDTH - 1, D)),
        "cache_k": nrm(ks[3], (N_ATTN_LAYERS, DEC_BATCH, PAST_LEN, H, HEAD_DIM)),
        "cache_v": nrm(ks[4], (N_ATTN_LAYERS, DEC_BATCH, PAST_LEN, H, HEAD_DIM)),
        "cache_logf": jax.nn.log_sigmoid(FORGET_BIAS_MEAN + 0.5 * nrm(ks[5], (N_ATTN_LAYERS, DEC_BATCH, PAST_LEN, H))),
        "norm_ffn": 1.0 + 0.05 * nrm(ks[6], (DEPTH, 2, D)),
        "ffn_w_in": nrm(ks[7], (DEPTH, 2, D, 2 * F)) * D ** -0.5,
        "ffn_w_out": nrm(ks[8], (DEPTH, 2, F, D)) * F ** -0.5,
        "norm_mix": 1.0 + 0.05 * nrm(ks[9], (DEPTH, D)),
        "conv_w_in": nrm(ks[10], (N_CONV_LAYERS, D, 3 * D)) * D ** -0.5,
        "conv_w": nrm(ks[11], (N_CONV_LAYERS, CONV_WIDTH, D)) * CONV_WIDTH ** -0.5,
        "conv_w_out": nrm(ks[12], (N_CONV_LAYERS, D, D)) * D ** -0.5,
        "attn_w_in": nrm(ks[13], (N_ATTN_LAYERS, D, 3 * D + H)) * D ** -0.5,
        "attn_b_f": FORGET_BIAS_MEAN + 0.5 * nrm(ks[14], (N_ATTN_LAYERS, H)),
        "q_norm": 1.0 + 0.05 * nrm(ks[15], (N_ATTN_LAYERS, HEAD_DIM)),
        "k_norm": 1.0 + 0.05 * nrm(ks[16], (N_ATTN_LAYERS, HEAD_DIM)),
        "attn_w_out": nrm(ks[17], (N_ATTN_LAYERS, D, D)) * D ** -0.5,
    }


def reference(x_prompt, x_sample, state_conv, cache_k, cache_v, cache_logf,
              norm_ffn, ffn_w_in, ffn_w_out, norm_mix,
              conv_w_in, conv_w, conv_w_out,
              attn_w_in, attn_b_f, q_norm, k_norm, attn_w_out):
    yp, ys = x_prompt, x_sample
    conv_p, conv_s = [], []
    kp_l, vp_l, fp_l, ks_l, vs_l, fs_l = [], [], [], [], [], []
    for i in range(DEPTH):
        li = i // N_MIXERS
        yp = yp + FFN_RESIDUAL * swiglu_ffn(yp, norm_ffn[i, 0], ffn_w_in[i, 0], ffn_w_out[i, 0])
        ys = ys + FFN_RESIDUAL * swiglu_ffn(ys, norm_ffn[i, 0], ffn_w_in[i, 0], ffn_w_out[i, 0])
        xn_p = rms_norm(yp, norm_mix[i])
        xn_s = rms_norm(ys, norm_mix[i])
        if i % N_MIXERS == 0:
            hist_p = jnp.zeros((yp.shape[0], CONV_WIDTH - 1, D_MODEL), yp.dtype)
            mp, st_p = conv_mixer(xn_p, hist_p, conv_w_in[li], conv_w[li], conv_w_out[li])
            ms, st_s = conv_mixer(xn_s, state_conv[li], conv_w_in[li], conv_w[li], conv_w_out[li])
            conv_p.append(st_p)
            conv_s.append(st_s)
        else:
            qp, kp, vp, fp = fox_project(xn_p, attn_w_in[li], attn_b_f[li], q_norm[li], k_norm[li])
            qs, kq, vq, fq = fox_project(xn_s, attn_w_in[li], attn_b_f[li], q_norm[li], k_norm[li])
            mp = fox_prompt(qp, kp, vp, fp) @ attn_w_out[li]
            ms = fox_sample(qs, kq, vq, fq, cache_k[li], cache_v[li], cache_logf[li]) @ attn_w_out[li]
            kp_l.append(kp); vp_l.append(vp); fp_l.append(fp)
            ks_l.append(kq); vs_l.append(vq); fs_l.append(fq)
        yp = yp + mp
        ys = ys + ms
        yp = yp + FFN_RESIDUAL * swiglu_ffn(yp, norm_ffn[i, 1], ffn_w_in[i, 1], ffn_w_out[i, 1])
        ys = ys + FFN_RESIDUAL * swiglu_ffn(ys, norm_ffn[i, 1], ffn_w_in[i, 1], ffn_w_out[i, 1])
    return (yp, ys,
            jnp.stack(conv_p), jnp.stack(conv_s),
            jnp.stack(kp_l), jnp.stack(vp_l), jnp.stack(fp_l),
            jnp.stack(ks_l), jnp.stack(vs_l), jnp.stack(fs_l))
```

```python
import functools

import jax
import jax.numpy as jnp
from jax import lax
from jax.experimental import pallas as pl
from jax.experimental.pallas import tpu as pltpu

D_MODEL = 1024
N_HEADS = 16
HEAD_DIM = 64
D_FF = 2816
CONV_WIDTH = 3
NORM_EPS = 1e-6
NEG_INF = -1e30
FFN_RESIDUAL = 0.5

LANES = 128
F_PAD = LANES
VMEM_LIMIT = 56 * 1024 * 1024
FF_CHUNKS = ((0, 1536), (1536, 2816))
ROW_TILE = 512
ATT_TILE = 256
SCAN_TILE = 256

f32 = jnp.float32
bf16 = jnp.bfloat16


def _dot(a, b):
    return jnp.dot(a, b, preferred_element_type=f32)


def _dot_nt(a, b):
    return lax.dot_general(a, b, (((1,), (1,)), ((), ())), preferred_element_type=f32)


def _rms(x, g):
    return x * lax.rsqrt(jnp.mean(x * x, axis=-1, keepdims=True) + NORM_EPS) * g


def _log_sigmoid(z):
    return -(jnp.maximum(-z, 0.0) + jnp.log1p(jnp.exp(-jnp.abs(z))))


def _params(n_axes):
    return pltpu.CompilerParams(dimension_semantics=("arbitrary",) * n_axes,
                                vmem_limit_bytes=VMEM_LIMIT)


def _const_spec(shape):
    return pl.BlockSpec(shape, lambda *_: (0,) * len(shape), pipeline_mode=pl.Buffered(1))


def _ffn_body(x, g_ref, wg_ref, wu_ref, wo_ref):
    xn = _rms(x, g_ref[...]).astype(bf16)
    acc = None
    for lo, hi in FF_CHUNKS:
        a = _dot(xn, wg_ref[:, lo:hi])
        b = _dot(xn, wu_ref[:, lo:hi])
        h = (a * jax.nn.sigmoid(a) * b).astype(bf16)
        y = _dot(h, wo_ref[lo:hi, :])
        acc = y if acc is None else acc + y
    return x + FFN_RESIDUAL * acc


def _ffn_kernel(x_ref, g_ref, wg_ref, wu_ref, wo_ref, o_ref):
    o_ref[...] = _ffn_body(x_ref[...], g_ref, wg_ref, wu_ref, wo_ref)


def _mix_ffn_kernel(x_ref, a_ref, wm_ref, g_ref, wg_ref, wu_ref, wo_ref, o_ref):
    x = x_ref[...] + _dot(a_ref[...], wm_ref[...])
    o_ref[...] = _ffn_body(x, g_ref, wg_ref, wu_ref, wo_ref)


def _ffn(x, g, wg, wu, wo, mix=None):
    m = x.shape[0]
    tm = min(ROW_TILE, m)
    row = pl.BlockSpec((tm, D_MODEL), lambda i: (i, 0))
    w_specs = [_const_spec((1, D_MODEL)), _const_spec((D_MODEL, D_FF)),
               _const_spec((D_MODEL, D_FF)), _const_spec((D_FF, D_MODEL))]
    if mix is None:
        kern, ins, specs = _ffn_kernel, (x, g, wg, wu, wo), [row] + w_specs
    else:
        a, wm = mix
        kern, ins = _mix_ffn_kernel, (x, a, wm, g, wg, wu, wo)
        specs = [row, row, _const_spec((D_MODEL, D_MODEL))] + w_specs
    return pl.pallas_call(
        kern, grid=(m // tm,), in_specs=specs, out_specs=row,
        out_shape=jax.ShapeDtypeStruct((m, D_MODEL), f32),
        name="ffn" if mix is None else "mix_ffn",
        compiler_params=_params(1))(*ins)


def _conv_kernel(x_ref, hist_ref, g_ref, win_ref, wk_ref, wout_ref, o_ref, st_ref, carry_ref):
    t = pl.program_id(1)
    tt = x_ref.shape[1]

    @pl.when(t == 0)
    def _():
        carry_ref[0:2, :] = hist_ref[0]

    x = x_ref[0]
    xn = _rms(x, g_ref[...]).astype(bf16)
    p = _dot(xn, win_ref[...])
    gate_b = p[:, :D_MODEL]
    u = p[:, D_MODEL:2 * D_MODEL] * p[:, 2 * D_MODEL:]
    prev2 = carry_ref[0:1, :]
    prev1 = carry_ref[1:2, :]
    row = lax.broadcasted_iota(jnp.int32, (tt, 1), 0)
    u1 = jnp.where(row == 0, prev1, pltpu.roll(u, 1, 0))
    u2 = jnp.where(row == 0, prev2, jnp.where(row == 1, prev1, pltpu.roll(u, 2, 0)))
    wk = wk_ref[...]
    conv = wk[0:1, :] * u2 + wk[1:2, :] * u1 + wk[2:3, :] * u
    y = _dot((gate_b * conv).astype(bf16), wout_ref[...])
    o_ref[0] = x + y
    last = u[tt - 2:tt, :]
    carry_ref[0:2, :] = last
    st_ref[0] = last


def _conv_mixer(x, hist, g, win, wk, wout):
    b, t, _ = x.shape
    tt = min(ROW_TILE, t)
    blk = pl.BlockSpec((1, tt, D_MODEL), lambda i, j: (i, j, 0))
    st = pl.BlockSpec((1, CONV_WIDTH - 1, D_MODEL), lambda i, j: (i, 0, 0))
    return pl.pallas_call(
        _conv_kernel, grid=(b, t // tt),
        in_specs=[blk, st, _const_spec((1, D_MODEL)), _const_spec((D_MODEL, 3 * D_MODEL)),
                  _const_spec((CONV_WIDTH, D_MODEL)), _const_spec((D_MODEL, D_MODEL))],
        out_specs=[blk, st],
        out_shape=[jax.ShapeDtypeStruct((b, t, D_MODEL), f32),
                   jax.ShapeDtypeStruct((b, CONV_WIDTH - 1, D_MODEL), f32)],
        scratch_shapes=[pltpu.VMEM((8, D_MODEL), f32)],
        name="conv_mixer", compiler_params=_params(2))(x, hist, g, win, wk, wout)


def _head_norm(t, gain, seg_ref, exp_ref):
    ms = _dot((t * t).astype(bf16), seg_ref[...])
    r = lax.rsqrt(ms + NORM_EPS)
    r_hi = r.astype(bf16)
    r_lo = (r - r_hi.astype(f32)).astype(bf16)
    rb = _dot(jnp.concatenate([r_hi, r_lo], axis=1), exp_ref[...])
    return t * rb * gain


def _proj_kernel(x_ref, g_ref, w_ref, wft_ref, bcol_ref, brow_ref, seg_ref, exp_ref, gq_ref, gk_ref,
                 k_ref, v_ref, lf_ref, lft_ref, qb_ref, kb_ref, vb_ref):
    xn = _rms(x_ref[0], g_ref[...]).astype(bf16)
    p = _dot(xn, w_ref[...])
    q = _head_norm(p[:, :D_MODEL], gq_ref[...], seg_ref, exp_ref)
    k = _head_norm(p[:, D_MODEL:2 * D_MODEL], gk_ref[...], seg_ref, exp_ref)
    v = p[:, 2 * D_MODEL:3 * D_MODEL]
    k_ref[0] = k
    v_ref[0] = v
    lf_ref[0] = _log_sigmoid(p[:, 3 * D_MODEL:3 * D_MODEL + N_HEADS] + brow_ref[...])
    lft_ref[0] = _log_sigmoid(_dot_nt(wft_ref[...], xn) + bcol_ref[...])
    qb_ref[0] = (q * (HEAD_DIM ** -0.5)).astype(bf16)
    kb_ref[0] = k.astype(bf16)
    vb_ref[0] = v.astype(bf16)


def _attn_proj(x, g, w, wft, bcol, brow, seg, exp, gq, gk):
    b, t, _ = x.shape
    tt = min(ROW_TILE, t)
    blk = pl.BlockSpec((1, tt, D_MODEL), lambda i, j: (i, j, 0))
    lf = pl.BlockSpec((1, tt, N_HEADS), lambda i, j: (i, j, 0))
    lft = pl.BlockSpec((1, N_HEADS, tt), lambda i, j: (i, 0, j))
    big = jax.ShapeDtypeStruct((b, t, D_MODEL), f32)
    half = jax.ShapeDtypeStruct((b, t, D_MODEL), bf16)
    return pl.pallas_call(
        _proj_kernel, grid=(b, t // tt),
        in_specs=[blk, _const_spec((1, D_MODEL)), _const_spec((D_MODEL, 3 * D_MODEL + F_PAD)),
                  _const_spec((N_HEADS, D_MODEL)), _const_spec((N_HEADS, 1)), _const_spec((1, N_HEADS)),
                  _const_spec((D_MODEL, LANES)), _const_spec((2 * LANES, D_MODEL)),
                  _const_spec((1, D_MODEL)), _const_spec((1, D_MODEL))],
        out_specs=[blk, blk, lf, lft, blk, blk, blk],
        out_shape=[big, big, jax.ShapeDtypeStruct((b, t, N_HEADS), f32),
                   jax.ShapeDtypeStruct((b, N_HEADS, t), f32), half, half, half],
        name="attn_proj", compiler_params=_params(2))(x, g, w, wft, bcol, brow, seg, exp, gq, gk)


def _scan_kernel(x_ref, tri_ref, o_ref):
    n = x_ref.shape[2] // SCAN_TILE
    carry = jnp.zeros((N_HEADS, 1), f32)
    for c in range(n):
        sl = slice(c * SCAN_TILE, (c + 1) * SCAN_TILE)
        lf = x_ref[0, :, sl]
        hi = lf.astype(bf16)
        r1 = lf - hi.astype(f32)
        mid = r1.astype(bf16)
        lo = (r1 - mid.astype(f32)).astype(bf16)
        cs = _dot(jnp.concatenate([hi, mid, lo], axis=0), tri_ref[...])
        out = cs[:N_HEADS] + cs[N_HEADS:2 * N_HEADS] + cs[2 * N_HEADS:] + carry
        o_ref[0, :, sl] = out
        carry = out[:, SCAN_TILE - 1:SCAN_TILE]


def _cumsum_lanes(lft, tri):
    b, h, t = lft.shape
    blk = pl.BlockSpec((1, h, t), lambda i: (i, 0, 0))
    return pl.pallas_call(
        _scan_kernel, grid=(b,), in_specs=[blk, _const_spec((SCAN_TILE, SCAN_TILE))], out_specs=blk,
        out_shape=jax.ShapeDtypeStruct((b, h, t), f32), name="cumsum_lanes",
        compiler_params=_params(1))(lft, tri)


def _softmax_step(s, v2, m_ref, l_ref, acc_ref):
    m_prev = m_ref[...]
    m_new = jnp.maximum(m_prev, jnp.max(s, axis=1, keepdims=True))
    alpha = jnp.exp(m_prev - m_new)
    p = jnp.exp(s - m_new[:, :1])
    l_ref[...] = alpha * l_ref[...] + jnp.sum(p, axis=1, keepdims=True)
    acc_ref[...] = alpha * acc_ref[...] + _dot(p.astype(bf16), v2)
    m_ref[...] = m_new


def _attn_kernel(q_ref, k_ref, v_ref, cq_ref, ck_ref, o_ref, m_ref, l_ref, acc_ref):
    i = pl.program_id(1)
    tq = q_ref.shape[1]
    tk = ck_ref.shape[3]
    low = lax.broadcasted_iota(jnp.int32, (1, LANES), 1) < HEAD_DIM
    causal = (lax.broadcasted_iota(jnp.int32, (tq, tk), 1)
              <= lax.broadcasted_iota(jnp.int32, (tq, tk), 0))
    for hp in range(N_HEADS // 2):
        lanes = slice(hp * LANES, (hp + 1) * LANES)
        q2 = q_ref[0, :, lanes]
        halves = []
        for sub in range(2):
            h = 2 * hp + sub
            qm = jnp.where(low if sub == 0 else jnp.logical_not(low), q2, jnp.zeros_like(q2))
            cq = cq_ref[0, :, h:h + 1]
            m_ref[...] = jnp.full(m_ref.shape, NEG_INF, f32)
            l_ref[...] = jnp.zeros(l_ref.shape, f32)
            acc_ref[...] = jnp.zeros(acc_ref.shape, f32)

            def step(j, masked, qm=qm, cq=cq, h=h, lanes=lanes):
                rows = pl.ds(pl.multiple_of(j * tk, tk), tk)
                k2 = k_ref[0, rows, lanes]
                v2 = v_ref[0, rows, lanes]
                ck = ck_ref[0, h, pl.ds(j, 1), :]
                s = _dot_nt(qm, k2) + cq - ck
                if masked:
                    s = jnp.where(causal, s, NEG_INF)
                _softmax_step(s, v2, m_ref, l_ref, acc_ref)

            def body(j, carry):
                step(j, False)
                return carry

            lax.fori_loop(0, i, body, 0)
            step(i, True)
            halves.append(acc_ref[...] / l_ref[...])
        o_ref[0, :, lanes] = jnp.where(low, halves[0], halves[1]).astype(bf16)


def _attn_prompt(qb, kb, vb, c_nat, c_t):
    b, s, _ = qb.shape
    tq = ATT_TILE
    qblk = pl.BlockSpec((1, tq, D_MODEL), lambda i, j: (i, j, 0))
    kvblk = pl.BlockSpec((1, s, D_MODEL), lambda i, j: (i, 0, 0))
    return pl.pallas_call(
        _attn_kernel, grid=(b, s // tq),
        in_specs=[qblk, kvblk, kvblk,
                  pl.BlockSpec((1, tq, N_HEADS), lambda i, j: (i, j, 0)),
                  pl.BlockSpec((1, N_HEADS, s // tq, tq), lambda i, j: (i, 0, 0, 0))],
        out_specs=qblk,
        out_shape=jax.ShapeDtypeStruct((b, s, D_MODEL), bf16),
        scratch_shapes=[pltpu.VMEM((tq, LANES), f32)] * 3,
        name="attn_prompt", compiler_params=_params(2))(qb, kb, vb, c_nat, c_t.reshape(b, N_HEADS, s // tq, tq))


def _attn_sample_kernel(q_ref, kn_ref, vn_ref, kp_ref, vp_ref, cq_ref, ck_ref, o_ref):
    t = q_ref.shape[1]
    past = kp_ref.shape[1]
    low = lax.broadcasted_iota(jnp.int32, (1, LANES), 1) < HEAD_DIM
    causal = (lax.broadcasted_iota(jnp.int32, (t, t), 1) <= lax.broadcasted_iota(jnp.int32, (t, t), 0))
    for hp in range(N_HEADS // 2):
        lanes = slice(hp * LANES, (hp + 1) * LANES)
        q2 = q_ref[0, :, lanes]
        kp2 = kp_ref[0, :, lanes].astype(bf16)
        vp2 = vp_ref[0, :, lanes].astype(bf16)
        kn2 = kn_ref[0, :, lanes]
        vn2 = vn_ref[0, :, lanes]
        halves = []
        for sub in range(2):
            h = 2 * hp + sub
            qm = jnp.where(low if sub == 0 else jnp.logical_not(low), q2, jnp.zeros_like(q2))
            cq = cq_ref[0, :, h:h + 1]
            s_past = _dot_nt(qm, kp2) + cq - ck_ref[0, h:h + 1, :past]
            s_new = _dot_nt(qm, kn2) + cq - ck_ref[0, h:h + 1, past:past + t]
            s_new = jnp.where(causal, s_new, NEG_INF)
            m = jnp.maximum(jnp.max(s_past, axis=1, keepdims=True), jnp.max(s_new, axis=1, keepdims=True))
            p_past = jnp.exp(s_past - m)
            p_new = jnp.exp(s_new - m)
            l = jnp.sum(p_past, axis=1, keepdims=True) + jnp.sum(p_new, axis=1, keepdims=True)
            acc = _dot(p_past.astype(bf16), vp2) + _dot(p_new.astype(bf16), vn2)
            halves.append(acc / l)
        o_ref[0, :, lanes] = jnp.where(low, halves[0], halves[1]).astype(bf16)


def _attn_sample(qb, kb, vb, past_k, past_v, c_nat, c_t):
    b, t, _ = qb.shape
    past = past_k.shape[1]
    new = pl.BlockSpec((1, t, D_MODEL), lambda i: (i, 0, 0))
    old = pl.BlockSpec((1, past, D_MODEL), lambda i: (i, 0, 0))
    return pl.pallas_call(
        _attn_sample_kernel, grid=(b,),
        in_specs=[new, new, new, old, old,
                  pl.BlockSpec((1, t, N_HEADS), lambda i: (i, 0, 0)),
                  pl.BlockSpec((1, N_HEADS, c_t.shape[2]), lambda i: (i, 0, 0))],
        out_specs=new,
        out_shape=jax.ShapeDtypeStruct((b, t, D_MODEL), bf16),
        name="attn_sample", compiler_params=_params(1))(qb, kb, vb, past_k, past_v, c_nat, c_t)


def kernel(x_prompt, x_sample, state_conv, cache_k, cache_v, cache_logf, norm_ffn, ffn_w_in, ffn_w_out, norm_mix, conv_w_in, conv_w, conv_w_out, attn_w_in, attn_b_f, q_norm, k_norm, attn_w_out):
    bp, sp, d = x_prompt.shape
    bs, ss, _ = x_sample.shape
    past = cache_k.shape[2]

    def ffn_weights(i, j):
        w_in = ffn_w_in[i, j].astype(bf16)
        return (norm_ffn[i, j].reshape(1, d), w_in[:, :D_FF], w_in[:, D_FF:], ffn_w_out[i, j].astype(bf16))

    yp = x_prompt.reshape(bp * sp, d)
    ys = x_sample.reshape(bs * ss, d)

    w = ffn_weights(0, 0)
    yp, ys = _ffn(yp, *w), _ffn(ys, *w)
    cw = (norm_mix[0].reshape(1, d), conv_w_in[0].astype(bf16), conv_w[0], conv_w_out[0].astype(bf16))
    yp, conv_p = _conv_mixer(yp.reshape(bp, sp, d), jnp.zeros((bp, CONV_WIDTH - 1, d), f32), *cw)
    ys, conv_s = _conv_mixer(ys.reshape(bs, ss, d), state_conv[0], *cw)
    w = ffn_weights(0, 1)
    yp, ys = _ffn(yp.reshape(bp * sp, d), *w), _ffn(ys.reshape(bs * ss, d), *w)

    w = ffn_weights(1, 0)
    yp, ys = _ffn(yp, *w), _ffn(ys, *w)

    w_in = attn_w_in[0]
    w_all = jnp.concatenate([w_in, jnp.zeros((d, F_PAD - N_HEADS), f32)], axis=1).astype(bf16)
    wft = w_in[:, 3 * d:].T.astype(bf16)
    head_of = jnp.arange(d) // HEAD_DIM
    seg = (head_of[:, None] == jnp.arange(LANES)[None, :]).astype(f32) * (1.0 / HEAD_DIM)
    expand = (jnp.arange(LANES)[:, None] == head_of[None, :]).astype(bf16)
    pw = (norm_mix[1].reshape(1, d), w_all, wft, attn_b_f[0].reshape(N_HEADS, 1), attn_b_f[0].reshape(1, N_HEADS),
          seg.astype(bf16), jnp.concatenate([expand, expand], axis=0),
          jnp.tile(q_norm[0], N_HEADS).reshape(1, d), jnp.tile(k_norm[0], N_HEADS).reshape(1, d))
    kp, vp, fp, fpt, qpb, kpb, vpb = _attn_proj(yp.reshape(bp, sp, d), *pw)
    kq, vq, fq, fqt, qsb, ksb, vsb = _attn_proj(ys.reshape(bs, ss, d), *pw)

    tri = jnp.triu(jnp.ones((SCAN_TILE, SCAN_TILE), f32)).astype(bf16)
    cpt = _cumsum_lanes(fpt, tri)
    pad = (-(past + ss)) % SCAN_TILE
    lf_all = jnp.concatenate([jnp.swapaxes(cache_logf[0], 1, 2), fqt, jnp.zeros((bs, N_HEADS, pad), f32)], axis=2)
    cst = _cumsum_lanes(lf_all, tri)

    ap = _attn_prompt(qpb, kpb, vpb, jnp.swapaxes(cpt, 1, 2), cpt)
    a_s = _attn_sample(qsb, ksb, vsb, cache_k[0].reshape(bs, past, d), cache_v[0].reshape(bs, past, d),
                       jnp.swapaxes(cst[:, :, past:past + ss], 1, 2), cst)

    w = ffn_weights(1, 1)
    wm = attn_w_out[0].astype(bf16)
    yp = _ffn(yp, *w, mix=(ap.reshape(bp * sp, d), wm))
    ys = _ffn(ys, *w, mix=(a_s.reshape(bs * ss, d), wm))

    hd = (N_HEADS, HEAD_DIM)
    return (yp.reshape(bp, sp, d), ys.reshape(bs, ss, d),
            conv_p[None], conv_s[None],
            kp.reshape(1, bp, sp, *hd), vp.reshape(1, bp, sp, *hd), fp[None],
            kq.reshape(1, bs, ss, *hd), vq.reshape(1, bs, ss, *hd), fq[None])
```

```python
import jax
import jax.numpy as jnp
from jax import lax
from jax.experimental import pallas as pl
from jax.experimental.pallas import tpu as pltpu

D_MODEL = 1024
N_HEADS = 16
HEAD_DIM = 64
D_FF = 2816
CONV_WIDTH = 3
NORM_EPS = 1e-6
NEG_INF = -1e30
FFN_RESIDUAL = 0.5

LANES = 128
SUBLANES = 8
F_PAD = LANES
VMEM_LIMIT = 56 * 1024 * 1024
FF_CHUNKS = ((0, 1536), (1536, 2816))
ROW_TILE = 512
ATT_TILE = 256
ATT_HEADS = 8
ATT_SKEW = 5
SCAN_TILE = 256
N_AUG = 3

f32 = jnp.float32
bf16 = jnp.bfloat16


def _dot(a, b):
    return jnp.dot(a, b, preferred_element_type=f32)


def _dot_nt(a, b):
    return lax.dot_general(a, b, (((1,), (1,)), ((), ())), preferred_element_type=f32)


def _rms(x, g):
    return x * lax.rsqrt(jnp.mean(x * x, axis=-1, keepdims=True) + NORM_EPS) * g


def _log_sigmoid(z):
    return -(jnp.maximum(-z, 0.0) + jnp.log1p(jnp.exp(-jnp.abs(z))))


def _split3(x):
    hi = x.astype(bf16)
    r1 = x - hi.astype(f32)
    mid = r1.astype(bf16)
    lo = (r1 - mid.astype(f32)).astype(bf16)
    return hi, mid, lo


def _params(n_axes, flags=None):
    return pltpu.CompilerParams(dimension_semantics=("arbitrary",) * n_axes,
                                vmem_limit_bytes=VMEM_LIMIT, flags=flags)


def _const_spec(shape):
    return pl.BlockSpec(shape, lambda *_: (0,) * len(shape), pipeline_mode=pl.Buffered(1))


def _ffn_body(x, g_ref, wg_ref, wu_ref, wo_ref):
    xn = _rms(x, g_ref[...]).astype(bf16)
    acc = None
    for lo, hi in FF_CHUNKS:
        a = _dot(xn, wg_ref[:, lo:hi])
        b = _dot(xn, wu_ref[:, lo:hi])
        h = (a * jax.nn.sigmoid(a) * b).astype(bf16)
        y = _dot(h, wo_ref[lo:hi, :])
        acc = y if acc is None else acc + y
    return x + FFN_RESIDUAL * acc


def _ffn_kernel(x_ref, g_ref, wg_ref, wu_ref, wo_ref, o_ref):
    o_ref[...] = _ffn_body(x_ref[...], g_ref, wg_ref, wu_ref, wo_ref)


def _mix_ffn_kernel(x_ref, a_ref, wm_ref, g_ref, wg_ref, wu_ref, wo_ref, o_ref):
    x = x_ref[...] + _dot(a_ref[...], wm_ref[...])
    o_ref[...] = _ffn_body(x, g_ref, wg_ref, wu_ref, wo_ref)


def _ffn(x, g, wg, wu, wo, mix=None):
    m = x.shape[0]
    tm = min(ROW_TILE, m)
    row = pl.BlockSpec((tm, D_MODEL), lambda i: (i, 0))
    w_specs = [_const_spec((1, D_MODEL)), _const_spec((D_MODEL, D_FF)),
               _const_spec((D_MODEL, D_FF)), _const_spec((D_FF, D_MODEL))]
    if mix is None:
        kern, ins, specs = _ffn_kernel, (x, g, wg, wu, wo), [row] + w_specs
    else:
        a, wm = mix
        kern, ins = _mix_ffn_kernel, (x, a, wm, g, wg, wu, wo)
        specs = [row, row, _const_spec((D_MODEL, D_MODEL))] + w_specs
    return pl.pallas_call(
        kern, grid=(m // tm,), in_specs=specs, out_specs=row,
        out_shape=jax.ShapeDtypeStruct((m, D_MODEL), f32),
        name="ffn" if mix is None else "mix_ffn",
        compiler_params=_params(1))(*ins)


def _conv_kernel(x_ref, hist_ref, g_ref, win_ref, wk_ref, wout_ref, o_ref, st_ref, carry_ref):
    t = pl.program_id(1)
    tt = x_ref.shape[1]

    @pl.when(t == 0)
    def _():
        carry_ref[0:2, :] = hist_ref[0]

    x = x_ref[0]
    xn = _rms(x, g_ref[...]).astype(bf16)
    p = _dot(xn, win_ref[...])
    gate_b = p[:, :D_MODEL]
    u = p[:, D_MODEL:2 * D_MODEL] * p[:, 2 * D_MODEL:]
    prev2 = carry_ref[0:1, :]
    prev1 = carry_ref[1:2, :]
    row = lax.broadcasted_iota(jnp.int32, (tt, 1), 0)
    u1 = jnp.where(row == 0, prev1, pltpu.roll(u, 1, 0))
    u2 = jnp.where(row == 0, prev2, jnp.where(row == 1, prev1, pltpu.roll(u, 2, 0)))
    wk = wk_ref[...]
    conv = wk[0:1, :] * u2 + wk[1:2, :] * u1 + wk[2:3, :] * u
    y = _dot((gate_b * conv).astype(bf16), wout_ref[...])
    o_ref[0] = x + y
    last = u[tt - 2:tt, :]
    carry_ref[0:2, :] = last
    st_ref[0] = last


def _conv_mixer(x, hist, g, win, wk, wout):
    b, t, _ = x.shape
    tt = min(ROW_TILE, t)
    blk = pl.BlockSpec((1, tt, D_MODEL), lambda i, j: (i, j, 0))
    st = pl.BlockSpec((1, CONV_WIDTH - 1, D_MODEL), lambda i, j: (i, 0, 0))
    return pl.pallas_call(
        _conv_kernel, grid=(b, t // tt),
        in_specs=[blk, st, _const_spec((1, D_MODEL)), _const_spec((D_MODEL, 3 * D_MODEL)),
                  _const_spec((CONV_WIDTH, D_MODEL)), _const_spec((D_MODEL, D_MODEL))],
        out_specs=[blk, st],
        out_shape=[jax.ShapeDtypeStruct((b, t, D_MODEL), f32),
                   jax.ShapeDtypeStruct((b, CONV_WIDTH - 1, D_MODEL), f32)],
        scratch_shapes=[pltpu.VMEM((8, D_MODEL), f32)],
        name="conv_mixer", compiler_params=_params(2))(x, hist, g, win, wk, wout)


def _head_norm(t, gain, seg_ref, exp_ref):
    ms = _dot((t * t).astype(bf16), seg_ref[...])
    r = lax.rsqrt(ms + NORM_EPS)
    r_hi = r.astype(bf16)
    r_lo = (r - r_hi.astype(f32)).astype(bf16)
    rb = _dot(jnp.concatenate([r_hi, r_lo], axis=1), exp_ref[...])
    return t * rb * gain


def _proj_kernel(x_ref, g_ref, w_ref, wft_ref, bcol_ref, brow_ref, seg_ref, exp_ref, gq_ref, gk_ref,
                 k_ref, v_ref, lf_ref, lft_ref, qb_ref, kb_ref, vb_ref):
    xn = _rms(x_ref[0], g_ref[...]).astype(bf16)
    p = _dot(xn, w_ref[...])
    q = _head_norm(p[:, :D_MODEL], gq_ref[...], seg_ref, exp_ref)
    k = _head_norm(p[:, D_MODEL:2 * D_MODEL], gk_ref[...], seg_ref, exp_ref)
    v = p[:, 2 * D_MODEL:3 * D_MODEL]
    k_ref[0] = k
    v_ref[0] = v
    lf_ref[0] = _log_sigmoid(p[:, 3 * D_MODEL:3 * D_MODEL + N_HEADS] + brow_ref[...])
    lft_ref[0] = _log_sigmoid(_dot_nt(wft_ref[...], xn) + bcol_ref[...])
    qb_ref[0] = (q * (HEAD_DIM ** -0.5)).astype(bf16)
    kb_ref[0] = k.astype(bf16)
    vb_ref[0] = v.astype(bf16)


def _attn_proj(x, g, w, wft, bcol, brow, seg, exp, gq, gk):
    b, t, _ = x.shape
    tt = min(ROW_TILE, t)
    blk = pl.BlockSpec((1, tt, D_MODEL), lambda i, j: (i, j, 0))
    lf = pl.BlockSpec((1, tt, N_HEADS), lambda i, j: (i, j, 0))
    lft = pl.BlockSpec((1, N_HEADS, tt), lambda i, j: (i, 0, j))
    big = jax.ShapeDtypeStruct((b, t, D_MODEL), f32)
    half = jax.ShapeDtypeStruct((b, t, D_MODEL), bf16)
    return pl.pallas_call(
        _proj_kernel, grid=(b, t // tt),
        in_specs=[blk, _const_spec((1, D_MODEL)), _const_spec((D_MODEL, 3 * D_MODEL + F_PAD)),
                  _const_spec((N_HEADS, D_MODEL)), _const_spec((N_HEADS, 1)), _const_spec((1, N_HEADS)),
                  _const_spec((D_MODEL, LANES)), _const_spec((2 * LANES, D_MODEL)),
                  _const_spec((1, D_MODEL)), _const_spec((1, D_MODEL))],
        out_specs=[blk, blk, lf, lft, blk, blk, blk],
        out_shape=[big, big, jax.ShapeDtypeStruct((b, t, N_HEADS), f32),
                   jax.ShapeDtypeStruct((b, N_HEADS, t), f32), half, half, half],
        name="attn_proj", compiler_params=_params(2))(x, g, w, wft, bcol, brow, seg, exp, gq, gk)


def _proj_prompt_kernel(x_ref, g_ref, w_ref, wqt_ref, wft_ref, bcol_ref, brow_ref, seg_ref, exp_ref,
                        gqc_ref, gk_ref, tril_ref, triu_ref,
                        k_ref, v_ref, lf_ref, ka_ref, qat_ref, vt_ref, crow_ref, ccol_ref):
    t = pl.program_id(1)
    tt = x_ref.shape[1]
    n_tiles = tt // ATT_TILE

    @pl.when(t == 0)
    def _():
        crow_ref[...] = jnp.zeros(crow_ref.shape, f32)
        ccol_ref[...] = jnp.zeros(ccol_ref.shape, f32)

    xn = _rms(x_ref[0], g_ref[...]).astype(bf16)
    p = _dot(xn, w_ref[...])
    k = _head_norm(p[:, :D_MODEL], gk_ref[...], seg_ref, exp_ref)
    v = p[:, D_MODEL:2 * D_MODEL]
    k_ref[0] = k
    v_ref[0] = v
    lf = _log_sigmoid(p[:, 2 * D_MODEL:] + brow_ref[...])
    lf_ref[0] = lf[:, :N_HEADS]

    c3 = _dot(tril_ref[...], jnp.concatenate(_split3(lf), axis=1))
    c = c3[:, :LANES] + c3[:, LANES:2 * LANES] + c3[:, 2 * LANES:] + crow_ref[0:1, :]
    crow_ref[0:1, :] = c[tt - 1:tt, :]
    neg_c = [-piece.astype(f32) for piece in _split3(c)]

    lft = _log_sigmoid(_dot_nt(wft_ref[...], xn) + bcol_ref[...])
    ct3 = _dot(jnp.concatenate(_split3(lft), axis=0), triu_ref[...])
    ct = ct3[:N_HEADS] + ct3[N_HEADS:2 * N_HEADS] + ct3[2 * N_HEADS:] + ccol_ref[:, 0:1]
    ccol_ref[...] = jnp.broadcast_to(ct[:, tt - 1:tt], ccol_ref.shape)
    ct_pieces = [piece.astype(f32) for piece in _split3(ct)]

    qt = _dot_nt(wqt_ref[...], xn)
    vt = v.T.astype(bf16)
    for j in range(n_tiles):
        vt_ref[0, j] = vt[:, j * ATT_TILE:(j + 1) * ATT_TILE]

    lane = lax.broadcasted_iota(jnp.int32, (1, LANES), 1)
    sub = lax.broadcasted_iota(jnp.int32, (SUBLANES, 1), 0)
    gqc = gqc_ref[...]
    for hp in range(N_HEADS // 2):
        k2 = k[:, hp * LANES:(hp + 1) * LANES]
        for s in range(2):
            h = 2 * hp + s
            base = k2 if s == 0 else pltpu.roll(k2, HEAD_DIM, 1)
            ext = jnp.zeros((tt, LANES), f32)
            for n in range(N_AUG):
                ext = jnp.where(lane == HEAD_DIM + N_AUG + n,
                                jnp.broadcast_to(neg_c[n][:, h:h + 1], (tt, LANES)), ext)
            ext = jnp.where(lane < HEAD_DIM + N_AUG, 1.0, ext)
            ka_ref[0, h] = jnp.where(lane < HEAD_DIM, base, ext).astype(bf16)

            qh = qt[h * HEAD_DIM:(h + 1) * HEAD_DIM, :]
            qn = qh * lax.rsqrt(jnp.mean(qh * qh, axis=0, keepdims=True) + NORM_EPS) * gqc
            aug = jnp.where(sub < 2 * N_AUG, 1.0, 0.0) * jnp.ones((1, tt), f32)
            for n in range(N_AUG):
                aug = jnp.where(sub == n, ct_pieces[n][h:h + 1, :], aug)
            blk = jnp.concatenate([qn * (HEAD_DIM ** -0.5), aug,
                                   jnp.zeros((LANES - HEAD_DIM - SUBLANES, tt), f32)], axis=0).astype(bf16)
            for j in range(n_tiles):
                qat_ref[0, j, h * LANES:(h + 1) * LANES, :] = blk[:, j * ATT_TILE:(j + 1) * ATT_TILE]


def _attn_proj_prompt(x, g, w, wqt, wft, bcol, brow, seg, exp, gqc, gk, tril, triu):
    b, t, _ = x.shape
    tt = ROW_TILE
    nt = tt // ATT_TILE
    blk = pl.BlockSpec((1, tt, D_MODEL), lambda i, j: (i, j, 0))
    big = jax.ShapeDtypeStruct((b, t, D_MODEL), f32)
    return pl.pallas_call(
        _proj_prompt_kernel, grid=(b, t // tt),
        in_specs=[blk, _const_spec((1, D_MODEL)), _const_spec((D_MODEL, 2 * D_MODEL + F_PAD)),
                  _const_spec((D_MODEL, D_MODEL)), _const_spec((N_HEADS, D_MODEL)),
                  _const_spec((N_HEADS, 1)), _const_spec((1, F_PAD)),
                  _const_spec((D_MODEL, LANES)), _const_spec((2 * LANES, D_MODEL)),
                  _const_spec((HEAD_DIM, tt)), _const_spec((1, D_MODEL)),
                  _const_spec((tt, tt)), _const_spec((tt, tt))],
        out_specs=[blk, blk,
                   pl.BlockSpec((1, tt, N_HEADS), lambda i, j: (i, j, 0)),
                   pl.BlockSpec((1, N_HEADS, tt, LANES), lambda i, j: (i, 0, j, 0)),
                   pl.BlockSpec((1, nt, N_HEADS * LANES, ATT_TILE), lambda i, j: (i, j, 0, 0)),
                   pl.BlockSpec((1, nt, D_MODEL, ATT_TILE), lambda i, j: (i, j, 0, 0))],
        out_shape=[big, big, jax.ShapeDtypeStruct((b, t, N_HEADS), f32),
                   jax.ShapeDtypeStruct((b, N_HEADS, t, LANES), bf16),
                   jax.ShapeDtypeStruct((b, t // ATT_TILE, N_HEADS * LANES, ATT_TILE), bf16),
                   jax.ShapeDtypeStruct((b, t // ATT_TILE, D_MODEL, ATT_TILE), bf16)],
        scratch_shapes=[pltpu.VMEM((SUBLANES, LANES), f32), pltpu.VMEM((N_HEADS, LANES), f32)],
        name="attn_proj_prompt", compiler_params=_params(2))(
            x, g, w, wqt, wft, bcol, brow, seg, exp, gqc, gk, tril, triu)


def _scan_kernel(x_ref, tri_ref, o_ref):
    n = x_ref.shape[2] // SCAN_TILE
    carry = jnp.zeros((N_HEADS, 1), f32)
    for c in range(n):
        sl = slice(c * SCAN_TILE, (c + 1) * SCAN_TILE)
        cs = _dot(jnp.concatenate(_split3(x_ref[0, :, sl]), axis=0), tri_ref[...])
        out = cs[:N_HEADS] + cs[N_HEADS:2 * N_HEADS] + cs[2 * N_HEADS:] + carry
        o_ref[0, :, sl] = out
        carry = out[:, SCAN_TILE - 1:SCAN_TILE]


def _cumsum_lanes(lft, tri):
    b, h, t = lft.shape
    blk = pl.BlockSpec((1, h, t), lambda i: (i, 0, 0))
    return pl.pallas_call(
        _scan_kernel, grid=(b,), in_specs=[blk, _const_spec((SCAN_TILE, SCAN_TILE))], out_specs=blk,
        out_shape=jax.ShapeDtypeStruct((b, h, t), f32), name="cumsum_lanes",
        compiler_params=_params(1))(lft, tri)


def _attn_kernel(qat_ref, ka_ref, vt_ref, o_ref, m_ref, l_ref, acc_ref):
    i = pl.program_id(2)
    tq = qat_ref.shape[3]
    tk = vt_ref.shape[3]
    causal = (lax.broadcasted_iota(jnp.int32, (tk, tq), 0) <= lax.broadcasted_iota(jnp.int32, (tk, tq), 1))
    m_ref[...] = jnp.full(m_ref.shape, NEG_INF, f32)
    l_ref[...] = jnp.zeros(l_ref.shape, f32)
    acc_ref[...] = jnp.zeros(acc_ref.shape, f32)

    def tile(j, masked):
        rows = pl.ds(pl.multiple_of(j * tk, tk), tk)

        def scores(s):
            return _dot(ka_ref[0, s, rows, :], qat_ref[0, 0, s * LANES:(s + 1) * LANES, :])

        ahead = [scores(s) for s in range(ATT_SKEW)]
        for s in range(ATT_HEADS):
            sc = ahead.pop(0)
            if s + ATT_SKEW < ATT_HEADS:
                ahead.append(scores(s + ATT_SKEW))
            if masked:
                sc = jnp.where(causal, sc, NEG_INF)
            m = m_ref[s]
            m_new = jnp.maximum(m, jnp.max(sc, axis=0, keepdims=True))
            alpha = jnp.exp(m - m_new)
            pr = jnp.exp(sc - m_new)
            l_ref[s] = alpha * l_ref[s] + jnp.sum(pr, axis=0, keepdims=True)
            vt = vt_ref[0, j, s * HEAD_DIM:(s + 1) * HEAD_DIM, :]
            acc_ref[s] = alpha * acc_ref[s] + _dot(vt, pr.astype(bf16))
            m_ref[s] = m_new

    def body(j, carry):
        tile(j, False)
        return carry

    lax.fori_loop(0, i, body, 0)
    tile(i, True)
    o_t = jnp.concatenate([acc_ref[s] / l_ref[s] for s in range(ATT_HEADS)], axis=0)
    o_ref[0] = o_t.T.astype(bf16)


def _attn_prompt(qat, ka, vt):
    b, nq, _, tq = qat.shape
    s = nq * tq
    g = ATT_HEADS
    return pl.pallas_call(
        _attn_kernel, grid=(b, N_HEADS // g, nq),
        in_specs=[pl.BlockSpec((1, 1, g * LANES, tq), lambda i, h, j: (i, j, h, 0)),
                  pl.BlockSpec((1, g, s, LANES), lambda i, h, j: (i, h, 0, 0)),
                  pl.BlockSpec((1, nq, g * HEAD_DIM, tq), lambda i, h, j: (i, 0, h, 0))],
        out_specs=pl.BlockSpec((1, tq, g * HEAD_DIM), lambda i, h, j: (i, j, h)),
        out_shape=jax.ShapeDtypeStruct((b, s, D_MODEL), bf16),
        scratch_shapes=[pltpu.VMEM((g, 1, tq), f32), pltpu.VMEM((g, 1, tq), f32),
                        pltpu.VMEM((g, HEAD_DIM, tq), f32)],
        name="attn_prompt",
        compiler_params=_params(3))(qat, ka, vt)


def _attn_sample_kernel(q_ref, kn_ref, vn_ref, kp_ref, vp_ref, cq_ref, ck_ref, o_ref):
    t = q_ref.shape[1]
    past = kp_ref.shape[1]
    low = lax.broadcasted_iota(jnp.int32, (1, LANES), 1) < HEAD_DIM
    causal = (lax.broadcasted_iota(jnp.int32, (t, t), 1) <= lax.broadcasted_iota(jnp.int32, (t, t), 0))
    for hp in range(N_HEADS // 2):
        lanes = slice(hp * LANES, (hp + 1) * LANES)
        q2 = q_ref[0, :, lanes]
        kp2 = kp_ref[0, :, lanes].astype(bf16)
        vp2 = vp_ref[0, :, lanes].astype(bf16)
        kn2 = kn_ref[0, :, lanes]
        vn2 = vn_ref[0, :, lanes]
        halves = []
        for sub in range(2):
            h = 2 * hp + sub
            qm = jnp.where(low if sub == 0 else jnp.logical_not(low), q2, jnp.zeros_like(q2))
            cq = cq_ref[0, :, h:h + 1]
            s_past = _dot_nt(qm, kp2) + cq - ck_ref[0, h:h + 1, :past]
            s_new = _dot_nt(qm, kn2) + cq - ck_ref[0, h:h + 1, past:past + t]
            s_new = jnp.where(causal, s_new, NEG_INF)
            m = jnp.maximum(jnp.max(s_past, axis=1, keepdims=True), jnp.max(s_new, axis=1, keepdims=True))
            p_past = jnp.exp(s_past - m)
            p_new = jnp.exp(s_new - m)
            l = jnp.sum(p_past, axis=1, keepdims=True) + jnp.sum(p_new, axis=1, keepdims=True)
            acc = _dot(p_past.astype(bf16), vp2) + _dot(p_new.astype(bf16), vn2)
            halves.append(acc / l)
        o_ref[0, :, lanes] = jnp.where(low, halves[0], halves[1]).astype(bf16)


def _attn_sample(qb, kb, vb, past_k, past_v, c_nat, c_t):
    b, t, _ = qb.shape
    past = past_k.shape[1]
    new = pl.BlockSpec((1, t, D_MODEL), lambda i: (i, 0, 0))
    old = pl.BlockSpec((1, past, D_MODEL), lambda i: (i, 0, 0))
    return pl.pallas_call(
        _attn_sample_kernel, grid=(b,),
        in_specs=[new, new, new, old, old,
                  pl.BlockSpec((1, t, N_HEADS), lambda i: (i, 0, 0)),
                  pl.BlockSpec((1, N_HEADS, c_t.shape[2]), lambda i: (i, 0, 0))],
        out_specs=new,
        out_shape=jax.ShapeDtypeStruct((b, t, D_MODEL), bf16),
        name="attn_sample", compiler_params=_params(1))(qb, kb, vb, past_k, past_v, c_nat, c_t)


def kernel(x_prompt, x_sample, state_conv, cache_k, cache_v, cache_logf, norm_ffn, ffn_w_in, ffn_w_out, norm_mix, conv_w_in, conv_w, conv_w_out, attn_w_in, attn_b_f, q_norm, k_norm, attn_w_out):
    bp, sp, d = x_prompt.shape
    bs, ss, _ = x_sample.shape
    past = cache_k.shape[2]

    def ffn_weights(i, j):
        w_in = ffn_w_in[i, j].astype(bf16)
        return (norm_ffn[i, j].reshape(1, d), w_in[:, :D_FF], w_in[:, D_FF:], ffn_w_out[i, j].astype(bf16))

    yp = x_prompt.reshape(bp * sp, d)
    ys = x_sample.reshape(bs * ss, d)

    w = ffn_weights(0, 0)
    yp, ys = _ffn(yp, *w), _ffn(ys, *w)
    cw = (norm_mix[0].reshape(1, d), conv_w_in[0].astype(bf16), conv_w[0], conv_w_out[0].astype(bf16))
    yp, conv_p = _conv_mixer(yp.reshape(bp, sp, d), jnp.zeros((bp, CONV_WIDTH - 1, d), f32), *cw)
    ys, conv_s = _conv_mixer(ys.reshape(bs, ss, d), state_conv[0], *cw)
    w = ffn_weights(0, 1)
    yp, ys = _ffn(yp.reshape(bp * sp, d), *w), _ffn(ys.reshape(bs * ss, d), *w)

    w = ffn_weights(1, 0)
    yp, ys = _ffn(yp, *w), _ffn(ys, *w)

    w_in = attn_w_in[0]
    f_pad = jnp.zeros((d, F_PAD - N_HEADS), f32)
    wft = w_in[:, 3 * d:].T.astype(bf16)
    head_of = jnp.arange(d) // HEAD_DIM
    seg = ((head_of[:, None] == jnp.arange(LANES)[None, :]).astype(f32) * (1.0 / HEAD_DIM)).astype(bf16)
    expand = (jnp.arange(LANES)[:, None] == head_of[None, :]).astype(bf16)
    expand2 = jnp.concatenate([expand, expand], axis=0)
    g_mix = norm_mix[1].reshape(1, d)
    bcol = attn_b_f[0].reshape(N_HEADS, 1)
    gk = jnp.tile(k_norm[0], N_HEADS).reshape(1, d)

    tril = jnp.tril(jnp.ones((ROW_TILE, ROW_TILE), f32)).astype(bf16)
    kp, vp, fp, ka, qat, vt = _attn_proj_prompt(
        yp.reshape(bp, sp, d), g_mix, jnp.concatenate([w_in[:, d:], f_pad], axis=1).astype(bf16),
        w_in[:, :d].T.astype(bf16), wft, bcol,
        jnp.concatenate([attn_b_f[0], jnp.zeros((F_PAD - N_HEADS,), f32)]).reshape(1, F_PAD),
        seg, expand2, jnp.broadcast_to(q_norm[0][:, None], (HEAD_DIM, ROW_TILE)), gk, tril, tril.T)
    ap = _attn_prompt(qat, ka, vt)

    kq, vq, fq, fqt, qsb, ksb, vsb = _attn_proj(
        ys.reshape(bs, ss, d), g_mix, jnp.concatenate([w_in, f_pad], axis=1).astype(bf16), wft, bcol,
        attn_b_f[0].reshape(1, N_HEADS), seg, expand2, jnp.tile(q_norm[0], N_HEADS).reshape(1, d), gk)
    pad = (-(past + ss)) % SCAN_TILE
    lf_all = jnp.concatenate([jnp.swapaxes(cache_logf[0], 1, 2), fqt, jnp.zeros((bs, N_HEADS, pad), f32)], axis=2)
    cst = _cumsum_lanes(lf_all, jnp.triu(jnp.ones((SCAN_TILE, SCAN_TILE), f32)).astype(bf16))
    a_s = _attn_sample(qsb, ksb, vsb, cache_k[0].reshape(bs, past, d), cache_v[0].reshape(bs, past, d),
                       jnp.swapaxes(cst[:, :, past:past + ss], 1, 2), cst)

    w = ffn_weights(1, 1)
    wm = attn_w_out[0].astype(bf16)
    yp = _ffn(yp, *w, mix=(ap.reshape(bp * sp, d), wm))
    ys = _ffn(ys, *w, mix=(a_s.reshape(bs * ss, d), wm))

    hd = (N_HEADS, HEAD_DIM)
    return (yp.reshape(bp, sp, d), ys.reshape(bs, ss, d),
            conv_p[None], conv_s[None],
            kp.reshape(1, bp, sp, *hd), vp.reshape(1, bp, sp, *hd), fp[None],
            kq.reshape(1, bs, ss, *hd), vq.reshape(1, bs, ss, *hd), fq[None])
```

```python
import functools

import jax
import jax.numpy as jnp
from jax import lax
from jax.experimental import pallas as pl
from jax.experimental.pallas import tpu as pltpu

D_MODEL = 1024
N_HEADS = 16
HEAD_DIM = 64
D_FF = 2816
CONV_WIDTH = 3
NORM_EPS = 1e-6
NEG_INF = -1e30
FFN_RESIDUAL = 0.5

LANES = 128
SUBLANES = 8
F_PAD = LANES
VMEM_LIMIT = 56 * 1024 * 1024
FF_CHUNKS = ((0, 1536), (1536, 2816))
ROW_TILE = 512
ATT_TILE = 256
ATT_HEADS = 8
ATT_SKEW = 5
SCAN_TILE = 256
N_AUG = 3

f32 = jnp.float32
bf16 = jnp.bfloat16


def _dot(a, b):
    return jnp.dot(a, b, preferred_element_type=f32)


def _dot_nt(a, b):
    return lax.dot_general(a, b, (((1,), (1,)), ((), ())), preferred_element_type=f32)


def _rms(x, g):
    return x * lax.rsqrt(jnp.mean(x * x, axis=-1, keepdims=True) + NORM_EPS) * g


def _log_sigmoid(z):
    return -(jnp.maximum(-z, 0.0) + jnp.log1p(jnp.exp(-jnp.abs(z))))


def _split3(x):
    hi = x.astype(bf16)
    r1 = x - hi.astype(f32)
    mid = r1.astype(bf16)
    lo = (r1 - mid.astype(f32)).astype(bf16)
    return hi, mid, lo


def _params(n_axes, flags=None):
    return pltpu.CompilerParams(dimension_semantics=("arbitrary",) * n_axes,
                                vmem_limit_bytes=VMEM_LIMIT, flags=flags)


def _const_spec(shape):
    return pl.BlockSpec(shape, lambda *_: (0,) * len(shape), pipeline_mode=pl.Buffered(1))


def _ffn_kernel(*refs, n_prompt, split_in, mix):
    refs = list(refs)
    is_prompt = pl.program_id(0) < n_prompt

    def rows():
        if split_in or mix:
            p_ref, s_ref = refs.pop(0), refs.pop(0)
            return jnp.where(is_prompt, p_ref[...], s_ref[...])
        return refs.pop(0)[...]

    if mix:
        x = refs.pop(0)[...]
        a = rows()
        x = x + _dot(a, refs.pop(0)[...])
    else:
        x = rows()
    g_ref, win_ref, wout_ref = refs[:3]
    outs = refs[3:]
    xn = _rms(x, g_ref[...]).astype(bf16)
    acc = None
    for lo, hi in FF_CHUNKS:
        a = _dot(xn, win_ref[:, lo:hi])
        b = _dot(xn, win_ref[:, D_FF + lo:D_FF + hi])
        h = (a * jax.nn.sigmoid(a) * b).astype(bf16)
        y = _dot(h, wout_ref[lo:hi, :])
        acc = y if acc is None else acc + y
    y = x + FFN_RESIDUAL * acc
    if mix:
        @pl.when(is_prompt)
        def _():
            outs[0][...] = y

        @pl.when(jnp.logical_not(is_prompt))
        def _():
            outs[1][...] = y
    else:
        outs[0][...] = y


def _ffn(xs, g, w_in, w_out, layer, mix=None):
    li, lj = layer
    tm = ROW_TILE
    split_in = isinstance(xs, tuple)
    if mix is None and split_in:
        n_prompt, n_sample = xs[0].shape[0] // tm, xs[1].shape[0] // tm
    elif mix is not None:
        n_prompt, n_sample = mix[0].shape[0] // tm, mix[1].shape[0] // tm
    else:
        n_prompt, n_sample = xs.shape[0] // tm, 0
    assert n_sample in (0, 1)
    steps = n_prompt + n_sample
    row = pl.BlockSpec((tm, D_MODEL), lambda i: (i, 0))
    prompt_row = pl.BlockSpec((tm, D_MODEL), lambda i: (jnp.minimum(i, n_prompt - 1), 0))
    sample_row = pl.BlockSpec((tm, D_MODEL), lambda i: (0, 0))
    w_specs = [_const_spec((1, D_MODEL)),
               pl.BlockSpec((None, None, D_MODEL, 2 * D_FF), lambda i: (li, lj, 0, 0), pipeline_mode=pl.Buffered(1)),
               pl.BlockSpec((None, None, D_FF, D_MODEL), lambda i: (li, lj, 0, 0), pipeline_mode=pl.Buffered(1))]
    stacked = jax.ShapeDtypeStruct((steps * tm, D_MODEL), f32)
    if mix is not None:
        a_p, a_s, wm = mix
        ins = (xs, a_p, a_s, wm)
        specs = [row, prompt_row, sample_row, _const_spec((D_MODEL, D_MODEL))]
        out_specs = [prompt_row, sample_row]
        out_shape = [jax.ShapeDtypeStruct((n_prompt * tm, D_MODEL), f32),
                     jax.ShapeDtypeStruct((n_sample * tm, D_MODEL), f32)]
    elif split_in:
        ins, specs, out_specs, out_shape = tuple(xs), [prompt_row, sample_row], row, stacked
    else:
        ins, specs, out_specs, out_shape = (xs,), [row], row, stacked
    kern = functools.partial(_ffn_kernel, n_prompt=n_prompt, split_in=split_in, mix=mix is not None)
    return pl.pallas_call(
        kern, grid=(steps,), in_specs=specs + w_specs, out_specs=out_specs, out_shape=out_shape,
        name="ffn" if mix is None else "mix_ffn",
        compiler_params=_params(1))(*ins, g, w_in, w_out)


def _conv_kernel(x_ref, hist_ref, g_ref, win_ref, wk_ref, wout_ref, *rest):
    o_ref, st_ref, carry_ref = rest[-3:]
    t = pl.program_id(1)
    tt = x_ref.shape[0]

    @pl.when(t == 0)
    def _():
        carry_ref[0:2, :] = hist_ref[0]

    x = x_ref[...]
    xn = _rms(x, g_ref[...]).astype(bf16)
    p = _dot(xn, win_ref[...])
    gate_b = p[:, :D_MODEL]
    u = p[:, D_MODEL:2 * D_MODEL] * p[:, 2 * D_MODEL:]
    prev2 = carry_ref[0:1, :]
    prev1 = carry_ref[1:2, :]
    row = lax.broadcasted_iota(jnp.int32, (tt, 1), 0)
    u1 = jnp.where(row == 0, prev1, pltpu.roll(u, 1, 0))
    u2 = jnp.where(row == 0, prev2, jnp.where(row == 1, prev1, pltpu.roll(u, 2, 0)))
    wk = wk_ref[...]
    conv = wk[0:1, :] * u2 + wk[1:2, :] * u1 + wk[2:3, :] * u
    y = _dot((gate_b * conv).astype(bf16), wout_ref[...])
    o_ref[...] = x + y
    last = u[tt - 2:tt, :]
    carry_ref[0:2, :] = last
    st_ref[0] = last


def _conv_mixer(x, hist, g, win, wk, wout, *, t, row0, into=None):
    b = hist.shape[0]
    tt = min(ROW_TILE, t)
    nt, base = t // tt, row0 // tt
    blk = pl.BlockSpec((tt, D_MODEL), lambda i, j: (base + i * nt + j, 0))
    st = pl.BlockSpec((1, CONV_WIDTH - 1, D_MODEL), lambda i, j: (i, 0, 0))
    ins = [x, hist, g, win, wk, wout]
    specs = [blk, st, _const_spec((1, D_MODEL)), _const_spec((D_MODEL, 3 * D_MODEL)),
             _const_spec((CONV_WIDTH, D_MODEL)), _const_spec((D_MODEL, D_MODEL))]
    aliases = {}
    if into is not None:
        aliases = {len(ins): 0}
        ins.append(into)
        specs.append(pl.BlockSpec(memory_space=pl.ANY))
    return pl.pallas_call(
        _conv_kernel, grid=(b, nt), in_specs=specs, out_specs=[blk, st],
        out_shape=[jax.ShapeDtypeStruct(x.shape, f32),
                   jax.ShapeDtypeStruct((b, CONV_WIDTH - 1, D_MODEL), f32)],
        scratch_shapes=[pltpu.VMEM((8, D_MODEL), f32)], input_output_aliases=aliases,
        name="conv_mixer", compiler_params=_params(2))(*ins)


def _head_norm(t, gain, seg_ref, exp_ref):
    ms = _dot((t * t).astype(bf16), seg_ref[...])
    r = lax.rsqrt(ms + NORM_EPS)
    r_hi = r.astype(bf16)
    r_lo = (r - r_hi.astype(f32)).astype(bf16)
    rb = _dot(jnp.concatenate([r_hi, r_lo], axis=1), exp_ref[...])
    return t * rb * gain


def _proj_kernel(x_ref, g_ref, w_ref, wft_ref, bcol_ref, brow_ref, seg_ref, exp_ref, gq_ref, gk_ref,
                 k_ref, v_ref, lf_ref, lft_ref, qb_ref, kb_ref, vb_ref):
    xn = _rms(x_ref[...], g_ref[...]).astype(bf16)
    p = _dot(xn, w_ref[...])
    q = _head_norm(p[:, :D_MODEL], gq_ref[...], seg_ref, exp_ref)
    k = _head_norm(p[:, D_MODEL:2 * D_MODEL], gk_ref[...], seg_ref, exp_ref)
    v = p[:, 2 * D_MODEL:3 * D_MODEL]
    k_ref[...] = k
    v_ref[...] = v
    lf_ref[...] = _log_sigmoid(p[:, 3 * D_MODEL:3 * D_MODEL + N_HEADS] + brow_ref[...])
    lft_ref[...] = _log_sigmoid(_dot_nt(wft_ref[...], xn) + bcol_ref[...])
    qb_ref[...] = (q * (HEAD_DIM ** -0.5)).astype(bf16)
    kb_ref[...] = k.astype(bf16)
    vb_ref[...] = v.astype(bf16)


def _attn_proj(x, g, w, wft, bcol, brow, seg, exp, gq, gk, *, rows, row0):
    blk = lambda dt: jax.ShapeDtypeStruct((rows, D_MODEL), dt)
    full = lambda shape: pl.BlockSpec(shape, lambda i: (0, 0))
    return pl.pallas_call(
        _proj_kernel, grid=(1,),
        in_specs=[pl.BlockSpec((rows, D_MODEL), lambda i: (row0 // rows, 0)),
                  _const_spec((1, D_MODEL)), _const_spec((D_MODEL, 3 * D_MODEL + F_PAD)),
                  _const_spec((N_HEADS, D_MODEL)), _const_spec((N_HEADS, 1)), _const_spec((1, N_HEADS)),
                  _const_spec((D_MODEL, LANES)), _const_spec((2 * LANES, D_MODEL)),
                  _const_spec((1, D_MODEL)), _const_spec((1, D_MODEL))],
        out_specs=[full((rows, D_MODEL)), full((rows, D_MODEL)), full((rows, N_HEADS)), full((N_HEADS, rows)),
                   full((rows, D_MODEL)), full((rows, D_MODEL)), full((rows, D_MODEL))],
        out_shape=[blk(f32), blk(f32), jax.ShapeDtypeStruct((rows, N_HEADS), f32),
                   jax.ShapeDtypeStruct((N_HEADS, rows), f32), blk(bf16), blk(bf16), blk(bf16)],
        name="attn_proj", compiler_params=_params(1))(x, g, w, wft, bcol, brow, seg, exp, gq, gk)


def _proj_prompt_kernel(x_ref, g_ref, w_ref, wqt_ref, wft_ref, bcol_ref, brow_ref, seg_ref, exp_ref,
                        gqc_ref, gk_ref, tril_ref, triu_ref,
                        k_ref, v_ref, lf_ref, ka_ref, qat_ref, vt_ref, crow_ref, ccol_ref):
    t = pl.program_id(1)
    tt = x_ref.shape[0]
    n_tiles = tt // ATT_TILE

    @pl.when(t == 0)
    def _():
        crow_ref[...] = jnp.zeros(crow_ref.shape, f32)
        ccol_ref[...] = jnp.zeros(ccol_ref.shape, f32)

    xn = _rms(x_ref[...], g_ref[...]).astype(bf16)
    p = _dot(xn, w_ref[...])
    k = _head_norm(p[:, :D_MODEL], gk_ref[...], seg_ref, exp_ref)
    v = p[:, D_MODEL:2 * D_MODEL]
    k_ref[0] = k
    v_ref[0] = v
    lf = _log_sigmoid(p[:, 2 * D_MODEL:] + brow_ref[...])
    lf_ref[0] = lf[:, :N_HEADS]

    c3 = _dot(tril_ref[...], jnp.concatenate(_split3(lf), axis=1))
    c = c3[:, :LANES] + c3[:, LANES:2 * LANES] + c3[:, 2 * LANES:] + crow_ref[0:1, :]
    crow_ref[0:1, :] = c[tt - 1:tt, :]
    neg_c = [-piece.astype(f32) for piece in _split3(c)]

    lft = _log_sigmoid(_dot_nt(wft_ref[...], xn) + bcol_ref[...])
    ct3 = _dot(jnp.concatenate(_split3(lft), axis=0), triu_ref[...])
    ct = ct3[:N_HEADS] + ct3[N_HEADS:2 * N_HEADS] + ct3[2 * N_HEADS:] + ccol_ref[:, 0:1]
    ccol_ref[...] = jnp.broadcast_to(ct[:, tt - 1:tt], ccol_ref.shape)
    ct_pieces = [piece.astype(f32) for piece in _split3(ct)]

    qt = _dot_nt(wqt_ref[...], xn)
    vt = v.T.astype(bf16)
    for j in range(n_tiles):
        vt_ref[0, j] = vt[:, j * ATT_TILE:(j + 1) * ATT_TILE]

    lane = lax.broadcasted_iota(jnp.int32, (1, LANES), 1)
    sub = lax.broadcasted_iota(jnp.int32, (SUBLANES, 1), 0)
    gqc = gqc_ref[...]
    for hp in range(N_HEADS // 2):
        k2 = k[:, hp * LANES:(hp + 1) * LANES]
        for s in range(2):
            h = 2 * hp + s
            base = k2 if s == 0 else pltpu.roll(k2, HEAD_DIM, 1)
            ext = jnp.zeros((tt, LANES), f32)
            for n in range(N_AUG):
                ext = jnp.where(lane == HEAD_DIM + N_AUG + n,
                                jnp.broadcast_to(neg_c[n][:, h:h + 1], (tt, LANES)), ext)
            ext = jnp.where(lane < HEAD_DIM + N_AUG, 1.0, ext)
            ka_ref[0, h] = jnp.where(lane < HEAD_DIM, base, ext).astype(bf16)

            qh = qt[h * HEAD_DIM:(h + 1) * HEAD_DIM, :]
            qn = qh * lax.rsqrt(jnp.mean(qh * qh, axis=0, keepdims=True) + NORM_EPS) * gqc
            aug = jnp.where(sub < 2 * N_AUG, 1.0, 0.0) * jnp.ones((1, tt), f32)
            for n in range(N_AUG):
                aug = jnp.where(sub == n, ct_pieces[n][h:h + 1, :], aug)
            blk = jnp.concatenate([qn * (HEAD_DIM ** -0.5), aug,
                                   jnp.zeros((LANES - HEAD_DIM - SUBLANES, tt), f32)], axis=0).astype(bf16)
            for j in range(n_tiles):
                qat_ref[0, j, h * LANES:(h + 1) * LANES, :] = blk[:, j * ATT_TILE:(j + 1) * ATT_TILE]


def _attn_proj_prompt(x, g, w, wqt, wft, bcol, brow, seg, exp, gqc, gk, tril, triu, *, b, t):
    tt = ROW_TILE
    nt = tt // ATT_TILE
    blk = pl.BlockSpec((1, tt, D_MODEL), lambda i, j: (i, j, 0))
    big = jax.ShapeDtypeStruct((b, t, D_MODEL), f32)
    return pl.pallas_call(
        _proj_prompt_kernel, grid=(b, t // tt),
        in_specs=[pl.BlockSpec((tt, D_MODEL), lambda i, j: (i * (t // tt) + j, 0)),
                  _const_spec((1, D_MODEL)), _const_spec((D_MODEL, 2 * D_MODEL + F_PAD)),
                  _const_spec((D_MODEL, D_MODEL)), _const_spec((N_HEADS, D_MODEL)),
                  _const_spec((N_HEADS, 1)), _const_spec((1, F_PAD)),
                  _const_spec((D_MODEL, LANES)), _const_spec((2 * LANES, D_MODEL)),
                  _const_spec((HEAD_DIM, tt)), _const_spec((1, D_MODEL)),
                  _const_spec((tt, tt)), _const_spec((tt, tt))],
        out_specs=[blk, blk,
                   pl.BlockSpec((1, tt, N_HEADS), lambda i, j: (i, j, 0)),
                   pl.BlockSpec((1, N_HEADS, tt, LANES), lambda i, j: (i, 0, j, 0)),
                   pl.BlockSpec((1, nt, N_HEADS * LANES, ATT_TILE), lambda i, j: (i, j, 0, 0)),
                   pl.BlockSpec((1, nt, D_MODEL, ATT_TILE), lambda i, j: (i, j, 0, 0))],
        out_shape=[big, big, jax.ShapeDtypeStruct((b, t, N_HEADS), f32),
                   jax.ShapeDtypeStruct((b, N_HEADS, t, LANES), bf16),
                   jax.ShapeDtypeStruct((b, t // ATT_TILE, N_HEADS * LANES, ATT_TILE), bf16),
                   jax.ShapeDtypeStruct((b, t // ATT_TILE, D_MODEL, ATT_TILE), bf16)],
        scratch_shapes=[pltpu.VMEM((SUBLANES, LANES), f32), pltpu.VMEM((N_HEADS, LANES), f32)],
        name="attn_proj_prompt", compiler_params=_params(2))(
            x, g, w, wqt, wft, bcol, brow, seg, exp, gqc, gk, tril, triu)


def _scan_kernel(x_ref, tri_ref, o_ref):
    n = x_ref.shape[2] // SCAN_TILE
    carry = jnp.zeros((N_HEADS, 1), f32)
    for c in range(n):
        sl = slice(c * SCAN_TILE, (c + 1) * SCAN_TILE)
        cs = _dot(jnp.concatenate(_split3(x_ref[0, :, sl]), axis=0), tri_ref[...])
        out = cs[:N_HEADS] + cs[N_HEADS:2 * N_HEADS] + cs[2 * N_HEADS:] + carry
        o_ref[0, :, sl] = out
        carry = out[:, SCAN_TILE - 1:SCAN_TILE]


def _cumsum_lanes(lft, tri):
    b, h, t = lft.shape
    blk = pl.BlockSpec((1, h, t), lambda i: (i, 0, 0))
    return pl.pallas_call(
        _scan_kernel, grid=(b,), in_specs=[blk, _const_spec((SCAN_TILE, SCAN_TILE))], out_specs=blk,
        out_shape=jax.ShapeDtypeStruct((b, h, t), f32), name="cumsum_lanes",
        compiler_params=_params(1))(lft, tri)


def _attn_kernel(qat_ref, ka_ref, vt_ref, o_ref, m_ref, l_ref, acc_ref):
    i = pl.program_id(2)
    tq = qat_ref.shape[3]
    tk = vt_ref.shape[3]
    causal = (lax.broadcasted_iota(jnp.int32, (tk, tq), 0) <= lax.broadcasted_iota(jnp.int32, (tk, tq), 1))
    m_ref[...] = jnp.full(m_ref.shape, NEG_INF, f32)
    l_ref[...] = jnp.zeros(l_ref.shape, f32)
    acc_ref[...] = jnp.zeros(acc_ref.shape, f32)

    def tile(j, masked):
        rows = pl.ds(pl.multiple_of(j * tk, tk), tk)

        def scores(s):
            return _dot(ka_ref[0, s, rows, :], qat_ref[0, 0, s * LANES:(s + 1) * LANES, :])

        ahead = [scores(s) for s in range(ATT_SKEW)]
        for s in range(ATT_HEADS):
            sc = ahead.pop(0)
            if s + ATT_SKEW < ATT_HEADS:
                ahead.append(scores(s + ATT_SKEW))
            if masked:
                sc = jnp.where(causal, sc, NEG_INF)
            m = m_ref[s]
            m_new = jnp.maximum(m, jnp.max(sc, axis=0, keepdims=True))
            alpha = jnp.exp(m - m_new)
            pr = jnp.exp(sc - m_new)
            l_ref[s] = alpha * l_ref[s] + jnp.sum(pr, axis=0, keepdims=True)
            vt = vt_ref[0, j, s * HEAD_DIM:(s + 1) * HEAD_DIM, :]
            acc_ref[s] = alpha * acc_ref[s] + _dot(vt, pr.astype(bf16))
            m_ref[s] = m_new

    def body(j, carry):
        tile(j, False)
        return carry

    lax.fori_loop(0, i, body, 0)
    tile(i, True)
    o_t = jnp.concatenate([acc_ref[s] / l_ref[s] for s in range(ATT_HEADS)], axis=0)
    o_ref[0] = o_t.T.astype(bf16)


def _attn_prompt(qat, ka, vt):
    b, nq, _, tq = qat.shape
    s = nq * tq
    g = ATT_HEADS
    return pl.pallas_call(
        _attn_kernel, grid=(b, N_HEADS // g, nq),
        in_specs=[pl.BlockSpec((1, 1, g * LANES, tq), lambda i, h, j: (i, j, h, 0)),
                  pl.BlockSpec((1, g, s, LANES), lambda i, h, j: (i, h, 0, 0)),
                  pl.BlockSpec((1, nq, g * HEAD_DIM, tq), lambda i, h, j: (i, 0, h, 0))],
        out_specs=pl.BlockSpec((1, tq, g * HEAD_DIM), lambda i, h, j: (i, j, h)),
        out_shape=jax.ShapeDtypeStruct((b, s, D_MODEL), bf16),
        scratch_shapes=[pltpu.VMEM((g, 1, tq), f32), pltpu.VMEM((g, 1, tq), f32),
                        pltpu.VMEM((g, HEAD_DIM, tq), f32)],
        name="attn_prompt",
        compiler_params=_params(3))(qat, ka, vt)


def _attn_sample_kernel(q_ref, kn_ref, vn_ref, kp_ref, vp_ref, cq_ref, ck_ref, o_ref):
    t = q_ref.shape[1]
    past = kp_ref.shape[1]
    low = lax.broadcasted_iota(jnp.int32, (1, LANES), 1) < HEAD_DIM
    causal = (lax.broadcasted_iota(jnp.int32, (t, t), 1) <= lax.broadcasted_iota(jnp.int32, (t, t), 0))
    for hp in range(N_HEADS // 2):
        lanes = slice(hp * LANES, (hp + 1) * LANES)
        q2 = q_ref[0, :, lanes]
        kp2 = kp_ref[0, :, lanes].astype(bf16)
        vp2 = vp_ref[0, :, lanes].astype(bf16)
        kn2 = kn_ref[0, :, lanes]
        vn2 = vn_ref[0, :, lanes]
        halves = []
        for sub in range(2):
            h = 2 * hp + sub
            qm = jnp.where(low if sub == 0 else jnp.logical_not(low), q2, jnp.zeros_like(q2))
            cq = cq_ref[0, :, h:h + 1]
            s_past = _dot_nt(qm, kp2) + cq - ck_ref[0, h:h + 1, :past]
            s_new = _dot_nt(qm, kn2) + cq - ck_ref[0, h:h + 1, past:past + t]
            s_new = jnp.where(causal, s_new, NEG_INF)
            m = jnp.maximum(jnp.max(s_past, axis=1, keepdims=True), jnp.max(s_new, axis=1, keepdims=True))
            p_past = jnp.exp(s_past - m)
            p_new = jnp.exp(s_new - m)
            l = jnp.sum(p_past, axis=1, keepdims=True) + jnp.sum(p_new, axis=1, keepdims=True)
            acc = _dot(p_past.astype(bf16), vp2) + _dot(p_new.astype(bf16), vn2)
            halves.append(acc / l)
        o_ref[0, :, lanes] = jnp.where(low, halves[0], halves[1]).astype(bf16)


def _attn_sample(qb, kb, vb, past_k, past_v, c_nat, c_t):
    b, t, _ = qb.shape
    past = past_k.shape[1]
    new = pl.BlockSpec((1, t, D_MODEL), lambda i: (i, 0, 0))
    old = pl.BlockSpec((1, past, D_MODEL), lambda i: (i, 0, 0))
    return pl.pallas_call(
        _attn_sample_kernel, grid=(b,),
        in_specs=[new, new, new, old, old,
                  pl.BlockSpec((1, t, N_HEADS), lambda i: (i, 0, 0)),
                  pl.BlockSpec((1, N_HEADS, c_t.shape[2]), lambda i: (i, 0, 0))],
        out_specs=new,
        out_shape=jax.ShapeDtypeStruct((b, t, D_MODEL), bf16),
        name="attn_sample", compiler_params=_params(1))(qb, kb, vb, past_k, past_v, c_nat, c_t)


def kernel(x_prompt, x_sample, state_conv, cache_k, cache_v, cache_logf, norm_ffn, ffn_w_in, ffn_w_out, norm_mix, conv_w_in, conv_w, conv_w_out, attn_w_in, attn_b_f, q_norm, k_norm, attn_w_out):
    bp, sp, d = x_prompt.shape
    bs, ss, _ = x_sample.shape
    past = cache_k.shape[2]

    n_p, n_s = bp * sp, bs * ss
    w_in_all, w_out_all = ffn_w_in.astype(bf16), ffn_w_out.astype(bf16)

    def ffn(xs, i, j, mix=None):
        return _ffn(xs, norm_ffn[i, j].reshape(1, d), w_in_all, w_out_all, (i, j), mix=mix)

    y = ffn((x_prompt.reshape(n_p, d), x_sample.reshape(n_s, d)), 0, 0)
    cw = (norm_mix[0].reshape(1, d), conv_w_in[0].astype(bf16), conv_w[0], conv_w_out[0].astype(bf16))
    y_mixed, conv_p = _conv_mixer(y, jnp.zeros((bp, CONV_WIDTH - 1, d), f32), *cw, t=sp, row0=0)
    y, conv_s = _conv_mixer(y, state_conv[0], *cw, t=ss, row0=n_p, into=y_mixed)
    y = ffn(y, 0, 1)

    y = ffn(y, 1, 0)

    w_in = attn_w_in[0]
    f_pad = jnp.zeros((d, F_PAD - N_HEADS), f32)
    wft = w_in[:, 3 * d:].T.astype(bf16)
    head_of = jnp.arange(d) // HEAD_DIM
    seg = ((head_of[:, None] == jnp.arange(LANES)[None, :]).astype(f32) * (1.0 / HEAD_DIM)).astype(bf16)
    expand = (jnp.arange(LANES)[:, None] == head_of[None, :]).astype(bf16)
    expand2 = jnp.concatenate([expand, expand], axis=0)
    g_mix = norm_mix[1].reshape(1, d)
    bcol = attn_b_f[0].reshape(N_HEADS, 1)
    gk = jnp.tile(k_norm[0], N_HEADS).reshape(1, d)

    tril = jnp.tril(jnp.ones((ROW_TILE, ROW_TILE), f32)).astype(bf16)
    kp, vp, fp, ka, qat, vt = _attn_proj_prompt(
        y, g_mix, jnp.concatenate([w_in[:, d:], f_pad], axis=1).astype(bf16),
        w_in[:, :d].T.astype(bf16), wft, bcol,
        jnp.concatenate([attn_b_f[0], jnp.zeros((F_PAD - N_HEADS,), f32)]).reshape(1, F_PAD),
        seg, expand2, jnp.broadcast_to(q_norm[0][:, None], (HEAD_DIM, ROW_TILE)), gk, tril, tril.T, b=bp, t=sp)
    ap = _attn_prompt(qat, ka, vt)

    kq, vq, fq, fqt, qsb, ksb, vsb = _attn_proj(
        y, g_mix, jnp.concatenate([w_in, f_pad], axis=1).astype(bf16), wft, bcol,
        attn_b_f[0].reshape(1, N_HEADS), seg, expand2, jnp.tile(q_norm[0], N_HEADS).reshape(1, d), gk,
        rows=n_s, row0=n_p)
    pad = (-(past + ss)) % SCAN_TILE
    lf_all = jnp.concatenate([jnp.swapaxes(cache_logf[0], 1, 2),
                              jnp.swapaxes(fqt.reshape(N_HEADS, bs, ss), 0, 1),
                              jnp.zeros((bs, N_HEADS, pad), f32)], axis=2)
    cst = _cumsum_lanes(lf_all, jnp.triu(jnp.ones((SCAN_TILE, SCAN_TILE), f32)).astype(bf16))
    per_stream = lambda a: a.reshape(bs, ss, d)
    a_s = _attn_sample(per_stream(qsb), per_stream(ksb), per_stream(vsb),
                       cache_k[0].reshape(bs, past, d), cache_v[0].reshape(bs, past, d),
                       jnp.swapaxes(cst[:, :, past:past + ss], 1, 2), cst)

    yp, ys = ffn(y, 1, 1, mix=(ap.reshape(n_p, d), a_s.reshape(n_s, d), attn_w_out[0].astype(bf16)))

    hd = (N_HEADS, HEAD_DIM)
    return (yp.reshape(bp, sp, d), ys.reshape(bs, ss, d),
            conv_p[None], conv_s[None],
            kp.reshape(1, bp, sp, *hd), vp.reshape(1, bp, sp, *hd), fp[None],
            kq.reshape(1, bs, ss, *hd), vq.reshape(1, bs, ss, *hd), fq.reshape(1, bs, ss, N_HEADS))
```

```python
import functools

import jax
import jax.numpy as jnp
from jax import lax
from jax.experimental import pallas as pl
from jax.experimental.pallas import tpu as pltpu

D_MODEL = 1024
N_HEADS = 16
HEAD_DIM = 64
D_FF = 2816
CONV_WIDTH = 3
NORM_EPS = 1e-6
NEG_INF = -1e30
FFN_RESIDUAL = 0.5

LANES = 128
SUBLANES = 8
F_PAD = LANES
VMEM_LIMIT = 56 * 1024 * 1024
FF_CHUNKS = ((0, 1536), (1536, 2816))
ROW_TILE = 512
ATT_TILE = 256
ATT_HEADS = 8
ATT_SKEW = 5
SCAN_TILE = 256
N_AUG = 3

f32 = jnp.float32
bf16 = jnp.bfloat16


def _dot(a, b):
    return jnp.dot(a, b, preferred_element_type=f32)


def _dot_nt(a, b):
    return lax.dot_general(a, b, (((1,), (1,)), ((), ())), preferred_element_type=f32)


def _rms(x, g):
    return x * lax.rsqrt(jnp.mean(x * x, axis=-1, keepdims=True) + NORM_EPS) * g


def _log_sigmoid(z):
    return -(jnp.maximum(-z, 0.0) + jnp.log1p(jnp.exp(-jnp.abs(z))))


def _split3(x):
    hi = x.astype(bf16)
    r1 = x - hi.astype(f32)
    mid = r1.astype(bf16)
    lo = (r1 - mid.astype(f32)).astype(bf16)
    return hi, mid, lo


def _params(n_axes, flags=None):
    return pltpu.CompilerParams(dimension_semantics=("arbitrary",) * n_axes,
                                vmem_limit_bytes=VMEM_LIMIT, flags=flags)


def _const_spec(shape):
    return pl.BlockSpec(shape, lambda *_: (0,) * len(shape), pipeline_mode=pl.Buffered(1))


def _ffn_kernel(*refs, n_prompt, split_in, mix):
    refs = list(refs)
    is_prompt = pl.program_id(0) < n_prompt

    def rows():
        if split_in or mix:
            p_ref, s_ref = refs.pop(0), refs.pop(0)
            return jnp.where(is_prompt, p_ref[...], s_ref[...])
        return refs.pop(0)[...]

    if mix:
        x = refs.pop(0)[...]
        a = rows()
        x = x + _dot(a, refs.pop(0)[...])
    else:
        x = rows()
    g_ref, win_ref, wout_ref = refs[:3]
    outs = refs[3:]
    xn = _rms(x, g_ref[...]).astype(bf16)
    acc = None
    for lo, hi in FF_CHUNKS:
        a = _dot(xn, win_ref[:, lo:hi])
        b = _dot(xn, win_ref[:, D_FF + lo:D_FF + hi])
        h = (a * jax.nn.sigmoid(a) * b).astype(bf16)
        y = _dot(h, wout_ref[lo:hi, :])
        acc = y if acc is None else acc + y
    y = x + FFN_RESIDUAL * acc
    if mix:
        @pl.when(is_prompt)
        def _():
            outs[0][...] = y

        @pl.when(jnp.logical_not(is_prompt))
        def _():
            outs[1][...] = y
    else:
        outs[0][...] = y


def _ffn(xs, g, w_in, w_out, layer, mix=None):
    li, lj = layer
    tm = ROW_TILE
    split_in = isinstance(xs, tuple)
    if mix is None and split_in:
        n_prompt, n_sample = xs[0].shape[0] // tm, xs[1].shape[0] // tm
    elif mix is not None:
        n_prompt, n_sample = mix[0].shape[0] // tm, mix[1].shape[0] // tm
    else:
        n_prompt, n_sample = xs.shape[0] // tm, 0
    assert n_sample in (0, 1)
    steps = n_prompt + n_sample
    row = pl.BlockSpec((tm, D_MODEL), lambda i: (i, 0))
    prompt_row = pl.BlockSpec((tm, D_MODEL), lambda i: (jnp.minimum(i, n_prompt - 1), 0))
    sample_row = pl.BlockSpec((tm, D_MODEL), lambda i: (0, 0))
    w_specs = [_const_spec((1, D_MODEL)),
               pl.BlockSpec((None, None, D_MODEL, 2 * D_FF), lambda i: (li, lj, 0, 0), pipeline_mode=pl.Buffered(1)),
               pl.BlockSpec((None, None, D_FF, D_MODEL), lambda i: (li, lj, 0, 0), pipeline_mode=pl.Buffered(1))]
    stacked = jax.ShapeDtypeStruct((steps * tm, D_MODEL), f32)
    if mix is not None:
        a_p, a_s, wm = mix
        ins = (xs, a_p, a_s, wm)
        specs = [row, prompt_row, sample_row, _const_spec((D_MODEL, D_MODEL))]
        out_specs = [prompt_row, sample_row]
        out_shape = [jax.ShapeDtypeStruct((n_prompt * tm, D_MODEL), f32),
                     jax.ShapeDtypeStruct((n_sample * tm, D_MODEL), f32)]
    elif split_in:
        ins, specs, out_specs, out_shape = tuple(xs), [prompt_row, sample_row], row, stacked
    else:
        ins, specs, out_specs, out_shape = (xs,), [row], row, stacked
    kern = functools.partial(_ffn_kernel, n_prompt=n_prompt, split_in=split_in, mix=mix is not None)
    return pl.pallas_call(
        kern, grid=(steps,), in_specs=specs + w_specs, out_specs=out_specs, out_shape=out_shape,
        name="ffn" if mix is None else "mix_ffn",
        compiler_params=_params(1))(*ins, g, w_in, w_out)


def _conv_kernel(x_ref, hist_ref, g_ref, win_ref, wk_ref, wout_ref, *rest):
    o_ref, st_ref, carry_ref = rest[-3:]
    t = pl.program_id(1)
    tt = x_ref.shape[0]

    @pl.when(t == 0)
    def _():
        carry_ref[0:2, :] = hist_ref[0]

    x = x_ref[...]
    xn = _rms(x, g_ref[...]).astype(bf16)
    p = _dot(xn, win_ref[...])
    gate_b = p[:, :D_MODEL]
    u = p[:, D_MODEL:2 * D_MODEL] * p[:, 2 * D_MODEL:]
    prev2 = carry_ref[0:1, :]
    prev1 = carry_ref[1:2, :]
    row = lax.broadcasted_iota(jnp.int32, (tt, 1), 0)
    u1 = jnp.where(row == 0, prev1, pltpu.roll(u, 1, 0))
    u2 = jnp.where(row == 0, prev2, jnp.where(row == 1, prev1, pltpu.roll(u, 2, 0)))
    wk = wk_ref[...]
    conv = wk[0:1, :] * u2 + wk[1:2, :] * u1 + wk[2:3, :] * u
    y = _dot((gate_b * conv).astype(bf16), wout_ref[...])
    o_ref[...] = x + y
    last = u[tt - 2:tt, :]
    carry_ref[0:2, :] = last
    st_ref[0] = last


def _conv_mixer(x, hist, g, win, wk, wout, *, t, row0, into=None):
    b = hist.shape[0]
    tt = min(ROW_TILE, t)
    nt, base = t // tt, row0 // tt
    blk = pl.BlockSpec((tt, D_MODEL), lambda i, j: (base + i * nt + j, 0))
    st = pl.BlockSpec((1, CONV_WIDTH - 1, D_MODEL), lambda i, j: (i, 0, 0))
    ins = [x, hist, g, win, wk, wout]
    specs = [blk, st, _const_spec((1, D_MODEL)), _const_spec((D_MODEL, 3 * D_MODEL)),
             _const_spec((CONV_WIDTH, D_MODEL)), _const_spec((D_MODEL, D_MODEL))]
    aliases = {}
    if into is not None:
        aliases = {len(ins): 0}
        ins.append(into)
        specs.append(pl.BlockSpec(memory_space=pl.ANY))
    return pl.pallas_call(
        _conv_kernel, grid=(b, nt), in_specs=specs, out_specs=[blk, st],
        out_shape=[jax.ShapeDtypeStruct(x.shape, f32),
                   jax.ShapeDtypeStruct((b, CONV_WIDTH - 1, D_MODEL), f32)],
        scratch_shapes=[pltpu.VMEM((8, D_MODEL), f32)], input_output_aliases=aliases,
        name="conv_mixer", compiler_params=_params(2))(*ins)


def _head_norm(t, gain, seg_ref, exp_ref):
    ms = _dot((t * t).astype(bf16), seg_ref[...])
    r = lax.rsqrt(ms + NORM_EPS)
    r_hi = r.astype(bf16)
    r_lo = (r - r_hi.astype(f32)).astype(bf16)
    rb = _dot(jnp.concatenate([r_hi, r_lo], axis=1), exp_ref[...])
    return t * rb * gain


def _proj_kernel(x_ref, g_ref, w_ref, wft_ref, bcol_ref, brow_ref, seg_ref, exp_ref, gq_ref, gk_ref,
                 k_ref, v_ref, lf_ref, lft_ref, qb_ref, kb_ref, vb_ref):
    xn = _rms(x_ref[...], g_ref[...]).astype(bf16)
    p = _dot(xn, w_ref[...])
    q = _head_norm(p[:, :D_MODEL], gq_ref[...], seg_ref, exp_ref)
    k = _head_norm(p[:, D_MODEL:2 * D_MODEL], gk_ref[...], seg_ref, exp_ref)
    v = p[:, 2 * D_MODEL:3 * D_MODEL]
    k_ref[...] = k
    v_ref[...] = v
    lf_ref[...] = _log_sigmoid(p[:, 3 * D_MODEL:3 * D_MODEL + N_HEADS] + brow_ref[...])
    lft_ref[...] = _log_sigmoid(_dot_nt(wft_ref[...], xn) + bcol_ref[...])
    qb_ref[...] = (q * (HEAD_DIM ** -0.5)).astype(bf16)
    kb_ref[...] = k.astype(bf16)
    vb_ref[...] = v.astype(bf16)


def _attn_proj(x, g, w, wft, bcol, brow, seg, exp, gq, gk, *, rows, row0):
    blk = lambda dt: jax.ShapeDtypeStruct((rows, D_MODEL), dt)
    full = lambda shape: pl.BlockSpec(shape, lambda i: (0, 0))
    return pl.pallas_call(
        _proj_kernel, grid=(1,),
        in_specs=[pl.BlockSpec((rows, D_MODEL), lambda i: (row0 // rows, 0)),
                  _const_spec((1, D_MODEL)), _const_spec((D_MODEL, 3 * D_MODEL + F_PAD)),
                  _const_spec((N_HEADS, D_MODEL)), _const_spec((N_HEADS, 1)), _const_spec((1, N_HEADS)),
                  _const_spec((D_MODEL, LANES)), _const_spec((2 * LANES, D_MODEL)),
                  _const_spec((1, D_MODEL)), _const_spec((1, D_MODEL))],
        out_specs=[full((rows, D_MODEL)), full((rows, D_MODEL)), full((rows, N_HEADS)), full((N_HEADS, rows)),
                   full((rows, D_MODEL)), full((rows, D_MODEL)), full((rows, D_MODEL))],
        out_shape=[blk(f32), blk(f32), jax.ShapeDtypeStruct((rows, N_HEADS), f32),
                   jax.ShapeDtypeStruct((N_HEADS, rows), f32), blk(bf16), blk(bf16), blk(bf16)],
        name="attn_proj", compiler_params=_params(1))(x, g, w, wft, bcol, brow, seg, exp, gq, gk)


def _proj_prompt_kernel(x_ref, g_ref, w_ref, wqvt_ref, wft_ref, bcol_ref, brow_ref, seg_ref, exp_ref,
                        gqc_ref, gk_ref, tril_ref, triu_ref,
                        kt_ref, vt_ref, lft_ref, ka_ref, qat_ref, vtb_ref, crow_ref, ccol_ref):
    t = pl.program_id(1)
    tt = x_ref.shape[0]
    n_tiles = tt // ATT_TILE

    @pl.when(t == 0)
    def _():
        crow_ref[...] = jnp.zeros(crow_ref.shape, f32)
        ccol_ref[...] = jnp.zeros(ccol_ref.shape, f32)

    xn = _rms(x_ref[...], g_ref[...]).astype(bf16)
    p = _dot(xn, w_ref[...])
    k = _head_norm(p[:, :D_MODEL], gk_ref[...], seg_ref, exp_ref)
    kt_ref[0] = k.T
    lf = _log_sigmoid(p[:, D_MODEL:] + brow_ref[...])

    c3 = _dot(tril_ref[...], jnp.concatenate(_split3(lf), axis=1))
    c = c3[:, :LANES] + c3[:, LANES:2 * LANES] + c3[:, 2 * LANES:] + crow_ref[0:1, :]
    crow_ref[0:1, :] = c[tt - 1:tt, :]
    neg_c = [-piece.astype(f32) for piece in _split3(c)]

    lft = _log_sigmoid(_dot_nt(wft_ref[...], xn) + bcol_ref[...])
    lft_ref[0] = lft
    ct3 = _dot(jnp.concatenate(_split3(lft), axis=0), triu_ref[...])
    ct = ct3[:N_HEADS] + ct3[N_HEADS:2 * N_HEADS] + ct3[2 * N_HEADS:] + ccol_ref[:, 0:1]
    ccol_ref[...] = jnp.broadcast_to(ct[:, tt - 1:tt], ccol_ref.shape)
    ct_pieces = [piece.astype(f32) for piece in _split3(ct)]

    qvt = _dot_nt(wqvt_ref[...], xn)
    qt = qvt[:D_MODEL]
    vt = qvt[D_MODEL:]
    vt_ref[0] = vt
    vtb = vt.astype(bf16)
    for j in range(n_tiles):
        vtb_ref[0, j] = vtb[:, j * ATT_TILE:(j + 1) * ATT_TILE]

    lane = lax.broadcasted_iota(jnp.int32, (1, LANES), 1)
    sub = lax.broadcasted_iota(jnp.int32, (SUBLANES, 1), 0)
    gqc = gqc_ref[...]
    for hp in range(N_HEADS // 2):
        k2 = k[:, hp * LANES:(hp + 1) * LANES]
        for s in range(2):
            h = 2 * hp + s
            base = k2 if s == 0 else pltpu.roll(k2, HEAD_DIM, 1)
            ext = jnp.zeros((tt, LANES), f32)
            for n in range(N_AUG):
                ext = jnp.where(lane == HEAD_DIM + N_AUG + n,
                                jnp.broadcast_to(neg_c[n][:, h:h + 1], (tt, LANES)), ext)
            ext = jnp.where(lane < HEAD_DIM + N_AUG, 1.0, ext)
            ka_ref[0, h] = jnp.where(lane < HEAD_DIM, base, ext).astype(bf16)

            qh = qt[h * HEAD_DIM:(h + 1) * HEAD_DIM, :]
            qn = qh * lax.rsqrt(jnp.mean(qh * qh, axis=0, keepdims=True) + NORM_EPS) * gqc
            aug = jnp.where(sub < 2 * N_AUG, 1.0, 0.0) * jnp.ones((1, tt), f32)
            for n in range(N_AUG):
                aug = jnp.where(sub == n, ct_pieces[n][h:h + 1, :], aug)
            blk = jnp.concatenate([qn * (HEAD_DIM ** -0.5), aug,
                                   jnp.zeros((LANES - HEAD_DIM - SUBLANES, tt), f32)], axis=0).astype(bf16)
            for j in range(n_tiles):
                qat_ref[0, j, h * LANES:(h + 1) * LANES, :] = blk[:, j * ATT_TILE:(j + 1) * ATT_TILE]


def _attn_proj_prompt(x, g, w, wqvt, wft, bcol, brow, seg, exp, gqc, gk, tril, triu, *, b, t):
    tt = ROW_TILE
    nt = tt // ATT_TILE
    time_minor = pl.BlockSpec((1, D_MODEL, tt), lambda i, j: (i, 0, j))
    big = jax.ShapeDtypeStruct((b, D_MODEL, t), f32)
    return pl.pallas_call(
        _proj_prompt_kernel, grid=(b, t // tt),
        in_specs=[pl.BlockSpec((tt, D_MODEL), lambda i, j: (i * (t // tt) + j, 0)),
                  _const_spec((1, D_MODEL)), _const_spec((D_MODEL, D_MODEL + F_PAD)),
                  _const_spec((2 * D_MODEL, D_MODEL)), _const_spec((N_HEADS, D_MODEL)),
                  _const_spec((N_HEADS, 1)), _const_spec((1, F_PAD)),
                  _const_spec((D_MODEL, LANES)), _const_spec((2 * LANES, D_MODEL)),
                  _const_spec((HEAD_DIM, tt)), _const_spec((1, D_MODEL)),
                  _const_spec((tt, tt)), _const_spec((tt, tt))],
        out_specs=[time_minor, time_minor,
                   pl.BlockSpec((1, N_HEADS, tt), lambda i, j: (i, 0, j)),
                   pl.BlockSpec((1, N_HEADS, tt, LANES), lambda i, j: (i, 0, j, 0)),
                   pl.BlockSpec((1, nt, N_HEADS * LANES, ATT_TILE), lambda i, j: (i, j, 0, 0)),
                   pl.BlockSpec((1, nt, D_MODEL, ATT_TILE), lambda i, j: (i, j, 0, 0))],
        out_shape=[big, big, jax.ShapeDtypeStruct((b, N_HEADS, t), f32),
                   jax.ShapeDtypeStruct((b, N_HEADS, t, LANES), bf16),
                   jax.ShapeDtypeStruct((b, t // ATT_TILE, N_HEADS * LANES, ATT_TILE), bf16),
                   jax.ShapeDtypeStruct((b, t // ATT_TILE, D_MODEL, ATT_TILE), bf16)],
        scratch_shapes=[pltpu.VMEM((SUBLANES, LANES), f32), pltpu.VMEM((N_HEADS, LANES), f32)],
        name="attn_proj_prompt", compiler_params=_params(2))(
            x, g, w, wqvt, wft, bcol, brow, seg, exp, gqc, gk, tril, triu)


def _scan_kernel(x_ref, tri_ref, o_ref):
    n = x_ref.shape[2] // SCAN_TILE
    carry = jnp.zeros((N_HEADS, 1), f32)
    for c in range(n):
        sl = slice(c * SCAN_TILE, (c + 1) * SCAN_TILE)
        cs = _dot(jnp.concatenate(_split3(x_ref[0, :, sl]), axis=0), tri_ref[...])
        out = cs[:N_HEADS] + cs[N_HEADS:2 * N_HEADS] + cs[2 * N_HEADS:] + carry
        o_ref[0, :, sl] = out
        carry = out[:, SCAN_TILE - 1:SCAN_TILE]


def _cumsum_lanes(lft, tri):
    b, h, t = lft.shape
    blk = pl.BlockSpec((1, h, t), lambda i: (i, 0, 0))
    return pl.pallas_call(
        _scan_kernel, grid=(b,), in_specs=[blk, _const_spec((SCAN_TILE, SCAN_TILE))], out_specs=blk,
        out_shape=jax.ShapeDtypeStruct((b, h, t), f32), name="cumsum_lanes",
        compiler_params=_params(1))(lft, tri)


def _attn_kernel(qat_ref, ka_ref, vt_ref, o_ref, m_ref, l_ref, acc_ref):
    i = pl.program_id(2)
    tq = qat_ref.shape[3]
    tk = vt_ref.shape[3]
    causal = (lax.broadcasted_iota(jnp.int32, (tk, tq), 0) <= lax.broadcasted_iota(jnp.int32, (tk, tq), 1))
    m_ref[...] = jnp.full(m_ref.shape, NEG_INF, f32)
    l_ref[...] = jnp.zeros(l_ref.shape, f32)
    acc_ref[...] = jnp.zeros(acc_ref.shape, f32)

    def tile(j, masked):
        rows = pl.ds(pl.multiple_of(j * tk, tk), tk)

        def scores(s):
            return _dot(ka_ref[0, s, rows, :], qat_ref[0, 0, s * LANES:(s + 1) * LANES, :])

        ahead = [scores(s) for s in range(ATT_SKEW)]
        for s in range(ATT_HEADS):
            sc = ahead.pop(0)
            if s + ATT_SKEW < ATT_HEADS:
                ahead.append(scores(s + ATT_SKEW))
            if masked:
                sc = jnp.where(causal, sc, NEG_INF)
            m = m_ref[s]
            m_new = jnp.maximum(m, jnp.max(sc, axis=0, keepdims=True))
            alpha = jnp.exp(m - m_new)
            pr = jnp.exp(sc - m_new)
            l_ref[s] = alpha * l_ref[s] + jnp.sum(pr, axis=0, keepdims=True)
            vt = vt_ref[0, j, s * HEAD_DIM:(s + 1) * HEAD_DIM, :]
            acc_ref[s] = alpha * acc_ref[s] + _dot(vt, pr.astype(bf16))
            m_ref[s] = m_new

    def body(j, carry):
        tile(j, False)
        return carry

    lax.fori_loop(0, i, body, 0)
    tile(i, True)
    o_t = jnp.concatenate([acc_ref[s] / l_ref[s] for s in range(ATT_HEADS)], axis=0)
    o_ref[0] = o_t.T.astype(bf16)


def _attn_prompt(qat, ka, vt):
    b, nq, _, tq = qat.shape
    s = nq * tq
    g = ATT_HEADS
    return pl.pallas_call(
        _attn_kernel, grid=(b, N_HEADS // g, nq),
        in_specs=[pl.BlockSpec((1, 1, g * LANES, tq), lambda i, h, j: (i, j, h, 0)),
                  pl.BlockSpec((1, g, s, LANES), lambda i, h, j: (i, h, 0, 0)),
                  pl.BlockSpec((1, nq, g * HEAD_DIM, tq), lambda i, h, j: (i, 0, h, 0))],
        out_specs=pl.BlockSpec((1, tq, g * HEAD_DIM), lambda i, h, j: (i, j, h)),
        out_shape=jax.ShapeDtypeStruct((b, s, D_MODEL), bf16),
        scratch_shapes=[pltpu.VMEM((g, 1, tq), f32), pltpu.VMEM((g, 1, tq), f32),
                        pltpu.VMEM((g, HEAD_DIM, tq), f32)],
        name="attn_prompt",
        compiler_params=_params(3))(qat, ka, vt)


def _attn_sample_kernel(q_ref, kn_ref, vn_ref, kp_ref, vp_ref, cq_ref, ck_ref, o_ref):
    t = q_ref.shape[1]
    past = kp_ref.shape[2]
    low = lax.broadcasted_iota(jnp.int32, (1, LANES), 1) < HEAD_DIM
    causal = (lax.broadcasted_iota(jnp.int32, (t, t), 1) <= lax.broadcasted_iota(jnp.int32, (t, t), 0))
    for hp in range(N_HEADS // 2):
        lanes = slice(hp * LANES, (hp + 1) * LANES)
        q2 = q_ref[0, :, lanes]
        kp2 = kp_ref[0, lanes, :].astype(bf16)
        vp2 = vp_ref[0, lanes, :].astype(bf16)
        kn2 = kn_ref[0, :, lanes]
        vn2 = vn_ref[0, :, lanes]
        halves = []
        for sub in range(2):
            h = 2 * hp + sub
            qm = jnp.where(low if sub == 0 else jnp.logical_not(low), q2, jnp.zeros_like(q2))
            cq = cq_ref[0, :, h:h + 1]
            s_past = _dot(qm, kp2) + cq - ck_ref[0, h:h + 1, :past]
            s_new = _dot_nt(qm, kn2) + cq - ck_ref[0, h:h + 1, past:past + t]
            s_new = jnp.where(causal, s_new, NEG_INF)
            m = jnp.maximum(jnp.max(s_past, axis=1, keepdims=True), jnp.max(s_new, axis=1, keepdims=True))
            p_past = jnp.exp(s_past - m)
            p_new = jnp.exp(s_new - m)
            l = jnp.sum(p_past, axis=1, keepdims=True) + jnp.sum(p_new, axis=1, keepdims=True)
            acc = _dot_nt(p_past.astype(bf16), vp2) + _dot(p_new.astype(bf16), vn2)
            halves.append(acc / l)
        o_ref[0, :, lanes] = jnp.where(low, halves[0], halves[1]).astype(bf16)


def _attn_sample(qb, kb, vb, past_kt, past_vt, c_nat, c_t):
    b, t, _ = qb.shape
    past = past_kt.shape[2]
    new = pl.BlockSpec((1, t, D_MODEL), lambda i: (i, 0, 0))
    old = pl.BlockSpec((1, D_MODEL, past), lambda i: (i, 0, 0))
    return pl.pallas_call(
        _attn_sample_kernel, grid=(b,),
        in_specs=[new, new, new, old, old,
                  pl.BlockSpec((1, t, N_HEADS), lambda i: (i, 0, 0)),
                  pl.BlockSpec((1, N_HEADS, c_t.shape[2]), lambda i: (i, 0, 0))],
        out_specs=new,
        out_shape=jax.ShapeDtypeStruct((b, t, D_MODEL), bf16),
        name="attn_sample", compiler_params=_params(1))(qb, kb, vb, past_kt, past_vt, c_nat, c_t)


def kernel(x_prompt, x_sample, state_conv, cache_k, cache_v, cache_logf, norm_ffn, ffn_w_in, ffn_w_out, norm_mix, conv_w_in, conv_w, conv_w_out, attn_w_in, attn_b_f, q_norm, k_norm, attn_w_out):
    bp, sp, d = x_prompt.shape
    bs, ss, _ = x_sample.shape
    past = cache_k.shape[2]

    n_p, n_s = bp * sp, bs * ss
    w_in_all, w_out_all = ffn_w_in.astype(bf16), ffn_w_out.astype(bf16)

    def ffn(xs, i, j, mix=None):
        return _ffn(xs, norm_ffn[i, j].reshape(1, d), w_in_all, w_out_all, (i, j), mix=mix)

    y = ffn((x_prompt.reshape(n_p, d), x_sample.reshape(n_s, d)), 0, 0)
    cw = (norm_mix[0].reshape(1, d), conv_w_in[0].astype(bf16), conv_w[0], conv_w_out[0].astype(bf16))
    y_mixed, conv_p = _conv_mixer(y, jnp.zeros((bp, CONV_WIDTH - 1, d), f32), *cw, t=sp, row0=0)
    y, conv_s = _conv_mixer(y, state_conv[0], *cw, t=ss, row0=n_p, into=y_mixed)
    y = ffn(y, 0, 1)

    y = ffn(y, 1, 0)

    w_in = attn_w_in[0]
    f_pad = jnp.zeros((d, F_PAD - N_HEADS), f32)
    wft = w_in[:, 3 * d:].T.astype(bf16)
    head_of = jnp.arange(d) // HEAD_DIM
    seg = ((head_of[:, None] == jnp.arange(LANES)[None, :]).astype(f32) * (1.0 / HEAD_DIM)).astype(bf16)
    expand = (jnp.arange(LANES)[:, None] == head_of[None, :]).astype(bf16)
    expand2 = jnp.concatenate([expand, expand], axis=0)
    g_mix = norm_mix[1].reshape(1, d)
    bcol = attn_b_f[0].reshape(N_HEADS, 1)
    gk = jnp.tile(k_norm[0], N_HEADS).reshape(1, d)

    tril = jnp.tril(jnp.ones((ROW_TILE, ROW_TILE), f32)).astype(bf16)
    kpt, vpt, fpt, ka, qat, vt = _attn_proj_prompt(
        y, g_mix, jnp.concatenate([w_in[:, d:2 * d], w_in[:, 3 * d:], f_pad], axis=1).astype(bf16),
        jnp.concatenate([w_in[:, :d], w_in[:, 2 * d:3 * d]], axis=1).T.astype(bf16), wft, bcol,
        jnp.concatenate([attn_b_f[0], jnp.zeros((F_PAD - N_HEADS,), f32)]).reshape(1, F_PAD),
        seg, expand2, jnp.broadcast_to(q_norm[0][:, None], (HEAD_DIM, ROW_TILE)), gk, tril, tril.T, b=bp, t=sp)
    ap = _attn_prompt(qat, ka, vt)

    kq, vq, fq, fqt, qsb, ksb, vsb = _attn_proj(
        y, g_mix, jnp.concatenate([w_in, f_pad], axis=1).astype(bf16), wft, bcol,
        attn_b_f[0].reshape(1, N_HEADS), seg, expand2, jnp.tile(q_norm[0], N_HEADS).reshape(1, d), gk,
        rows=n_s, row0=n_p)
    pad = (-(past + ss)) % SCAN_TILE
    lf_all = jnp.concatenate([jnp.swapaxes(cache_logf[0], 1, 2),
                              jnp.swapaxes(fqt.reshape(N_HEADS, bs, ss), 0, 1),
                              jnp.zeros((bs, N_HEADS, pad), f32)], axis=2)
    cst = _cumsum_lanes(lf_all, jnp.triu(jnp.ones((SCAN_TILE, SCAN_TILE), f32)).astype(bf16))
    per_stream = lambda a: a.reshape(bs, ss, d)
    channel_major = lambda a: jnp.transpose(a, (0, 2, 3, 1)).reshape(bs, d, past)
    a_s = _attn_sample(per_stream(qsb), per_stream(ksb), per_stream(vsb),
                       channel_major(cache_k[0]), channel_major(cache_v[0]),
                       jnp.swapaxes(cst[:, :, past:past + ss], 1, 2), cst)

    yp, ys = ffn(y, 1, 1, mix=(ap.reshape(n_p, d), a_s.reshape(n_s, d), attn_w_out[0].astype(bf16)))

    hd = (N_HEADS, HEAD_DIM)
    token_major = lambda a: jnp.transpose(a.reshape(bp, *hd, sp), (0, 3, 1, 2))[None]
    return (yp.reshape(bp, sp, d), ys.reshape(bs, ss, d),
            conv_p[None], conv_s[None],
            token_major(kpt), token_major(vpt), jnp.swapaxes(fpt, 1, 2)[None],
            kq.reshape(1, bs, ss, *hd), vq.reshape(1, bs, ss, *hd), fq.reshape(1, bs, ss, N_HEADS))
```

```python
import functools

import jax
import jax.numpy as jnp
from jax import lax
from jax.experimental import pallas as pl
from jax.experimental.pallas import tpu as pltpu

D_MODEL = 1024
N_HEADS = 16
HEAD_DIM = 64
D_FF = 2816
CONV_WIDTH = 3
NORM_EPS = 1e-6
NEG_INF = -1e30
FFN_RESIDUAL = 0.5

LANES = 128
SUBLANES = 8
F_PAD = LANES
VMEM_LIMIT = 56 * 1024 * 1024
FF_CHUNKS = ((0, 1536), (1536, 2816))
ROW_TILE = 512
ATT_TILE = 256
ATT_HEADS = 8
ATT_SKEW = 5
SCAN_TILE = 256
N_AUG = 3

f32 = jnp.float32
bf16 = jnp.bfloat16


def _dot(a, b):
    return jnp.dot(a, b, preferred_element_type=f32)


def _dot_nt(a, b):
    return lax.dot_general(a, b, (((1,), (1,)), ((), ())), preferred_element_type=f32)


def _rms(x, g):
    return x * lax.rsqrt(jnp.mean(x * x, axis=-1, keepdims=True) + NORM_EPS) * g


def _log_sigmoid(z):
    return -(jnp.maximum(-z, 0.0) + jnp.log1p(jnp.exp(-jnp.abs(z))))


def _split3(x):
    hi = x.astype(bf16)
    r1 = x - hi.astype(f32)
    mid = r1.astype(bf16)
    lo = (r1 - mid.astype(f32)).astype(bf16)
    return hi, mid, lo


def _params(n_axes, flags=None):
    return pltpu.CompilerParams(dimension_semantics=("arbitrary",) * n_axes,
                                vmem_limit_bytes=VMEM_LIMIT, flags=flags)


def _const_spec(shape):
    return pl.BlockSpec(shape, lambda *_: (0,) * len(shape), pipeline_mode=pl.Buffered(1))


def _cast_job(src, lead, steps, lin):
    rows, cols = src.shape[-2:]
    slab = rows // steps
    assert slab * steps == rows and slab % 16 == 0
    pos = lambda *idx: jnp.minimum(lin(*idx), steps - 1)
    in_spec = pl.BlockSpec((None,) * len(lead) + (slab, cols), lambda *idx: (*lead, pos(*idx), 0))
    out_spec = pl.BlockSpec((slab, cols), lambda *idx: (pos(*idx), 0))
    return src, in_spec, out_spec, jax.ShapeDtypeStruct((rows, cols), bf16), steps


def _run_casts(srcs, dsts):
    for src_ref, dst_ref in zip(srcs, dsts):
        dst_ref[...] = src_ref[...].astype(bf16)


def _ffn_kernel(*refs, n_prompt, split_in, mix, n_jobs):
    refs = list(refs)
    cast_srcs = cast_dsts = ()
    if n_jobs:
        cast_dsts = refs[-n_jobs:]
        del refs[-n_jobs:]
        n_out = 2 if mix else 1
        cast_srcs = refs[-n_out - n_jobs:-n_out]
        del refs[-n_out - n_jobs:-n_out]
    is_prompt = pl.program_id(0) < n_prompt

    def rows():
        if split_in or mix:
            p_ref, s_ref = refs.pop(0), refs.pop(0)
            return jnp.where(is_prompt, p_ref[...], s_ref[...])
        return refs.pop(0)[...]

    if mix:
        x = refs.pop(0)[...]
        a = rows()
        x = x + _dot(a, refs.pop(0)[...])
    else:
        x = rows()
    g_ref, win_ref, wout_ref = refs[:3]
    outs = refs[3:]
    xn = _rms(x, g_ref[...]).astype(bf16)
    acc = None
    for lo, hi in FF_CHUNKS:
        a = _dot(xn, win_ref[:, lo:hi])
        b = _dot(xn, win_ref[:, D_FF + lo:D_FF + hi])
        if lo == 0:
            _run_casts(cast_srcs, cast_dsts)
        h = (a * jax.nn.sigmoid(a) * b).astype(bf16)
        y = _dot(h, wout_ref[lo:hi, :])
        acc = y if acc is None else acc + y
    y = x + FFN_RESIDUAL * acc
    if mix:
        @pl.when(is_prompt)
        def _():
            outs[0][...] = y

        @pl.when(jnp.logical_not(is_prompt))
        def _():
            outs[1][...] = y
    else:
        outs[0][...] = y


def _ffn(xs, g, w_in, w_out, mix=None, casts=()):
    tm = ROW_TILE
    split_in = isinstance(xs, tuple)
    if mix is None and split_in:
        n_prompt, n_sample = xs[0].shape[0] // tm, xs[1].shape[0] // tm
    elif mix is not None:
        n_prompt, n_sample = mix[0].shape[0] // tm, mix[1].shape[0] // tm
    else:
        n_prompt, n_sample = xs.shape[0] // tm, 0
    assert n_sample in (0, 1)
    steps = n_prompt + n_sample
    row = pl.BlockSpec((tm, D_MODEL), lambda i: (i, 0))
    prompt_row = pl.BlockSpec((tm, D_MODEL), lambda i: (jnp.minimum(i, n_prompt - 1), 0))
    sample_row = pl.BlockSpec((tm, D_MODEL), lambda i: (0, 0))
    w_specs = [_const_spec((1, D_MODEL)), _const_spec((D_MODEL, 2 * D_FF)), _const_spec((D_FF, D_MODEL))]
    stacked = jax.ShapeDtypeStruct((steps * tm, D_MODEL), f32)
    if mix is not None:
        a_p, a_s, wm = mix
        ins = (xs, a_p, a_s, wm)
        specs = [row, prompt_row, sample_row, _const_spec((D_MODEL, D_MODEL))]
        out_specs = [prompt_row, sample_row]
        out_shape = [jax.ShapeDtypeStruct((n_prompt * tm, D_MODEL), f32),
                     jax.ShapeDtypeStruct((n_sample * tm, D_MODEL), f32)]
    elif split_in:
        ins, specs, out_specs, out_shape = tuple(xs), [prompt_row, sample_row], [row], [stacked]
    else:
        ins, specs, out_specs, out_shape = (xs,), [row], [row], [stacked]
    jobs = [_cast_job(src, lead, n, lambda i: i) for src, lead, n in casts]
    kern = functools.partial(_ffn_kernel, n_prompt=n_prompt, split_in=split_in, mix=mix is not None,
                             n_jobs=len(jobs))
    return pl.pallas_call(
        kern, grid=(steps,), in_specs=specs + w_specs + [j[1] for j in jobs],
        out_specs=out_specs + [j[2] for j in jobs], out_shape=out_shape + [j[3] for j in jobs],
        name="ffn" if mix is None else "mix_ffn",
        compiler_params=_params(1))(*ins, g, w_in, w_out, *[j[0] for j in jobs])


def _conv_kernel(x_ref, hist_ref, g_ref, win_ref, wk_ref, wout_ref, *rest, aliased, n_jobs):
    rest = rest[1:] if aliased else rest
    o_ref, st_ref = rest[n_jobs:n_jobs + 2]
    carry_ref = rest[-1]
    t = pl.program_id(1)
    tt = x_ref.shape[0]

    @pl.when(t == 0)
    def _():
        carry_ref[0:2, :] = hist_ref[0]

    x = x_ref[...]
    xn = _rms(x, g_ref[...]).astype(bf16)
    p = _dot(xn, win_ref[...])
    _run_casts(rest[:n_jobs], rest[n_jobs + 2:-1])
    gate_b = p[:, :D_MODEL]
    u = p[:, D_MODEL:2 * D_MODEL] * p[:, 2 * D_MODEL:]
    prev2 = carry_ref[0:1, :]
    prev1 = carry_ref[1:2, :]
    row = lax.broadcasted_iota(jnp.int32, (tt, 1), 0)
    u1 = jnp.where(row == 0, prev1, pltpu.roll(u, 1, 0))
    u2 = jnp.where(row == 0, prev2, jnp.where(row == 1, prev1, pltpu.roll(u, 2, 0)))
    wk = wk_ref[...]
    conv = wk[0:1, :] * u2 + wk[1:2, :] * u1 + wk[2:3, :] * u
    y = _dot((gate_b * conv).astype(bf16), wout_ref[...])
    o_ref[...] = x + y
    last = u[tt - 2:tt, :]
    carry_ref[0:2, :] = last
    st_ref[0] = last


def _conv_mixer(x, hist, g, win, wk, wout, *, t, row0, into=None, casts=()):
    b = hist.shape[0]
    tt = min(ROW_TILE, t)
    nt, base = t // tt, row0 // tt
    blk = pl.BlockSpec((tt, D_MODEL), lambda i, j: (base + i * nt + j, 0))
    st = pl.BlockSpec((1, CONV_WIDTH - 1, D_MODEL), lambda i, j: (i, 0, 0))
    ins = [x, hist, g, win, wk, wout]
    specs = [blk, st, _const_spec((1, D_MODEL)), _const_spec((D_MODEL, 3 * D_MODEL)),
             _const_spec((CONV_WIDTH, D_MODEL)), _const_spec((D_MODEL, D_MODEL))]
    aliases = {}
    if into is not None:
        aliases = {len(ins): 0}
        ins.append(into)
        specs.append(pl.BlockSpec(memory_space=pl.ANY))
    jobs = [_cast_job(src, lead, n, lambda i, j: i * nt + j) for src, lead, n in casts]
    kern = functools.partial(_conv_kernel, aliased=into is not None, n_jobs=len(jobs))
    return pl.pallas_call(
        kern, grid=(b, nt), in_specs=specs + [j[1] for j in jobs], out_specs=[blk, st] + [j[2] for j in jobs],
        out_shape=[jax.ShapeDtypeStruct(x.shape, f32),
                   jax.ShapeDtypeStruct((b, CONV_WIDTH - 1, D_MODEL), f32)] + [j[3] for j in jobs],
        scratch_shapes=[pltpu.VMEM((8, D_MODEL), f32)], input_output_aliases=aliases,
        name="conv_mixer", compiler_params=_params(2))(*ins, *[j[0] for j in jobs])


def _head_norm(t, gain, seg_ref, exp_ref):
    ms = _dot((t * t).astype(bf16), seg_ref[...])
    r = lax.rsqrt(ms + NORM_EPS)
    r_hi = r.astype(bf16)
    r_lo = (r - r_hi.astype(f32)).astype(bf16)
    rb = _dot(jnp.concatenate([r_hi, r_lo], axis=1), exp_ref[...])
    return t * rb * gain


def _proj_kernel(x_ref, g_ref, w_ref, wft_ref, bcol_ref, brow_ref, seg_ref, exp_ref, gq_ref, gk_ref,
                 k_ref, v_ref, lf_ref, lft_ref, qb_ref, kb_ref, vb_ref):
    xn = _rms(x_ref[...], g_ref[...]).astype(bf16)
    p = _dot(xn, w_ref[...])
    q = _head_norm(p[:, :D_MODEL], gq_ref[...], seg_ref, exp_ref)
    k = _head_norm(p[:, D_MODEL:2 * D_MODEL], gk_ref[...], seg_ref, exp_ref)
    v = p[:, 2 * D_MODEL:3 * D_MODEL]
    k_ref[...] = k
    v_ref[...] = v
    lf_ref[...] = _log_sigmoid(p[:, 3 * D_MODEL:3 * D_MODEL + N_HEADS] + brow_ref[...])
    lft_ref[...] = _log_sigmoid(_dot_nt(wft_ref[...], xn) + bcol_ref[...])
    qb_ref[...] = (q * (HEAD_DIM ** -0.5)).astype(bf16)
    kb_ref[...] = k.astype(bf16)
    vb_ref[...] = v.astype(bf16)


def _attn_proj(x, g, w, wft, bcol, brow, seg, exp, gq, gk, *, rows, row0):
    blk = lambda dt: jax.ShapeDtypeStruct((rows, D_MODEL), dt)
    full = lambda shape: pl.BlockSpec(shape, lambda i: (0, 0))
    return pl.pallas_call(
        _proj_kernel, grid=(1,),
        in_specs=[pl.BlockSpec((rows, D_MODEL), lambda i: (row0 // rows, 0)),
                  _const_spec((1, D_MODEL)), _const_spec((D_MODEL, 3 * D_MODEL + F_PAD)),
                  _const_spec((N_HEADS, D_MODEL)), _const_spec((N_HEADS, 1)), _const_spec((1, N_HEADS)),
                  _const_spec((D_MODEL, LANES)), _const_spec((2 * LANES, D_MODEL)),
                  _const_spec((1, D_MODEL)), _const_spec((1, D_MODEL))],
        out_specs=[full((rows, D_MODEL)), full((rows, D_MODEL)), full((rows, N_HEADS)), full((N_HEADS, rows)),
                   full((rows, D_MODEL)), full((rows, D_MODEL)), full((rows, D_MODEL))],
        out_shape=[blk(f32), blk(f32), jax.ShapeDtypeStruct((rows, N_HEADS), f32),
                   jax.ShapeDtypeStruct((N_HEADS, rows), f32), blk(bf16), blk(bf16), blk(bf16)],
        name="attn_proj", compiler_params=_params(1))(x, g, w, wft, bcol, brow, seg, exp, gq, gk)


def _proj_prompt_kernel(x_ref, g_ref, w_ref, wqvt_ref, wft_ref, bcol_ref, brow_ref, seg_ref, exp_ref,
                        gqc_ref, gk_ref, tril_ref, triu_ref,
                        kt_ref, vt_ref, lft_ref, ka_ref, qat_ref, vtb_ref, crow_ref, ccol_ref):
    t = pl.program_id(1)
    tt = x_ref.shape[0]
    n_tiles = tt // ATT_TILE

    @pl.when(t == 0)
    def _():
        crow_ref[...] = jnp.zeros(crow_ref.shape, f32)
        ccol_ref[...] = jnp.zeros(ccol_ref.shape, f32)

    xn = _rms(x_ref[...], g_ref[...]).astype(bf16)
    p = _dot(xn, w_ref[...])
    ft = _dot_nt(wft_ref[...], xn)
    k = _head_norm(p[:, :D_MODEL], gk_ref[...], seg_ref, exp_ref)
    kt_ref[0] = k.T
    lf = _log_sigmoid(p[:, D_MODEL:] + brow_ref[...])

    c3 = _dot(tril_ref[...], jnp.concatenate(_split3(lf), axis=1))
    c = c3[:, :LANES] + c3[:, LANES:2 * LANES] + c3[:, 2 * LANES:] + crow_ref[0:1, :]
    crow_ref[0:1, :] = c[tt - 1:tt, :]
    neg_c = [-piece.astype(f32) for piece in _split3(c)]

    lft = _log_sigmoid(ft + bcol_ref[...])
    lft_ref[0] = lft
    ct3 = _dot(jnp.concatenate(_split3(lft), axis=0), triu_ref[...])
    ct = ct3[:N_HEADS] + ct3[N_HEADS:2 * N_HEADS] + ct3[2 * N_HEADS:] + ccol_ref[:, 0:1]
    ccol_ref[...] = jnp.broadcast_to(ct[:, tt - 1:tt], ccol_ref.shape)
    ct_pieces = [piece.astype(f32) for piece in _split3(ct)]

    qvt = _dot_nt(wqvt_ref[...], xn)
    qt = qvt[:D_MODEL]
    vt = qvt[D_MODEL:]
    vt_ref[0] = vt
    vtb = vt.astype(bf16)
    for j in range(n_tiles):
        vtb_ref[0, j] = vtb[:, j * ATT_TILE:(j + 1) * ATT_TILE]

    lane = lax.broadcasted_iota(jnp.int32, (1, LANES), 1)
    sub = lax.broadcasted_iota(jnp.int32, (SUBLANES, 1), 0)
    gqc = gqc_ref[...]
    for hp in range(N_HEADS // 2):
        k2 = k[:, hp * LANES:(hp + 1) * LANES]
        for s in range(2):
            h = 2 * hp + s
            base = k2 if s == 0 else pltpu.roll(k2, HEAD_DIM, 1)
            ext = jnp.zeros((tt, LANES), f32)
            for n in range(N_AUG):
                ext = jnp.where(lane == HEAD_DIM + N_AUG + n,
                                jnp.broadcast_to(neg_c[n][:, h:h + 1], (tt, LANES)), ext)
            ext = jnp.where(lane < HEAD_DIM + N_AUG, 1.0, ext)
            ka_ref[0, h] = jnp.where(lane < HEAD_DIM, base, ext).astype(bf16)

            qh = qt[h * HEAD_DIM:(h + 1) * HEAD_DIM, :]
            qn = qh * lax.rsqrt(jnp.mean(qh * qh, axis=0, keepdims=True) + NORM_EPS) * gqc
            aug = jnp.where(sub < 2 * N_AUG, 1.0, 0.0) * jnp.ones((1, tt), f32)
            for n in range(N_AUG):
                aug = jnp.where(sub == n, ct_pieces[n][h:h + 1, :], aug)
            blk = jnp.concatenate([qn * (HEAD_DIM ** -0.5), aug,
                                   jnp.zeros((LANES - HEAD_DIM - SUBLANES, tt), f32)], axis=0).astype(bf16)
            for j in range(n_tiles):
                qat_ref[0, j, h * LANES:(h + 1) * LANES, :] = blk[:, j * ATT_TILE:(j + 1) * ATT_TILE]


def _attn_proj_prompt(x, g, w, wqvt, wft, bcol, brow, seg, exp, gqc, gk, tril, triu, *, b, t):
    tt = ROW_TILE
    nt = tt // ATT_TILE
    time_minor = pl.BlockSpec((1, D_MODEL, tt), lambda i, j: (i, 0, j))
    big = jax.ShapeDtypeStruct((b, D_MODEL, t), f32)
    return pl.pallas_call(
        _proj_prompt_kernel, grid=(b, t // tt),
        in_specs=[pl.BlockSpec((tt, D_MODEL), lambda i, j: (i * (t // tt) + j, 0)),
                  _const_spec((1, D_MODEL)), _const_spec((D_MODEL, D_MODEL + F_PAD)),
                  _const_spec((2 * D_MODEL, D_MODEL)), _const_spec((N_HEADS, D_MODEL)),
                  _const_spec((N_HEADS, 1)), _const_spec((1, F_PAD)),
                  _const_spec((D_MODEL, LANES)), _const_spec((2 * LANES, D_MODEL)),
                  _const_spec((HEAD_DIM, tt)), _const_spec((1, D_MODEL)),
                  _const_spec((tt, tt)), _const_spec((tt, tt))],
        out_specs=[time_minor, time_minor,
                   pl.BlockSpec((1, N_HEADS, tt), lambda i, j: (i, 0, j)),
                   pl.BlockSpec((1, N_HEADS, tt, LANES), lambda i, j: (i, 0, j, 0)),
                   pl.BlockSpec((1, nt, N_HEADS * LANES, ATT_TILE), lambda i, j: (i, j, 0, 0)),
                   pl.BlockSpec((1, nt, D_MODEL, ATT_TILE), lambda i, j: (i, j, 0, 0))],
        out_shape=[big, big, jax.ShapeDtypeStruct((b, N_HEADS, t), f32),
                   jax.ShapeDtypeStruct((b, N_HEADS, t, LANES), bf16),
                   jax.ShapeDtypeStruct((b, t // ATT_TILE, N_HEADS * LANES, ATT_TILE), bf16),
                   jax.ShapeDtypeStruct((b, t // ATT_TILE, D_MODEL, ATT_TILE), bf16)],
        scratch_shapes=[pltpu.VMEM((SUBLANES, LANES), f32), pltpu.VMEM((N_HEADS, LANES), f32)],
        name="attn_proj_prompt", compiler_params=_params(2))(
            x, g, w, wqvt, wft, bcol, brow, seg, exp, gqc, gk, tril, triu)


def _scan_kernel(x_ref, tri_ref, o_ref):
    n = x_ref.shape[2] // SCAN_TILE
    carry = jnp.zeros((N_HEADS, 1), f32)
    for c in range(n):
        sl = slice(c * SCAN_TILE, (c + 1) * SCAN_TILE)
        cs = _dot(jnp.concatenate(_split3(x_ref[0, :, sl]), axis=0), tri_ref[...])
        out = cs[:N_HEADS] + cs[N_HEADS:2 * N_HEADS] + cs[2 * N_HEADS:] + carry
        o_ref[0, :, sl] = out
        carry = out[:, SCAN_TILE - 1:SCAN_TILE]


def _cumsum_lanes(lft, tri):
    b, h, t = lft.shape
    blk = pl.BlockSpec((1, h, t), lambda i: (i, 0, 0))
    return pl.pallas_call(
        _scan_kernel, grid=(b,), in_specs=[blk, _const_spec((SCAN_TILE, SCAN_TILE))], out_specs=blk,
        out_shape=jax.ShapeDtypeStruct((b, h, t), f32), name="cumsum_lanes",
        compiler_params=_params(1))(lft, tri)


def _attn_kernel(qat_ref, ka_ref, vt_ref, o_ref, m_ref, l_ref, acc_ref):
    i = pl.program_id(2)
    tq = qat_ref.shape[3]
    tk = vt_ref.shape[3]
    causal = (lax.broadcasted_iota(jnp.int32, (tk, tq), 0) <= lax.broadcasted_iota(jnp.int32, (tk, tq), 1))
    m_ref[...] = jnp.full(m_ref.shape, NEG_INF, f32)
    l_ref[...] = jnp.zeros(l_ref.shape, f32)
    acc_ref[...] = jnp.zeros(acc_ref.shape, f32)

    def tile(j, masked):
        rows = pl.ds(pl.multiple_of(j * tk, tk), tk)

        def scores(s):
            return _dot(ka_ref[0, s, rows, :], qat_ref[0, 0, s * LANES:(s + 1) * LANES, :])

        ahead = [scores(s) for s in range(ATT_SKEW)]
        for s in range(ATT_HEADS):
            sc = ahead.pop(0)
            if s + ATT_SKEW < ATT_HEADS:
                ahead.append(scores(s + ATT_SKEW))
            if masked:
                sc = jnp.where(causal, sc, NEG_INF)
            m = m_ref[s]
            m_new = jnp.maximum(m, jnp.max(sc, axis=0, keepdims=True))
            alpha = jnp.exp(m - m_new)
            pr = jnp.exp(sc - m_new)
            l_ref[s] = alpha * l_ref[s] + jnp.sum(pr, axis=0, keepdims=True)
            vt = vt_ref[0, j, s * HEAD_DIM:(s + 1) * HEAD_DIM, :]
            acc_ref[s] = alpha * acc_ref[s] + _dot(vt, pr.astype(bf16))
            m_ref[s] = m_new

    def body(j, carry):
        tile(j, False)
        return carry

    lax.fori_loop(0, i, body, 0)
    tile(i, True)
    o_t = jnp.concatenate([acc_ref[s] / l_ref[s] for s in range(ATT_HEADS)], axis=0)
    o_ref[0] = o_t.T.astype(bf16)


def _attn_prompt(qat, ka, vt):
    b, nq, _, tq = qat.shape
    s = nq * tq
    g = ATT_HEADS
    return pl.pallas_call(
        _attn_kernel, grid=(b, N_HEADS // g, nq),
        in_specs=[pl.BlockSpec((1, 1, g * LANES, tq), lambda i, h, j: (i, j, h, 0)),
                  pl.BlockSpec((1, g, s, LANES), lambda i, h, j: (i, h, 0, 0)),
                  pl.BlockSpec((1, nq, g * HEAD_DIM, tq), lambda i, h, j: (i, 0, h, 0))],
        out_specs=pl.BlockSpec((1, tq, g * HEAD_DIM), lambda i, h, j: (i, j, h)),
        out_shape=jax.ShapeDtypeStruct((b, s, D_MODEL), bf16),
        scratch_shapes=[pltpu.VMEM((g, 1, tq), f32), pltpu.VMEM((g, 1, tq), f32),
                        pltpu.VMEM((g, HEAD_DIM, tq), f32)],
        name="attn_prompt",
        compiler_params=_params(3))(qat, ka, vt)


def _attn_sample_kernel(q_ref, kn_ref, vn_ref, kp_ref, vp_ref, cq_ref, ck_ref, o_ref):
    t = q_ref.shape[1]
    past = kp_ref.shape[2]
    low = lax.broadcasted_iota(jnp.int32, (1, LANES), 1) < HEAD_DIM
    causal = (lax.broadcasted_iota(jnp.int32, (t, t), 1) <= lax.broadcasted_iota(jnp.int32, (t, t), 0))
    for hp in range(N_HEADS // 2):
        lanes = slice(hp * LANES, (hp + 1) * LANES)
        q2 = q_ref[0, :, lanes]
        kp2 = kp_ref[0, lanes, :].astype(bf16)
        vp2 = vp_ref[0, lanes, :].astype(bf16)
        kn2 = kn_ref[0, :, lanes]
        vn2 = vn_ref[0, :, lanes]
        halves = []
        for sub in range(2):
            h = 2 * hp + sub
            qm = jnp.where(low if sub == 0 else jnp.logical_not(low), q2, jnp.zeros_like(q2))
            cq = cq_ref[0, :, h:h + 1]
            s_past = _dot(qm, kp2) + cq - ck_ref[0, h:h + 1, :past]
            s_new = _dot_nt(qm, kn2) + cq - ck_ref[0, h:h + 1, past:past + t]
            s_new = jnp.where(causal, s_new, NEG_INF)
            m = jnp.maximum(jnp.max(s_past, axis=1, keepdims=True), jnp.max(s_new, axis=1, keepdims=True))
            p_past = jnp.exp(s_past - m)
            p_new = jnp.exp(s_new - m)
            l = jnp.sum(p_past, axis=1, keepdims=True) + jnp.sum(p_new, axis=1, keepdims=True)
            acc = _dot_nt(p_past.astype(bf16), vp2) + _dot(p_new.astype(bf16), vn2)
            halves.append(acc / l)
        o_ref[0, :, lanes] = jnp.where(low, halves[0], halves[1]).astype(bf16)


def _attn_sample(qb, kb, vb, past_kt, past_vt, c_nat, c_t):
    b, t, _ = qb.shape
    past = past_kt.shape[2]
    new = pl.BlockSpec((1, t, D_MODEL), lambda i: (i, 0, 0))
    old = pl.BlockSpec((1, D_MODEL, past), lambda i: (i, 0, 0))
    return pl.pallas_call(
        _attn_sample_kernel, grid=(b,),
        in_specs=[new, new, new, old, old,
                  pl.BlockSpec((1, t, N_HEADS), lambda i: (i, 0, 0)),
                  pl.BlockSpec((1, N_HEADS, c_t.shape[2]), lambda i: (i, 0, 0))],
        out_specs=new,
        out_shape=jax.ShapeDtypeStruct((b, t, D_MODEL), bf16),
        name="attn_sample", compiler_params=_params(1))(qb, kb, vb, past_kt, past_vt, c_nat, c_t)


def kernel(x_prompt, x_sample, state_conv, cache_k, cache_v, cache_logf, norm_ffn, ffn_w_in, ffn_w_out, norm_mix, conv_w_in, conv_w, conv_w_out, attn_w_in, attn_b_f, q_norm, k_norm, attn_w_out):
    bp, sp, d = x_prompt.shape
    bs, ss, _ = x_sample.shape
    past = cache_k.shape[2]

    n_p, n_s = bp * sp, bs * ss
    def ffn(xs, i, j, w, **kw):
        return _ffn(xs, norm_ffn[i, j].reshape(1, d), *w, **kw)

    w_slabs, wo_slabs = 32, 16
    ffn_casts = lambda i, j: [(ffn_w_in, (i, j), w_slabs), (ffn_w_out, (i, j), wo_slabs)]
    square_cast = lambda w: (w, (0,), w_slabs)

    y, conv_in_b, conv_out_b = ffn((x_prompt.reshape(n_p, d), x_sample.reshape(n_s, d)), 0, 0,
                                   (ffn_w_in[0, 0].astype(bf16), ffn_w_out[0, 0].astype(bf16)),
                                   casts=[square_cast(conv_w_in), square_cast(conv_w_out)])
    cw = (norm_mix[0].reshape(1, d), conv_in_b, conv_w[0], conv_out_b)
    y_mixed, conv_p, *w01 = _conv_mixer(y, jnp.zeros((bp, CONV_WIDTH - 1, d), f32), *cw, t=sp, row0=0,
                                        casts=ffn_casts(0, 1))
    y, conv_s = _conv_mixer(y, state_conv[0], *cw, t=ss, row0=n_p, into=y_mixed)
    y, *w10 = ffn(y, 0, 1, w01, casts=ffn_casts(1, 0))

    y, *w11, attn_out_b = ffn(y, 1, 0, w10, casts=ffn_casts(1, 1) + [square_cast(attn_w_out)])

    w_in = attn_w_in[0]
    f_pad = jnp.zeros((d, F_PAD - N_HEADS), f32)
    wft = w_in[:, 3 * d:].T.astype(bf16)
    head_of = jnp.arange(d) // HEAD_DIM
    seg = ((head_of[:, None] == jnp.arange(LANES)[None, :]).astype(f32) * (1.0 / HEAD_DIM)).astype(bf16)
    expand = (jnp.arange(LANES)[:, None] == head_of[None, :]).astype(bf16)
    expand2 = jnp.concatenate([expand, expand], axis=0)
    g_mix = norm_mix[1].reshape(1, d)
    bcol = attn_b_f[0].reshape(N_HEADS, 1)
    gk = jnp.tile(k_norm[0], N_HEADS).reshape(1, d)

    tril = jnp.tril(jnp.ones((ROW_TILE, ROW_TILE), f32)).astype(bf16)
    kpt, vpt, fpt, ka, qat, vt = _attn_proj_prompt(
        y, g_mix, jnp.concatenate([w_in[:, d:2 * d], w_in[:, 3 * d:], f_pad], axis=1).astype(bf16),
        jnp.concatenate([w_in[:, :d], w_in[:, 2 * d:3 * d]], axis=1).T.astype(bf16), wft, bcol,
        jnp.concatenate([attn_b_f[0], jnp.zeros((F_PAD - N_HEADS,), f32)]).reshape(1, F_PAD),
        seg, expand2, jnp.broadcast_to(q_norm[0][:, None], (HEAD_DIM, ROW_TILE)), gk, tril, tril.T, b=bp, t=sp)
    ap = _attn_prompt(qat, ka, vt)

    kq, vq, fq, fqt, qsb, ksb, vsb = _attn_proj(
        y, g_mix, jnp.concatenate([w_in, f_pad], axis=1).astype(bf16), wft, bcol,
        attn_b_f[0].reshape(1, N_HEADS), seg, expand2, jnp.tile(q_norm[0], N_HEADS).reshape(1, d), gk,
        rows=n_s, row0=n_p)
    pad = (-(past + ss)) % SCAN_TILE
    lf_all = jnp.concatenate([jnp.swapaxes(cache_logf[0], 1, 2),
                              jnp.swapaxes(fqt.reshape(N_HEADS, bs, ss), 0, 1),
                              jnp.zeros((bs, N_HEADS, pad), f32)], axis=2)
    cst = _cumsum_lanes(lf_all, jnp.triu(jnp.ones((SCAN_TILE, SCAN_TILE), f32)).astype(bf16))
    per_stream = lambda a: a.reshape(bs, ss, d)
    channel_major = lambda a: jnp.transpose(a, (0, 2, 3, 1)).reshape(bs, d, past)
    a_s = _attn_sample(per_stream(qsb), per_stream(ksb), per_stream(vsb),
                       channel_major(cache_k[0]), channel_major(cache_v[0]),
                       jnp.swapaxes(cst[:, :, past:past + ss], 1, 2), cst)

    yp, ys = ffn(y, 1, 1, w11, mix=(ap.reshape(n_p, d), a_s.reshape(n_s, d), attn_out_b))

    hd = (N_HEADS, HEAD_DIM)
    token_major = lambda a: jnp.transpose(a.reshape(bp, *hd, sp), (0, 3, 1, 2))[None]
    return (yp.reshape(bp, sp, d), ys.reshape(bs, ss, d),
            conv_p[None], conv_s[None],
            token_major(kpt), token_major(vpt), jnp.swapaxes(fpt, 1, 2)[None],
            kq.reshape(1, bs, ss, *hd), vq.reshape(1, bs, ss, *hd), fq.reshape(1, bs, ss, N_HEADS))
```

```python
import functools

import jax
import jax.numpy as jnp
from jax import lax
from jax.experimental import pallas as pl
from jax.experimental.pallas import tpu as pltpu

D_MODEL = 1024
N_HEADS = 16
HEAD_DIM = 64
D_FF = 2816
CONV_WIDTH = 3
NORM_EPS = 1e-6
NEG_INF = -1e30
FFN_RESIDUAL = 0.5

LANES = 128
SUBLANES = 8
F_PAD = LANES
VMEM_LIMIT = 56 * 1024 * 1024
FF_CHUNKS = ((0, 1536), (1536, 2816))
ROW_TILE = 512
ATT_TILE = 256
ATT_HEADS = 8
SCAN_TILE = 256
N_AUG = 3
V_ROWS = HEAD_DIM + 16
LOG2_E = 1.4426950408889634

f32 = jnp.float32
bf16 = jnp.bfloat16


def _dot(a, b):
    return jnp.dot(a, b, preferred_element_type=f32)


def _dot_nt(a, b):
    return lax.dot_general(a, b, (((1,), (1,)), ((), ())), preferred_element_type=f32)


def _rms(x, g):
    return x * lax.rsqrt(jnp.mean(x * x, axis=-1, keepdims=True) + NORM_EPS) * g


def _log_sigmoid(z):
    return -(jnp.maximum(-z, 0.0) + jnp.log1p(jnp.exp(-jnp.abs(z))))


def _split3(x):
    hi = x.astype(bf16)
    r1 = x - hi.astype(f32)
    mid = r1.astype(bf16)
    lo = (r1 - mid.astype(f32)).astype(bf16)
    return hi, mid, lo


def _params(n_axes, flags=None):
    return pltpu.CompilerParams(dimension_semantics=("arbitrary",) * n_axes,
                                vmem_limit_bytes=VMEM_LIMIT, flags=flags)


def _const_spec(shape):
    return pl.BlockSpec(shape, lambda *_: (0,) * len(shape), pipeline_mode=pl.Buffered(1))


def _cast_job(src, lead, steps, lin):
    rows, cols = src.shape[-2:]
    slab = rows // steps
    assert slab * steps == rows and slab % 16 == 0
    pos = lambda *idx: jnp.minimum(lin(*idx), steps - 1)
    in_spec = pl.BlockSpec((None,) * len(lead) + (slab, cols), lambda *idx: (*lead, pos(*idx), 0))
    out_spec = pl.BlockSpec((slab, cols), lambda *idx: (pos(*idx), 0))
    return src, in_spec, out_spec, jax.ShapeDtypeStruct((rows, cols), bf16), steps


def _run_casts(srcs, dsts):
    for src_ref, dst_ref in zip(srcs, dsts):
        dst_ref[...] = src_ref[...].astype(bf16)


def _ffn_kernel(*refs, n_prompt, split_in, mix, n_jobs):
    refs = list(refs)
    cast_srcs = cast_dsts = ()
    if n_jobs:
        cast_dsts = refs[-n_jobs:]
        del refs[-n_jobs:]
        n_out = 2 if mix else 1
        cast_srcs = refs[-n_out - n_jobs:-n_out]
        del refs[-n_out - n_jobs:-n_out]
    is_prompt = pl.program_id(0) < n_prompt

    def rows():
        if split_in or mix:
            p_ref, s_ref = refs.pop(0), refs.pop(0)
            return jnp.where(is_prompt, p_ref[...], s_ref[...])
        return refs.pop(0)[...]

    if mix:
        x = refs.pop(0)[...]
        a = rows()
        x = x + _dot(a, refs.pop(0)[...])
    else:
        x = rows()
    g_ref, win_ref, wout_ref = refs[:3]
    outs = refs[3:]
    xn = _rms(x, g_ref[...]).astype(bf16)
    acc = None
    for lo, hi in FF_CHUNKS:
        a = _dot(xn, win_ref[:, lo:hi])
        b = _dot(xn, win_ref[:, D_FF + lo:D_FF + hi])
        if lo == 0:
            _run_casts(cast_srcs, cast_dsts)
        h = (a * jax.nn.sigmoid(a) * b).astype(bf16)
        y = _dot(h, wout_ref[lo:hi, :])
        acc = y if acc is None else acc + y
    y = x + FFN_RESIDUAL * acc
    if mix:
        @pl.when(is_prompt)
        def _():
            outs[0][...] = y

        @pl.when(jnp.logical_not(is_prompt))
        def _():
            outs[1][...] = y
    else:
        outs[0][...] = y


def _ffn(xs, g, w_in, w_out, mix=None, casts=()):
    tm = ROW_TILE
    split_in = isinstance(xs, tuple)
    if mix is None and split_in:
        n_prompt, n_sample = xs[0].shape[0] // tm, xs[1].shape[0] // tm
    elif mix is not None:
        n_prompt, n_sample = mix[0].shape[0] // tm, mix[1].shape[0] // tm
    else:
        n_prompt, n_sample = xs.shape[0] // tm, 0
    assert n_sample in (0, 1)
    steps = n_prompt + n_sample
    row = pl.BlockSpec((tm, D_MODEL), lambda i: (i, 0))
    prompt_row = pl.BlockSpec((tm, D_MODEL), lambda i: (jnp.minimum(i, n_prompt - 1), 0))
    sample_row = pl.BlockSpec((tm, D_MODEL), lambda i: (0, 0))
    w_specs = [_const_spec((1, D_MODEL)), _const_spec((D_MODEL, 2 * D_FF)), _const_spec((D_FF, D_MODEL))]
    stacked = jax.ShapeDtypeStruct((steps * tm, D_MODEL), f32)
    if mix is not None:
        a_p, a_s, wm = mix
        ins = (xs, a_p, a_s, wm)
        specs = [row, prompt_row, sample_row, _const_spec((D_MODEL, D_MODEL))]
        out_specs = [prompt_row, sample_row]
        out_shape = [jax.ShapeDtypeStruct((n_prompt * tm, D_MODEL), f32),
                     jax.ShapeDtypeStruct((n_sample * tm, D_MODEL), f32)]
    elif split_in:
        ins, specs, out_specs, out_shape = tuple(xs), [prompt_row, sample_row], [row], [stacked]
    else:
        ins, specs, out_specs, out_shape = (xs,), [row], [row], [stacked]
    jobs = [_cast_job(src, lead, n, lambda i: i) for src, lead, n in casts]
    kern = functools.partial(_ffn_kernel, n_prompt=n_prompt, split_in=split_in, mix=mix is not None,
                             n_jobs=len(jobs))
    return pl.pallas_call(
        kern, grid=(steps,), in_specs=specs + w_specs + [j[1] for j in jobs],
        out_specs=out_specs + [j[2] for j in jobs], out_shape=out_shape + [j[3] for j in jobs],
        name="ffn" if mix is None else "mix_ffn",
        compiler_params=_params(1))(*ins, g, w_in, w_out, *[j[0] for j in jobs])


def _conv_kernel(x_ref, hist_ref, g_ref, win_ref, wk_ref, wout_ref, *rest, aliased, n_jobs):
    rest = rest[1:] if aliased else rest
    o_ref, st_ref = rest[n_jobs:n_jobs + 2]
    carry_ref = rest[-1]
    t = pl.program_id(1)
    tt = x_ref.shape[0]

    @pl.when(t == 0)
    def _():
        carry_ref[0:2, :] = hist_ref[0]

    x = x_ref[...]
    xn = _rms(x, g_ref[...]).astype(bf16)
    p = _dot(xn, win_ref[...])
    _run_casts(rest[:n_jobs], rest[n_jobs + 2:-1])
    gate_b = p[:, :D_MODEL]
    u = p[:, D_MODEL:2 * D_MODEL] * p[:, 2 * D_MODEL:]
    prev2 = carry_ref[0:1, :]
    prev1 = carry_ref[1:2, :]
    row = lax.broadcasted_iota(jnp.int32, (tt, 1), 0)
    u1 = jnp.where(row == 0, prev1, pltpu.roll(u, 1, 0))
    u2 = jnp.where(row == 0, prev2, jnp.where(row == 1, prev1, pltpu.roll(u, 2, 0)))
    wk = wk_ref[...]
    conv = wk[0:1, :] * u2 + wk[1:2, :] * u1 + wk[2:3, :] * u
    y = _dot((gate_b * conv).astype(bf16), wout_ref[...])
    o_ref[...] = x + y
    last = u[tt - 2:tt, :]
    carry_ref[0:2, :] = last
    st_ref[0] = last


def _conv_mixer(x, hist, g, win, wk, wout, *, t, row0, into=None, casts=()):
    b = hist.shape[0]
    tt = min(ROW_TILE, t)
    nt, base = t // tt, row0 // tt
    blk = pl.BlockSpec((tt, D_MODEL), lambda i, j: (base + i * nt + j, 0))
    st = pl.BlockSpec((1, CONV_WIDTH - 1, D_MODEL), lambda i, j: (i, 0, 0))
    ins = [x, hist, g, win, wk, wout]
    specs = [blk, st, _const_spec((1, D_MODEL)), _const_spec((D_MODEL, 3 * D_MODEL)),
             _const_spec((CONV_WIDTH, D_MODEL)), _const_spec((D_MODEL, D_MODEL))]
    aliases = {}
    if into is not None:
        aliases = {len(ins): 0}
        ins.append(into)
        specs.append(pl.BlockSpec(memory_space=pl.ANY))
    jobs = [_cast_job(src, lead, n, lambda i, j: i * nt + j) for src, lead, n in casts]
    kern = functools.partial(_conv_kernel, aliased=into is not None, n_jobs=len(jobs))
    return pl.pallas_call(
        kern, grid=(b, nt), in_specs=specs + [j[1] for j in jobs], out_specs=[blk, st] + [j[2] for j in jobs],
        out_shape=[jax.ShapeDtypeStruct(x.shape, f32),
                   jax.ShapeDtypeStruct((b, CONV_WIDTH - 1, D_MODEL), f32)] + [j[3] for j in jobs],
        scratch_shapes=[pltpu.VMEM((8, D_MODEL), f32)], input_output_aliases=aliases,
        name="conv_mixer", compiler_params=_params(2))(*ins, *[j[0] for j in jobs])


def _head_norm(t, gain, seg_ref, exp_ref):
    ms = _dot((t * t).astype(bf16), seg_ref[...])
    r = lax.rsqrt(ms + NORM_EPS)
    r_hi = r.astype(bf16)
    r_lo = (r - r_hi.astype(f32)).astype(bf16)
    rb = _dot(jnp.concatenate([r_hi, r_lo], axis=1), exp_ref[...])
    return t * rb * gain


def _proj_kernel(x_ref, g_ref, w_ref, wft_ref, bcol_ref, brow_ref, seg_ref, exp_ref, gq_ref, gk_ref,
                 k_ref, v_ref, lf_ref, lft_ref, qb_ref, kb_ref, vb_ref):
    xn = _rms(x_ref[...], g_ref[...]).astype(bf16)
    p = _dot(xn, w_ref[...])
    q = _head_norm(p[:, :D_MODEL], gq_ref[...], seg_ref, exp_ref)
    k = _head_norm(p[:, D_MODEL:2 * D_MODEL], gk_ref[...], seg_ref, exp_ref)
    v = p[:, 2 * D_MODEL:3 * D_MODEL]
    k_ref[...] = k
    v_ref[...] = v
    lf_ref[...] = _log_sigmoid(p[:, 3 * D_MODEL:3 * D_MODEL + N_HEADS] + brow_ref[...])
    lft_ref[...] = _log_sigmoid(_dot_nt(wft_ref[...], xn) + bcol_ref[...])
    qb_ref[...] = (q * (HEAD_DIM ** -0.5)).astype(bf16)
    kb_ref[...] = k.astype(bf16)
    vb_ref[...] = v.astype(bf16)


def _attn_proj(x, g, w, wft, bcol, brow, seg, exp, gq, gk, *, rows, row0):
    blk = lambda dt: jax.ShapeDtypeStruct((rows, D_MODEL), dt)
    full = lambda shape: pl.BlockSpec(shape, lambda i: (0, 0))
    return pl.pallas_call(
        _proj_kernel, grid=(1,),
        in_specs=[pl.BlockSpec((rows, D_MODEL), lambda i: (row0 // rows, 0)),
                  _const_spec((1, D_MODEL)), _const_spec((D_MODEL, 3 * D_MODEL + F_PAD)),
                  _const_spec((N_HEADS, D_MODEL)), _const_spec((N_HEADS, 1)), _const_spec((1, N_HEADS)),
                  _const_spec((D_MODEL, LANES)), _const_spec((2 * LANES, D_MODEL)),
                  _const_spec((1, D_MODEL)), _const_spec((1, D_MODEL))],
        out_specs=[full((rows, D_MODEL)), full((rows, D_MODEL)), full((rows, N_HEADS)), full((N_HEADS, rows)),
                   full((rows, D_MODEL)), full((rows, D_MODEL)), full((rows, D_MODEL))],
        out_shape=[blk(f32), blk(f32), jax.ShapeDtypeStruct((rows, N_HEADS), f32),
                   jax.ShapeDtypeStruct((N_HEADS, rows), f32), blk(bf16), blk(bf16), blk(bf16)],
        name="attn_proj", compiler_params=_params(1))(x, g, w, wft, bcol, brow, seg, exp, gq, gk)


def _proj_prompt_kernel(x_ref, g_ref, w_ref, wqvt_ref, wft_ref, bcol_ref, brow_ref, seg_ref, exp_ref,
                        gqc_ref, gk_ref, tril_ref, triu_ref,
                        kt_ref, vt_ref, lft_ref, ka_ref, qat_ref, vtb_ref, crow_ref, ccol_ref):
    t = pl.program_id(1)
    tt = x_ref.shape[0]
    n_tiles = tt // ATT_TILE

    @pl.when(t == 0)
    def _():
        crow_ref[...] = jnp.zeros(crow_ref.shape, f32)
        ccol_ref[...] = jnp.zeros(ccol_ref.shape, f32)

    xn = _rms(x_ref[...], g_ref[...]).astype(bf16)
    p = _dot(xn, w_ref[...])
    ft = _dot_nt(wft_ref[...], xn)
    k = _head_norm(p[:, :D_MODEL], gk_ref[...], seg_ref, exp_ref)
    kt_ref[0] = k.T
    lf = _log_sigmoid(p[:, D_MODEL:] + brow_ref[...])

    c3 = _dot(tril_ref[...], jnp.concatenate(_split3(lf), axis=1))
    c = c3[:, :LANES] + c3[:, LANES:2 * LANES] + c3[:, 2 * LANES:] + crow_ref[0:1, :]
    crow_ref[0:1, :] = c[tt - 1:tt, :]
    neg_c = [-piece.astype(f32) for piece in _split3(c * LOG2_E)]

    lft = _log_sigmoid(ft + bcol_ref[...])
    lft_ref[0] = lft
    ct3 = _dot(jnp.concatenate(_split3(lft), axis=0), triu_ref[...])
    ct = ct3[:N_HEADS] + ct3[N_HEADS:2 * N_HEADS] + ct3[2 * N_HEADS:] + ccol_ref[:, 0:1]
    ccol_ref[...] = jnp.broadcast_to(ct[:, tt - 1:tt], ccol_ref.shape)
    ct_pieces = [piece.astype(f32) for piece in _split3(ct * LOG2_E)]

    qvt = _dot_nt(wqvt_ref[...], xn)
    qt = qvt[:D_MODEL]
    vt = qvt[D_MODEL:]
    vt_ref[0] = vt
    vtb = vt.astype(bf16)
    ones_rows = jnp.where(lax.broadcasted_iota(jnp.int32, (V_ROWS - HEAD_DIM, ATT_TILE), 0) == 0, 1.0, 0.0).astype(bf16)
    for j in range(n_tiles):
        for h in range(N_HEADS):
            vtb_ref[0, j, h * V_ROWS:h * V_ROWS + HEAD_DIM, :] = (
                vtb[h * HEAD_DIM:(h + 1) * HEAD_DIM, j * ATT_TILE:(j + 1) * ATT_TILE])
            vtb_ref[0, j, h * V_ROWS + HEAD_DIM:(h + 1) * V_ROWS, :] = ones_rows

    lane = lax.broadcasted_iota(jnp.int32, (1, LANES), 1)
    sub = lax.broadcasted_iota(jnp.int32, (SUBLANES, 1), 0)
    gqc = gqc_ref[...]
    for hp in range(N_HEADS // 2):
        k2 = k[:, hp * LANES:(hp + 1) * LANES]
        for s in range(2):
            h = 2 * hp + s
            base = k2 if s == 0 else pltpu.roll(k2, HEAD_DIM, 1)
            ext = jnp.zeros((tt, LANES), f32)
            for n in range(N_AUG):
                ext = jnp.where(lane == HEAD_DIM + N_AUG + n,
                                jnp.broadcast_to(neg_c[n][:, h:h + 1], (tt, LANES)), ext)
            ext = jnp.where(lane < HEAD_DIM + N_AUG, 1.0, ext)
            ka_ref[0, h] = jnp.where(lane < HEAD_DIM, base, ext).astype(bf16)

            qh = qt[h * HEAD_DIM:(h + 1) * HEAD_DIM, :]
            qn = qh * lax.rsqrt(jnp.mean(qh * qh, axis=0, keepdims=True) + NORM_EPS) * gqc
            aug = jnp.where(sub < 2 * N_AUG, 1.0, 0.0) * jnp.ones((1, tt), f32)
            for n in range(N_AUG):
                aug = jnp.where(sub == n, ct_pieces[n][h:h + 1, :], aug)
            blk = jnp.concatenate([qn * (LOG2_E * HEAD_DIM ** -0.5), aug,
                                   jnp.zeros((LANES - HEAD_DIM - SUBLANES, tt), f32)], axis=0).astype(bf16)
            for j in range(n_tiles):
                qat_ref[0, j, h * LANES:(h + 1) * LANES, :] = blk[:, j * ATT_TILE:(j + 1) * ATT_TILE]


def _attn_proj_prompt(x, g, w, wqvt, wft, bcol, brow, seg, exp, gqc, gk, tril, triu, *, b, t):
    tt = ROW_TILE
    nt = tt // ATT_TILE
    time_minor = pl.BlockSpec((1, D_MODEL, tt), lambda i, j: (i, 0, j))
    big = jax.ShapeDtypeStruct((b, D_MODEL, t), f32)
    return pl.pallas_call(
        _proj_prompt_kernel, grid=(b, t // tt),
        in_specs=[pl.BlockSpec((tt, D_MODEL), lambda i, j: (i * (t // tt) + j, 0)),
                  _const_spec((1, D_MODEL)), _const_spec((D_MODEL, D_MODEL + F_PAD)),
                  _const_spec((2 * D_MODEL, D_MODEL)), _const_spec((N_HEADS, D_MODEL)),
                  _const_spec((N_HEADS, 1)), _const_spec((1, F_PAD)),
                  _const_spec((D_MODEL, LANES)), _const_spec((2 * LANES, D_MODEL)),
                  _const_spec((HEAD_DIM, tt)), _const_spec((1, D_MODEL)),
                  _const_spec((tt, tt)), _const_spec((tt, tt))],
        out_specs=[time_minor, time_minor,
                   pl.BlockSpec((1, N_HEADS, tt), lambda i, j: (i, 0, j)),
                   pl.BlockSpec((1, N_HEADS, tt, LANES), lambda i, j: (i, 0, j, 0)),
                   pl.BlockSpec((1, nt, N_HEADS * LANES, ATT_TILE), lambda i, j: (i, j, 0, 0)),
                   pl.BlockSpec((1, nt, N_HEADS * V_ROWS, ATT_TILE), lambda i, j: (i, j, 0, 0))],
        out_shape=[big, big, jax.ShapeDtypeStruct((b, N_HEADS, t), f32),
                   jax.ShapeDtypeStruct((b, N_HEADS, t, LANES), bf16),
                   jax.ShapeDtypeStruct((b, t // ATT_TILE, N_HEADS * LANES, ATT_TILE), bf16),
                   jax.ShapeDtypeStruct((b, t // ATT_TILE, N_HEADS * V_ROWS, ATT_TILE), bf16)],
        scratch_shapes=[pltpu.VMEM((SUBLANES, LANES), f32), pltpu.VMEM((N_HEADS, LANES), f32)],
        name="attn_proj_prompt", compiler_params=_params(2))(
            x, g, w, wqvt, wft, bcol, brow, seg, exp, gqc, gk, tril, triu)


def _scan_kernel(x_ref, tri_ref, o_ref):
    n = x_ref.shape[2] // SCAN_TILE
    carry = jnp.zeros((N_HEADS, 1), f32)
    for c in range(n):
        sl = slice(c * SCAN_TILE, (c + 1) * SCAN_TILE)
        cs = _dot(jnp.concatenate(_split3(x_ref[0, :, sl]), axis=0), tri_ref[...])
        out = cs[:N_HEADS] + cs[N_HEADS:2 * N_HEADS] + cs[2 * N_HEADS:] + carry
        o_ref[0, :, sl] = out
        carry = out[:, SCAN_TILE - 1:SCAN_TILE]


def _cumsum_lanes(lft, tri):
    b, h, t = lft.shape
    blk = pl.BlockSpec((1, h, t), lambda i: (i, 0, 0))
    return pl.pallas_call(
        _scan_kernel, grid=(b,), in_specs=[blk, _const_spec((SCAN_TILE, SCAN_TILE))], out_specs=blk,
        out_shape=jax.ShapeDtypeStruct((b, h, t), f32), name="cumsum_lanes",
        compiler_params=_params(1))(lft, tri)


def _attn_kernel(qat_ref, ka_ref, vt_ref, o_ref, m_ref, acc_ref, sa_ref, sb_ref):
    i = pl.program_id(2)
    tq = qat_ref.shape[3]
    tk = vt_ref.shape[3]
    causal = (lax.broadcasted_iota(jnp.int32, (tk, tq), 0) <= lax.broadcasted_iota(jnp.int32, (tk, tq), 1))
    m_ref[...] = jnp.full(m_ref.shape, NEG_INF, f32)
    acc_ref[...] = jnp.zeros(acc_ref.shape, f32)

    def scores(j, s, buf):
        rows = pl.ds(pl.multiple_of(j * tk, tk), tk)
        buf[s] = _dot(ka_ref[0, s, rows, :], qat_ref[0, 0, s * LANES:(s + 1) * LANES, :])

    def consume(j, s, buf, masked):
        sc = buf[s]
        if masked:
            sc = jnp.where(causal, sc, NEG_INF)
        m = m_ref[s]
        m_new = jnp.maximum(m, jnp.max(sc, axis=0, keepdims=True))
        alpha = jnp.exp2(m - m_new)
        pr = jnp.exp2(sc - m_new)
        vt = vt_ref[0, j, s * V_ROWS:(s + 1) * V_ROWS, :]
        acc_ref[s] = alpha * acc_ref[s] + _dot(vt, pr.astype(bf16))
        m_ref[s] = m_new

    heads = range(ATT_HEADS)

    def step(j, cur, nxt):
        for s in heads:
            scores(j + 1, s, nxt)
            consume(j, s, cur, False)

    for s in heads:
        scores(0, s, sa_ref)

    def two_steps(t, carry):
        step(2 * t, sa_ref, sb_ref)
        step(2 * t + 1, sb_ref, sa_ref)
        return carry

    lax.fori_loop(0, i // 2, two_steps, 0)

    @pl.when(i % 2 == 0)
    def _():
        for s in heads:
            consume(i, s, sa_ref, True)

    @pl.when(i % 2 == 1)
    def _():
        step(i - 1, sa_ref, sb_ref)
        for s in heads:
            consume(i, s, sb_ref, True)
    o_t = jnp.concatenate([acc_ref[s, :HEAD_DIM, :] / acc_ref[s, HEAD_DIM:HEAD_DIM + 1, :]
                           for s in range(ATT_HEADS)], axis=0)
    o_ref[0] = o_t.T.astype(bf16)


def _attn_prompt(qat, ka, vt):
    b, nq, _, tq = qat.shape
    s = nq * tq
    g = ATT_HEADS
    return pl.pallas_call(
        _attn_kernel, grid=(b, N_HEADS // g, nq),
        in_specs=[pl.BlockSpec((1, 1, g * LANES, tq), lambda i, h, j: (i, j, h, 0)),
                  pl.BlockSpec((1, g, s, LANES), lambda i, h, j: (i, h, 0, 0)),
                  pl.BlockSpec((1, nq, g * V_ROWS, tq), lambda i, h, j: (i, 0, h, 0))],
        out_specs=pl.BlockSpec((1, tq, g * HEAD_DIM), lambda i, h, j: (i, j, h)),
        out_shape=jax.ShapeDtypeStruct((b, s, D_MODEL), bf16),
        scratch_shapes=[pltpu.VMEM((g, 1, tq), f32), pltpu.VMEM((g, V_ROWS, tq), f32),
                        pltpu.VMEM((g, tq, tq), f32), pltpu.VMEM((g, tq, tq), f32)],
        name="attn_prompt",
        compiler_params=_params(3))(qat, ka, vt)


def _attn_sample_kernel(q_ref, kn_ref, vn_ref, kp_ref, vp_ref, cq_ref, ck_ref, o_ref):
    t = q_ref.shape[1]
    past = kp_ref.shape[2]
    low = lax.broadcasted_iota(jnp.int32, (1, LANES), 1) < HEAD_DIM
    causal = (lax.broadcasted_iota(jnp.int32, (t, t), 1) <= lax.broadcasted_iota(jnp.int32, (t, t), 0))
    for hp in range(N_HEADS // 2):
        lanes = slice(hp * LANES, (hp + 1) * LANES)
        q2 = q_ref[0, :, lanes]
        kp2 = kp_ref[0, lanes, :].astype(bf16)
        vp2 = vp_ref[0, lanes, :].astype(bf16)
        kn2 = kn_ref[0, :, lanes]
        vn2 = vn_ref[0, :, lanes]
        halves = []
        for sub in range(2):
            h = 2 * hp + sub
            qm = jnp.where(low if sub == 0 else jnp.logical_not(low), q2, jnp.zeros_like(q2))
            cq = cq_ref[0, :, h:h + 1]
            s_past = _dot(qm, kp2) + cq - ck_ref[0, h:h + 1, :past]
            s_new = _dot_nt(qm, kn2) + cq - ck_ref[0, h:h + 1, past:past + t]
            s_new = jnp.where(causal, s_new, NEG_INF)
            m = jnp.maximum(jnp.max(s_past, axis=1, keepdims=True), jnp.max(s_new, axis=1, keepdims=True))
            p_past = jnp.exp(s_past - m)
            p_new = jnp.exp(s_new - m)
            l = jnp.sum(p_past, axis=1, keepdims=True) + jnp.sum(p_new, axis=1, keepdims=True)
            acc = _dot_nt(p_past.astype(bf16), vp2) + _dot(p_new.astype(bf16), vn2)
            halves.append(acc / l)
        o_ref[0, :, lanes] = jnp.where(low, halves[0], halves[1]).astype(bf16)


def _attn_sample(qb, kb, vb, past_kt, past_vt, c_nat, c_t):
    b, t, _ = qb.shape
    past = past_kt.shape[2]
    new = pl.BlockSpec((1, t, D_MODEL), lambda i: (i, 0, 0))
    old = pl.BlockSpec((1, D_MODEL, past), lambda i: (i, 0, 0))
    return pl.pallas_call(
        _attn_sample_kernel, grid=(b,),
        in_specs=[new, new, new, old, old,
                  pl.BlockSpec((1, t, N_HEADS), lambda i: (i, 0, 0)),
                  pl.BlockSpec((1, N_HEADS, c_t.shape[2]), lambda i: (i, 0, 0))],
        out_specs=new,
        out_shape=jax.ShapeDtypeStruct((b, t, D_MODEL), bf16),
        name="attn_sample", compiler_params=_params(1))(qb, kb, vb, past_kt, past_vt, c_nat, c_t)


def kernel(x_prompt, x_sample, state_conv, cache_k, cache_v, cache_logf, norm_ffn, ffn_w_in, ffn_w_out, norm_mix, conv_w_in, conv_w, conv_w_out, attn_w_in, attn_b_f, q_norm, k_norm, attn_w_out):
    bp, sp, d = x_prompt.shape
    bs, ss, _ = x_sample.shape
    past = cache_k.shape[2]

    n_p, n_s = bp * sp, bs * ss
    def ffn(xs, i, j, w, **kw):
        return _ffn(xs, norm_ffn[i, j].reshape(1, d), *w, **kw)

    w_slabs, wo_slabs = 32, 16
    ffn_casts = lambda i, j: [(ffn_w_in, (i, j), w_slabs), (ffn_w_out, (i, j), wo_slabs)]
    square_cast = lambda w: (w, (0,), w_slabs)

    y, conv_in_b, conv_out_b = ffn((x_prompt.reshape(n_p, d), x_sample.reshape(n_s, d)), 0, 0,
                                   (ffn_w_in[0, 0].astype(bf16), ffn_w_out[0, 0].astype(bf16)),
                                   casts=[square_cast(conv_w_in), square_cast(conv_w_out)])
    cw = (norm_mix[0].reshape(1, d), conv_in_b, conv_w[0], conv_out_b)
    y_mixed, conv_p, *w01 = _conv_mixer(y, jnp.zeros((bp, CONV_WIDTH - 1, d), f32), *cw, t=sp, row0=0,
                                        casts=ffn_casts(0, 1))
    y, conv_s = _conv_mixer(y, state_conv[0], *cw, t=ss, row0=n_p, into=y_mixed)
    y, *w10 = ffn(y, 0, 1, w01, casts=ffn_casts(1, 0))

    y, *w11, attn_out_b = ffn(y, 1, 0, w10, casts=ffn_casts(1, 1) + [square_cast(attn_w_out)])

    w_in = attn_w_in[0]
    f_pad = jnp.zeros((d, F_PAD - N_HEADS), f32)
    wft = w_in[:, 3 * d:].T.astype(bf16)
    head_of = jnp.arange(d) // HEAD_DIM
    seg = ((head_of[:, None] == jnp.arange(LANES)[None, :]).astype(f32) * (1.0 / HEAD_DIM)).astype(bf16)
    expand = (jnp.arange(LANES)[:, None] == head_of[None, :]).astype(bf16)
    expand2 = jnp.concatenate([expand, expand], axis=0)
    g_mix = norm_mix[1].reshape(1, d)
    bcol = attn_b_f[0].reshape(N_HEADS, 1)
    gk = jnp.tile(k_norm[0], N_HEADS).reshape(1, d)

    tril = jnp.tril(jnp.ones((ROW_TILE, ROW_TILE), f32)).astype(bf16)
    kpt, vpt, fpt, ka, qat, vt = _attn_proj_prompt(
        y, g_mix, jnp.concatenate([w_in[:, d:2 * d], w_in[:, 3 * d:], f_pad], axis=1).astype(bf16),
        jnp.concatenate([w_in[:, :d], w_in[:, 2 * d:3 * d]], axis=1).T.astype(bf16), wft, bcol,
        jnp.concatenate([attn_b_f[0], jnp.zeros((F_PAD - N_HEADS,), f32)]).reshape(1, F_PAD),
        seg, expand2, jnp.broadcast_to(q_norm[0][:, None], (HEAD_DIM, ROW_TILE)), gk, tril, tril.T, b=bp, t=sp)
    ap = _attn_prompt(qat, ka, vt)

    kq, vq, fq, fqt, qsb, ksb, vsb = _attn_proj(
        y, g_mix, jnp.concatenate([w_in, f_pad], axis=1).astype(bf16), wft, bcol,
        attn_b_f[0].reshape(1, N_HEADS), seg, expand2, jnp.tile(q_norm[0], N_HEADS).reshape(1, d), gk,
        rows=n_s, row0=n_p)
    pad = (-(past + ss)) % SCAN_TILE
    lf_all = jnp.concatenate([jnp.swapaxes(cache_logf[0], 1, 2),
                              jnp.swapaxes(fqt.reshape(N_HEADS, bs, ss), 0, 1),
                              jnp.zeros((bs, N_HEADS, pad), f32)], axis=2)
    cst = _cumsum_lanes(lf_all, jnp.triu(jnp.ones((SCAN_TILE, SCAN_TILE), f32)).astype(bf16))
    per_stream = lambda a: a.reshape(bs, ss, d)
    channel_major = lambda a: jnp.transpose(a, (0, 2, 3, 1)).reshape(bs, d, past)
    a_s = _attn_sample(per_stream(qsb), per_stream(ksb), per_stream(vsb),
                       channel_major(cache_k[0]), channel_major(cache_v[0]),
                       jnp.swapaxes(cst[:, :, past:past + ss], 1, 2), cst)

    yp, ys = ffn(y, 1, 1, w11, mix=(ap.reshape(n_p, d), a_s.reshape(n_s, d), attn_out_b))

    hd = (N_HEADS, HEAD_DIM)
    token_major = lambda a: jnp.transpose(a.reshape(bp, *hd, sp), (0, 3, 1, 2))[None]
    return (yp.reshape(bp, sp, d), ys.reshape(bs, ss, d),
            conv_p[None], conv_s[None],
            token_major(kpt), token_major(vpt), jnp.swapaxes(fpt, 1, 2)[None],
            kq.reshape(1, bs, ss, *hd), vq.reshape(1, bs, ss, *hd), fq.reshape(1, bs, ss, N_HEADS))
```

```python
import functools

import jax
import jax.numpy as jnp
from jax import lax
from jax.experimental import pallas as pl
from jax.experimental.pallas import tpu as pltpu

D_MODEL = 1024
N_HEADS = 16
HEAD_DIM = 64
D_FF = 2816
CONV_WIDTH = 3
NORM_EPS = 1e-6
NEG_INF = -1e30
FFN_RESIDUAL = 0.5

LANES = 128
SUBLANES = 8
F_PAD = LANES
VMEM_LIMIT = 56 * 1024 * 1024
FF_CHUNKS = ((0, 1536), (1536, 2816))
ROW_TILE = 512
ATT_TILE = 256
ATT_HEADS = 8
SCAN_TILE = 256
N_AUG = 3
V_ROWS = HEAD_DIM + 16
LOG2_E = 1.4426950408889634

f32 = jnp.float32
bf16 = jnp.bfloat16


def _dot(a, b):
    return jnp.dot(a, b, preferred_element_type=f32)


def _dot_nt(a, b):
    return lax.dot_general(a, b, (((1,), (1,)), ((), ())), preferred_element_type=f32)


def _rms(x, g):
    return x * lax.rsqrt(jnp.mean(x * x, axis=-1, keepdims=True) + NORM_EPS) * g


def _log_sigmoid(z):
    return -(jnp.maximum(-z, 0.0) + jnp.log1p(jnp.exp(-jnp.abs(z))))


def _split3(x):
    hi = x.astype(bf16)
    r1 = x - hi.astype(f32)
    mid = r1.astype(bf16)
    lo = (r1 - mid.astype(f32)).astype(bf16)
    return hi, mid, lo


def _params(n_axes, flags=None):
    return pltpu.CompilerParams(dimension_semantics=("arbitrary",) * n_axes,
                                vmem_limit_bytes=VMEM_LIMIT, flags=flags)


def _const_spec(shape):
    return pl.BlockSpec(shape, lambda *_: (0,) * len(shape), pipeline_mode=pl.Buffered(1))


def _cast_job(src, lead, steps, lin):
    rows, cols = src.shape[-2:]
    slab = rows // steps
    assert slab * steps == rows and slab % 16 == 0
    pos = lambda *idx: jnp.minimum(lin(*idx), steps - 1)
    in_spec = pl.BlockSpec((None,) * len(lead) + (slab, cols), lambda *idx: (*lead, pos(*idx), 0))
    out_spec = pl.BlockSpec((slab, cols), lambda *idx: (pos(*idx), 0))
    return src, in_spec, out_spec, jax.ShapeDtypeStruct((rows, cols), bf16), steps


def _run_casts(srcs, dsts):
    for src_ref, dst_ref in zip(srcs, dsts):
        dst_ref[...] = src_ref[...].astype(bf16)


def _ffn_kernel(*refs, n_prompt, split_in, mix, n_jobs):
    refs = list(refs)
    cast_srcs = cast_dsts = ()
    if n_jobs:
        cast_dsts = refs[-n_jobs:]
        del refs[-n_jobs:]
        n_out = 2 if mix else 1
        cast_srcs = refs[-n_out - n_jobs:-n_out]
        del refs[-n_out - n_jobs:-n_out]
    is_prompt = pl.program_id(0) < n_prompt

    def rows():
        if split_in or mix:
            p_ref, s_ref = refs.pop(0), refs.pop(0)
            return jnp.where(is_prompt, p_ref[...], s_ref[...])
        return refs.pop(0)[...]

    if mix:
        x = refs.pop(0)[...]
        a = rows()
        x = x + _dot(a, refs.pop(0)[...])
    else:
        x = rows()
    g_ref, win_ref, wout_ref = refs[:3]
    outs = refs[3:]
    xn = _rms(x, g_ref[...]).astype(bf16)
    acc = None
    for lo, hi in FF_CHUNKS:
        a = _dot(xn, win_ref[:, lo:hi])
        b = _dot(xn, win_ref[:, D_FF + lo:D_FF + hi])
        if lo == 0:
            _run_casts(cast_srcs, cast_dsts)
        h = (a * jax.nn.sigmoid(a) * b).astype(bf16)
        y = _dot(h, wout_ref[lo:hi, :])
        acc = y if acc is None else acc + y
    y = x + FFN_RESIDUAL * acc
    if mix:
        @pl.when(is_prompt)
        def _():
            outs[0][...] = y

        @pl.when(jnp.logical_not(is_prompt))
        def _():
            outs[1][...] = y
    else:
        outs[0][...] = y


def _ffn(xs, g, w_in, w_out, mix=None, casts=()):
    tm = ROW_TILE
    split_in = isinstance(xs, tuple)
    if mix is None and split_in:
        n_prompt, n_sample = xs[0].shape[0] // tm, xs[1].shape[0] // tm
    elif mix is not None:
        n_prompt, n_sample = mix[0].shape[0] // tm, mix[1].shape[0] // tm
    else:
        n_prompt, n_sample = xs.shape[0] // tm, 0
    assert n_sample in (0, 1)
    steps = n_prompt + n_sample
    row = pl.BlockSpec((tm, D_MODEL), lambda i: (i, 0))
    prompt_row = pl.BlockSpec((tm, D_MODEL), lambda i: (jnp.minimum(i, n_prompt - 1), 0))
    sample_row = pl.BlockSpec((tm, D_MODEL), lambda i: (0, 0))
    w_specs = [_const_spec((1, D_MODEL)), _const_spec((D_MODEL, 2 * D_FF)), _const_spec((D_FF, D_MODEL))]
    stacked = jax.ShapeDtypeStruct((steps * tm, D_MODEL), f32)
    if mix is not None:
        a_p, a_s, wm = mix
        ins = (xs, a_p, a_s, wm)
        specs = [row, prompt_row, sample_row, _const_spec((D_MODEL, D_MODEL))]
        out_specs = [prompt_row, sample_row]
        out_shape = [jax.ShapeDtypeStruct((n_prompt * tm, D_MODEL), f32),
                     jax.ShapeDtypeStruct((n_sample * tm, D_MODEL), f32)]
    elif split_in:
        ins, specs, out_specs, out_shape = tuple(xs), [prompt_row, sample_row], [row], [stacked]
    else:
        ins, specs, out_specs, out_shape = (xs,), [row], [row], [stacked]
    jobs = [_cast_job(src, lead, n, lambda i: i) for src, lead, n in casts]
    kern = functools.partial(_ffn_kernel, n_prompt=n_prompt, split_in=split_in, mix=mix is not None,
                             n_jobs=len(jobs))
    return pl.pallas_call(
        kern, grid=(steps,), in_specs=specs + w_specs + [j[1] for j in jobs],
        out_specs=out_specs + [j[2] for j in jobs], out_shape=out_shape + [j[3] for j in jobs],
        name="ffn" if mix is None else "mix_ffn",
        compiler_params=_params(1))(*ins, g, w_in, w_out, *[j[0] for j in jobs])


def _conv_kernel(x_ref, hist_ref, g_ref, win_ref, wk_ref, wout_ref, *rest, aliased, n_jobs):
    rest = rest[1:] if aliased else rest
    o_ref, st_ref = rest[n_jobs:n_jobs + 2]
    carry_ref = rest[-1]
    t = pl.program_id(1)
    tt = x_ref.shape[0]

    @pl.when(t == 0)
    def _():
        carry_ref[0:2, :] = hist_ref[0]

    x = x_ref[...]
    xn = _rms(x, g_ref[...]).astype(bf16)
    p = _dot(xn, win_ref[...])
    _run_casts(rest[:n_jobs], rest[n_jobs + 2:-1])
    gate_b = p[:, :D_MODEL]
    u = p[:, D_MODEL:2 * D_MODEL] * p[:, 2 * D_MODEL:]
    prev2 = carry_ref[0:1, :]
    prev1 = carry_ref[1:2, :]
    row = lax.broadcasted_iota(jnp.int32, (tt, 1), 0)
    u1 = jnp.where(row == 0, prev1, pltpu.roll(u, 1, 0))
    u2 = jnp.where(row == 0, prev2, jnp.where(row == 1, prev1, pltpu.roll(u, 2, 0)))
    wk = wk_ref[...]
    conv = wk[0:1, :] * u2 + wk[1:2, :] * u1 + wk[2:3, :] * u
    y = _dot((gate_b * conv).astype(bf16), wout_ref[...])
    o_ref[...] = x + y
    last = u[tt - 2:tt, :]
    carry_ref[0:2, :] = last
    st_ref[0] = last


def _conv_mixer(x, hist, g, win, wk, wout, *, t, row0, into=None, casts=()):
    b = hist.shape[0]
    tt = min(ROW_TILE, t)
    nt, base = t // tt, row0 // tt
    blk = pl.BlockSpec((tt, D_MODEL), lambda i, j: (base + i * nt + j, 0))
    st = pl.BlockSpec((1, CONV_WIDTH - 1, D_MODEL), lambda i, j: (i, 0, 0))
    ins = [x, hist, g, win, wk, wout]
    specs = [blk, st, _const_spec((1, D_MODEL)), _const_spec((D_MODEL, 3 * D_MODEL)),
             _const_spec((CONV_WIDTH, D_MODEL)), _const_spec((D_MODEL, D_MODEL))]
    aliases = {}
    if into is not None:
        aliases = {len(ins): 0}
        ins.append(into)
        specs.append(pl.BlockSpec(memory_space=pl.ANY))
    jobs = [_cast_job(src, lead, n, lambda i, j: i * nt + j) for src, lead, n in casts]
    kern = functools.partial(_conv_kernel, aliased=into is not None, n_jobs=len(jobs))
    return pl.pallas_call(
        kern, grid=(b, nt), in_specs=specs + [j[1] for j in jobs], out_specs=[blk, st] + [j[2] for j in jobs],
        out_shape=[jax.ShapeDtypeStruct(x.shape, f32),
                   jax.ShapeDtypeStruct((b, CONV_WIDTH - 1, D_MODEL), f32)] + [j[3] for j in jobs],
        scratch_shapes=[pltpu.VMEM((8, D_MODEL), f32)], input_output_aliases=aliases,
        name="conv_mixer", compiler_params=_params(2))(*ins, *[j[0] for j in jobs])


def _head_norm(t, gain, seg_ref, exp_ref):
    ms = _dot((t * t).astype(bf16), seg_ref[...])
    r = lax.rsqrt(ms + NORM_EPS)
    r_hi = r.astype(bf16)
    r_lo = (r - r_hi.astype(f32)).astype(bf16)
    rb = _dot(jnp.concatenate([r_hi, r_lo], axis=1), exp_ref[...])
    return t * rb * gain


def _proj_kernel(x_ref, g_ref, w_ref, wft_ref, bcol_ref, brow_ref, seg_ref, exp_ref, gq_ref, gk_ref,
                 k_ref, v_ref, lf_ref, lft_ref, qb_ref, kb_ref, vb_ref):
    xn = _rms(x_ref[...], g_ref[...]).astype(bf16)
    p = _dot(xn, w_ref[...])
    q = _head_norm(p[:, :D_MODEL], gq_ref[...], seg_ref, exp_ref)
    k = _head_norm(p[:, D_MODEL:2 * D_MODEL], gk_ref[...], seg_ref, exp_ref)
    v = p[:, 2 * D_MODEL:3 * D_MODEL]
    k_ref[...] = k
    v_ref[...] = v
    lf_ref[...] = _log_sigmoid(p[:, 3 * D_MODEL:3 * D_MODEL + N_HEADS] + brow_ref[...])
    lft_ref[...] = _log_sigmoid(_dot_nt(wft_ref[...], xn) + bcol_ref[...])
    qb_ref[...] = (q * (HEAD_DIM ** -0.5)).astype(bf16)
    kb_ref[...] = k.astype(bf16)
    vb_ref[...] = v.astype(bf16)


def _attn_proj(x, g, w, wft, bcol, brow, seg, exp, gq, gk, *, rows, row0):
    blk = lambda dt: jax.ShapeDtypeStruct((rows, D_MODEL), dt)
    full = lambda shape: pl.BlockSpec(shape, lambda i: (0, 0))
    return pl.pallas_call(
        _proj_kernel, grid=(1,),
        in_specs=[pl.BlockSpec((rows, D_MODEL), lambda i: (row0 // rows, 0)),
                  _const_spec((1, D_MODEL)), _const_spec((D_MODEL, 3 * D_MODEL + F_PAD)),
                  _const_spec((N_HEADS, D_MODEL)), _const_spec((N_HEADS, 1)), _const_spec((1, N_HEADS)),
                  _const_spec((D_MODEL, LANES)), _const_spec((2 * LANES, D_MODEL)),
                  _const_spec((1, D_MODEL)), _const_spec((1, D_MODEL))],
        out_specs=[full((rows, D_MODEL)), full((rows, D_MODEL)), full((rows, N_HEADS)), full((N_HEADS, rows)),
                   full((rows, D_MODEL)), full((rows, D_MODEL)), full((rows, D_MODEL))],
        out_shape=[blk(f32), blk(f32), jax.ShapeDtypeStruct((rows, N_HEADS), f32),
                   jax.ShapeDtypeStruct((N_HEADS, rows), f32), blk(bf16), blk(bf16), blk(bf16)],
        name="attn_proj", compiler_params=_params(1))(x, g, w, wft, bcol, brow, seg, exp, gq, gk)


def _proj_prompt_kernel(x_ref, g_ref, w_ref, wqvt_ref, wft_ref, bcol_ref, brow_ref, seg_ref, exp_ref,
                        gqc_ref, gk_ref, tril_ref, triu_ref, place_ref,
                        kt_ref, vt_ref, lft_ref, ka_ref, qat_ref, vtb_ref, crow_ref, ccol_ref):
    t = pl.program_id(1)
    tt = x_ref.shape[0]
    n_tiles = tt // ATT_TILE

    @pl.when(t == 0)
    def _():
        crow_ref[...] = jnp.zeros(crow_ref.shape, f32)
        ccol_ref[...] = jnp.zeros(ccol_ref.shape, f32)

    xn = _rms(x_ref[...], g_ref[...]).astype(bf16)
    p = _dot(xn, w_ref[...])
    ft = _dot_nt(wft_ref[...], xn)
    qt = _dot_nt(wqvt_ref[:D_MODEL, :], xn)
    k = _head_norm(p[:, :D_MODEL], gk_ref[...], seg_ref, exp_ref)
    kt_ref[0] = k.T
    lf = _log_sigmoid(p[:, D_MODEL:] + brow_ref[...])

    c3 = _dot(tril_ref[...], jnp.concatenate(_split3(lf), axis=1))
    c = c3[:, :LANES] + c3[:, LANES:2 * LANES] + c3[:, 2 * LANES:] + crow_ref[0:1, :]
    crow_ref[0:1, :] = c[tt - 1:tt, :]
    lane = lax.broadcasted_iota(jnp.int32, (1, LANES), 1)
    hi, mid, lo = [piece.astype(f32) for piece in _split3(c * LOG2_E)]
    packed = jnp.where(lane < N_HEADS, hi, jnp.where(lane < 2 * N_HEADS, pltpu.roll(mid, N_HEADS, 1),
                       jnp.where(lane < 3 * N_HEADS, pltpu.roll(lo, 2 * N_HEADS, 1),
                                 jnp.where(lane == 3 * N_HEADS, -1.0, 0.0))))

    lft = _log_sigmoid(ft + bcol_ref[...])
    lft_ref[0] = lft
    ct3 = _dot(jnp.concatenate(_split3(lft), axis=0), triu_ref[...])
    ct = ct3[:N_HEADS] + ct3[N_HEADS:2 * N_HEADS] + ct3[2 * N_HEADS:] + ccol_ref[:, 0:1]
    ccol_ref[...] = jnp.broadcast_to(ct[:, tt - 1:tt], ccol_ref.shape)
    ct_pieces = [piece.astype(f32) for piece in _split3(ct * LOG2_E)]

    vt = _dot_nt(wqvt_ref[D_MODEL:, :], xn)
    ext = _dot(packed.astype(bf16), place_ref[...])
    vt_ref[0] = vt
    vtb = vt.astype(bf16)
    ones_rows = jnp.where(lax.broadcasted_iota(jnp.int32, (V_ROWS - HEAD_DIM, ATT_TILE), 0) == 0, 1.0, 0.0).astype(bf16)
    for j in range(n_tiles):
        for h in range(N_HEADS):
            vtb_ref[0, j, h * V_ROWS:h * V_ROWS + HEAD_DIM, :] = (
                vtb[h * HEAD_DIM:(h + 1) * HEAD_DIM, j * ATT_TILE:(j + 1) * ATT_TILE])
            vtb_ref[0, j, h * V_ROWS + HEAD_DIM:(h + 1) * V_ROWS, :] = ones_rows

    sub = lax.broadcasted_iota(jnp.int32, (SUBLANES, 1), 0)
    gqc = gqc_ref[...]
    for hp in range(N_HEADS // 2):
        k2 = k[:, hp * LANES:(hp + 1) * LANES]
        for s in range(2):
            h = 2 * hp + s
            base = k2 if s == 0 else pltpu.roll(k2, HEAD_DIM, 1)
            ka_ref[0, h] = jnp.where(lane < HEAD_DIM, base, ext[:, h * LANES:(h + 1) * LANES]).astype(bf16)

            qh = qt[h * HEAD_DIM:(h + 1) * HEAD_DIM, :]
            qn = qh * lax.rsqrt(jnp.mean(qh * qh, axis=0, keepdims=True) + NORM_EPS) * gqc
            aug = jnp.where(sub < 2 * N_AUG, 1.0, 0.0) * jnp.ones((1, tt), f32)
            for n in range(N_AUG):
                aug = jnp.where(sub == n, ct_pieces[n][h:h + 1, :], aug)
            blk = jnp.concatenate([qn * (LOG2_E * HEAD_DIM ** -0.5), aug,
                                   jnp.zeros((LANES - HEAD_DIM - SUBLANES, tt), f32)], axis=0).astype(bf16)
            for j in range(n_tiles):
                qat_ref[0, j, h * LANES:(h + 1) * LANES, :] = blk[:, j * ATT_TILE:(j + 1) * ATT_TILE]


def _attn_proj_prompt(x, g, w, wqvt, wft, bcol, brow, seg, exp, gqc, gk, tril, triu, place, *, b, t):
    tt = ROW_TILE
    nt = tt // ATT_TILE
    time_minor = pl.BlockSpec((1, D_MODEL, tt), lambda i, j: (i, 0, j))
    big = jax.ShapeDtypeStruct((b, D_MODEL, t), f32)
    return pl.pallas_call(
        _proj_prompt_kernel, grid=(b, t // tt),
        in_specs=[pl.BlockSpec((tt, D_MODEL), lambda i, j: (i * (t // tt) + j, 0)),
                  _const_spec((1, D_MODEL)), _const_spec((D_MODEL, D_MODEL + F_PAD)),
                  _const_spec((2 * D_MODEL, D_MODEL)), _const_spec((N_HEADS, D_MODEL)),
                  _const_spec((N_HEADS, 1)), _const_spec((1, F_PAD)),
                  _const_spec((D_MODEL, LANES)), _const_spec((2 * LANES, D_MODEL)),
                  _const_spec((HEAD_DIM, tt)), _const_spec((1, D_MODEL)),
                  _const_spec((tt, tt)), _const_spec((tt, tt)), _const_spec((LANES, N_HEADS * LANES))],
        out_specs=[time_minor, time_minor,
                   pl.BlockSpec((1, N_HEADS, tt), lambda i, j: (i, 0, j)),
                   pl.BlockSpec((1, N_HEADS, tt, LANES), lambda i, j: (i, 0, j, 0)),
                   pl.BlockSpec((1, nt, N_HEADS * LANES, ATT_TILE), lambda i, j: (i, j, 0, 0)),
                   pl.BlockSpec((1, nt, N_HEADS * V_ROWS, ATT_TILE), lambda i, j: (i, j, 0, 0))],
        out_shape=[big, big, jax.ShapeDtypeStruct((b, N_HEADS, t), f32),
                   jax.ShapeDtypeStruct((b, N_HEADS, t, LANES), bf16),
                   jax.ShapeDtypeStruct((b, t // ATT_TILE, N_HEADS * LANES, ATT_TILE), bf16),
                   jax.ShapeDtypeStruct((b, t // ATT_TILE, N_HEADS * V_ROWS, ATT_TILE), bf16)],
        scratch_shapes=[pltpu.VMEM((SUBLANES, LANES), f32), pltpu.VMEM((N_HEADS, LANES), f32)],
        name="attn_proj_prompt", compiler_params=_params(2))(
            x, g, w, wqvt, wft, bcol, brow, seg, exp, gqc, gk, tril, triu, place)


def _scan_kernel(x_ref, tri_ref, o_ref):
    n = x_ref.shape[2] // SCAN_TILE
    carry = jnp.zeros((N_HEADS, 1), f32)
    for c in range(n):
        sl = slice(c * SCAN_TILE, (c + 1) * SCAN_TILE)
        cs = _dot(jnp.concatenate(_split3(x_ref[0, :, sl]), axis=0), tri_ref[...])
        out = cs[:N_HEADS] + cs[N_HEADS:2 * N_HEADS] + cs[2 * N_HEADS:] + carry
        o_ref[0, :, sl] = out
        carry = out[:, SCAN_TILE - 1:SCAN_TILE]


def _cumsum_lanes(lft, tri):
    b, h, t = lft.shape
    blk = pl.BlockSpec((1, h, t), lambda i: (i, 0, 0))
    return pl.pallas_call(
        _scan_kernel, grid=(b,), in_specs=[blk, _const_spec((SCAN_TILE, SCAN_TILE))], out_specs=blk,
        out_shape=jax.ShapeDtypeStruct((b, h, t), f32), name="cumsum_lanes",
        compiler_params=_params(1))(lft, tri)


def _attn_kernel(qat_ref, qnext_ref, ka_ref, vt_ref, o_ref, m_ref, acc_ref, sa_ref, sb_ref):
    i = pl.program_id(2)
    tq = qat_ref.shape[3]
    tk = vt_ref.shape[3]
    causal = (lax.broadcasted_iota(jnp.int32, (tk, tq), 0) <= lax.broadcasted_iota(jnp.int32, (tk, tq), 1))
    m_ref[...] = jnp.full(m_ref.shape, NEG_INF, f32)
    acc_ref[...] = jnp.zeros(acc_ref.shape, f32)

    def scores(q_ref, j, s, buf):
        rows = pl.ds(pl.multiple_of(j * tk, tk), tk)
        buf[s] = _dot(ka_ref[0, s, rows, :], q_ref[0, 0, s * LANES:(s + 1) * LANES, :])

    def consume(j, s, buf, masked):
        sc = buf[s]
        if masked:
            sc = jnp.where(causal, sc, NEG_INF)
        m = m_ref[s]
        m_new = jnp.maximum(m, jnp.max(sc, axis=0, keepdims=True))
        alpha = jnp.exp2(m - m_new)
        pr = jnp.exp2(sc - m_new)
        vt = vt_ref[0, j, s * V_ROWS:(s + 1) * V_ROWS, :]
        acc_ref[s] = alpha * acc_ref[s] + _dot(vt, pr.astype(bf16))
        m_ref[s] = m_new

    heads = range(ATT_HEADS)

    def step(j, cur, nxt):
        for s in heads:
            scores(qat_ref, j + 1, s, nxt)
            consume(j, s, cur, False)

    def last_step(cur, nxt):
        for s in heads:
            scores(qnext_ref, 0, s, nxt)
            consume(i, s, cur, True)

    @pl.when(i == 0)
    def _():
        for s in heads:
            scores(qat_ref, 0, s, sa_ref)

    def sweep(a, b):
        def two_steps(t, carry):
            step(2 * t, a, b)
            step(2 * t + 1, b, a)
            return carry

        lax.fori_loop(0, i // 2, two_steps, 0)

        @pl.when(i % 2 == 0)
        def _():
            last_step(a, b)

        @pl.when(i % 2 == 1)
        def _():
            step(i - 1, a, b)
            last_step(b, a)

    first_in_b = ((i * (i + 1)) // 2) % 2

    @pl.when(first_in_b == 0)
    def _():
        sweep(sa_ref, sb_ref)

    @pl.when(first_in_b == 1)
    def _():
        sweep(sb_ref, sa_ref)

    o_t = jnp.concatenate([acc_ref[s, :HEAD_DIM, :] / acc_ref[s, HEAD_DIM:HEAD_DIM + 1, :]
                           for s in range(ATT_HEADS)], axis=0)
    o_ref[0] = o_t.T.astype(bf16)


def _attn_prompt(qat, ka, vt):
    b, nq, _, tq = qat.shape
    s = nq * tq
    g = ATT_HEADS
    return pl.pallas_call(
        _attn_kernel, grid=(b, N_HEADS // g, nq),
        in_specs=[pl.BlockSpec((1, 1, g * LANES, tq), lambda i, h, j: (i, j, h, 0)),
                  pl.BlockSpec((1, 1, g * LANES, tq), lambda i, h, j: (i, jnp.minimum(j + 1, nq - 1), h, 0)),
                  pl.BlockSpec((1, g, s, LANES), lambda i, h, j: (i, h, 0, 0)),
                  pl.BlockSpec((1, nq, g * V_ROWS, tq), lambda i, h, j: (i, 0, h, 0))],
        out_specs=pl.BlockSpec((1, tq, g * HEAD_DIM), lambda i, h, j: (i, j, h)),
        out_shape=jax.ShapeDtypeStruct((b, s, D_MODEL), bf16),
        scratch_shapes=[pltpu.VMEM((g, 1, tq), f32), pltpu.VMEM((g, V_ROWS, tq), f32),
                        pltpu.VMEM((g, tq, tq), f32), pltpu.VMEM((g, tq, tq), f32)],
        name="attn_prompt",
        compiler_params=_params(3))(qat, qat, ka, vt)


def _attn_sample_kernel(q_ref, kn_ref, vn_ref, kp_ref, vp_ref, cq_ref, ck_ref, o_ref):
    t = q_ref.shape[1]
    past = kp_ref.shape[2]
    low = lax.broadcasted_iota(jnp.int32, (1, LANES), 1) < HEAD_DIM
    causal = (lax.broadcasted_iota(jnp.int32, (t, t), 1) <= lax.broadcasted_iota(jnp.int32, (t, t), 0))
    for hp in range(N_HEADS // 2):
        lanes = slice(hp * LANES, (hp + 1) * LANES)
        q2 = q_ref[0, :, lanes]
        kp2 = kp_ref[0, lanes, :].astype(bf16)
        vp2 = vp_ref[0, lanes, :].astype(bf16)
        kn2 = kn_ref[0, :, lanes]
        vn2 = vn_ref[0, :, lanes]
        halves = []
        for sub in range(2):
            h = 2 * hp + sub
            qm = jnp.where(low if sub == 0 else jnp.logical_not(low), q2, jnp.zeros_like(q2))
            cq = cq_ref[0, :, h:h + 1]
            s_past = _dot(qm, kp2) + cq - ck_ref[0, h:h + 1, :past]
            s_new = _dot_nt(qm, kn2) + cq - ck_ref[0, h:h + 1, past:past + t]
            s_new = jnp.where(causal, s_new, NEG_INF)
            m = jnp.maximum(jnp.max(s_past, axis=1, keepdims=True), jnp.max(s_new, axis=1, keepdims=True))
            p_past = jnp.exp(s_past - m)
            p_new = jnp.exp(s_new - m)
            l = jnp.sum(p_past, axis=1, keepdims=True) + jnp.sum(p_new, axis=1, keepdims=True)
            acc = _dot_nt(p_past.astype(bf16), vp2) + _dot(p_new.astype(bf16), vn2)
            halves.append(acc / l)
        o_ref[0, :, lanes] = jnp.where(low, halves[0], halves[1]).astype(bf16)


def _attn_sample(qb, kb, vb, past_kt, past_vt, c_nat, c_t):
    b, t, _ = qb.shape
    past = past_kt.shape[2]
    new = pl.BlockSpec((1, t, D_MODEL), lambda i: (i, 0, 0))
    old = pl.BlockSpec((1, D_MODEL, past), lambda i: (i, 0, 0))
    return pl.pallas_call(
        _attn_sample_kernel, grid=(b,),
        in_specs=[new, new, new, old, old,
                  pl.BlockSpec((1, t, N_HEADS), lambda i: (i, 0, 0)),
                  pl.BlockSpec((1, N_HEADS, c_t.shape[2]), lambda i: (i, 0, 0))],
        out_specs=new,
        out_shape=jax.ShapeDtypeStruct((b, t, D_MODEL), bf16),
        name="attn_sample", compiler_params=_params(1))(qb, kb, vb, past_kt, past_vt, c_nat, c_t)


def kernel(x_prompt, x_sample, state_conv, cache_k, cache_v, cache_logf, norm_ffn, ffn_w_in, ffn_w_out, norm_mix, conv_w_in, conv_w, conv_w_out, attn_w_in, attn_b_f, q_norm, k_norm, attn_w_out):
    bp, sp, d = x_prompt.shape
    bs, ss, _ = x_sample.shape
    past = cache_k.shape[2]

    n_p, n_s = bp * sp, bs * ss
    def ffn(xs, i, j, w, **kw):
        return _ffn(xs, norm_ffn[i, j].reshape(1, d), *w, **kw)

    w_slabs, wo_slabs = 32, 16
    ffn_casts = lambda i, j: [(ffn_w_in, (i, j), w_slabs), (ffn_w_out, (i, j), wo_slabs)]
    square_cast = lambda w: (w, (0,), w_slabs)

    y, conv_in_b, conv_out_b = ffn((x_prompt.reshape(n_p, d), x_sample.reshape(n_s, d)), 0, 0,
                                   (ffn_w_in[0, 0].astype(bf16), ffn_w_out[0, 0].astype(bf16)),
                                   casts=[square_cast(conv_w_in), square_cast(conv_w_out)])
    cw = (norm_mix[0].reshape(1, d), conv_in_b, conv_w[0], conv_out_b)
    y_mixed, conv_p, *w01 = _conv_mixer(y, jnp.zeros((bp, CONV_WIDTH - 1, d), f32), *cw, t=sp, row0=0,
                                        casts=ffn_casts(0, 1))
    y, conv_s = _conv_mixer(y, state_conv[0], *cw, t=ss, row0=n_p, into=y_mixed)
    y, *w10 = ffn(y, 0, 1, w01, casts=ffn_casts(1, 0))

    y, *w11, attn_out_b = ffn(y, 1, 0, w10, casts=ffn_casts(1, 1) + [square_cast(attn_w_out)])

    w_in = attn_w_in[0]
    f_pad = jnp.zeros((d, F_PAD - N_HEADS), f32)
    wft = w_in[:, 3 * d:].T.astype(bf16)
    head_of = jnp.arange(d) // HEAD_DIM
    seg = ((head_of[:, None] == jnp.arange(LANES)[None, :]).astype(f32) * (1.0 / HEAD_DIM)).astype(bf16)
    expand = (jnp.arange(LANES)[:, None] == head_of[None, :]).astype(bf16)
    expand2 = jnp.concatenate([expand, expand], axis=0)
    g_mix = norm_mix[1].reshape(1, d)
    bcol = attn_b_f[0].reshape(N_HEADS, 1)
    gk = jnp.tile(k_norm[0], N_HEADS).reshape(1, d)

    tril = jnp.tril(jnp.ones((ROW_TILE, ROW_TILE), f32)).astype(bf16)
    p_row, p_col = jnp.arange(LANES)[:, None], jnp.arange(N_HEADS * LANES)[None, :]
    p_lane = p_col % LANES
    place = jnp.where(((p_row < N_AUG * N_HEADS) & (p_col // LANES == p_row % N_HEADS)
                       & (p_lane == HEAD_DIM + N_AUG + p_row // N_HEADS))
                      | ((p_row == N_AUG * N_HEADS) & (p_lane >= HEAD_DIM) & (p_lane < HEAD_DIM + N_AUG)),
                      -1.0, 0.0).astype(bf16)
    kpt, vpt, fpt, ka, qat, vt = _attn_proj_prompt(
        y, g_mix, jnp.concatenate([w_in[:, d:2 * d], w_in[:, 3 * d:], f_pad], axis=1).astype(bf16),
        jnp.concatenate([w_in[:, :d], w_in[:, 2 * d:3 * d]], axis=1).T.astype(bf16), wft, bcol,
        jnp.concatenate([attn_b_f[0], jnp.zeros((F_PAD - N_HEADS,), f32)]).reshape(1, F_PAD),
        seg, expand2, jnp.broadcast_to(q_norm[0][:, None], (HEAD_DIM, ROW_TILE)), gk, tril, tril.T, place,
        b=bp, t=sp)
    ap = _attn_prompt(qat, ka, vt)

    kq, vq, fq, fqt, qsb, ksb, vsb = _attn_proj(
        y, g_mix, jnp.concatenate([w_in, f_pad], axis=1).astype(bf16), wft, bcol,
        attn_b_f[0].reshape(1, N_HEADS), seg, expand2, jnp.tile(q_norm[0], N_HEADS).reshape(1, d), gk,
        rows=n_s, row0=n_p)
    pad = (-(past + ss)) % SCAN_TILE
    lf_all = jnp.concatenate([jnp.swapaxes(cache_logf[0], 1, 2),
                              jnp.swapaxes(fqt.reshape(N_HEADS, bs, ss), 0, 1),
                              jnp.zeros((bs, N_HEADS, pad), f32)], axis=2)
    cst = _cumsum_lanes(lf_all, jnp.triu(jnp.ones((SCAN_TILE, SCAN_TILE), f32)).astype(bf16))
    per_stream = lambda a: a.reshape(bs, ss, d)
    channel_major = lambda a: jnp.transpose(a, (0, 2, 3, 1)).reshape(bs, d, past)
    a_s = _attn_sample(per_stream(qsb), per_stream(ksb), per_stream(vsb),
                       channel_major(cache_k[0]), channel_major(cache_v[0]),
                       jnp.swapaxes(cst[:, :, past:past + ss], 1, 2), cst)

    yp, ys = ffn(y, 1, 1, w11, mix=(ap.reshape(n_p, d), a_s.reshape(n_s, d), attn_out_b))

    hd = (N_HEADS, HEAD_DIM)
    token_major = lambda a: jnp.transpose(a.reshape(bp, *hd, sp), (0, 3, 1, 2))[None]
    return (yp.reshape(bp, sp, d), ys.reshape(bs, ss, d),
            conv_p[None], conv_s[None],
            token_major(kpt), token_major(vpt), jnp.swapaxes(fpt, 1, 2)[None],
            kq.reshape(1, bs, ss, *hd), vq.reshape(1, bs, ss, *hd), fq.reshape(1, bs, ss, N_HEADS))
```

```python
import functools

import jax
import jax.numpy as jnp
from jax import lax
from jax.experimental import pallas as pl
from jax.experimental.pallas import tpu as pltpu

D_MODEL = 1024
N_HEADS = 16
HEAD_DIM = 64
D_FF = 2816
CONV_WIDTH = 3
NORM_EPS = 1e-6
NEG_INF = -1e30
FFN_RESIDUAL = 0.5

LANES = 128
SUBLANES = 8
F_PAD = LANES
VMEM_LIMIT = 56 * 1024 * 1024
FF_CHUNKS = ((0, 1536), (1536, 2816))
ROW_TILE = 512
ATT_TILE = 256
ATT_HEADS = 16
SCAN_TILE = 256
N_AUG = 3
V_ROWS = HEAD_DIM + 16
LOG2_E = 1.4426950408889634

f32 = jnp.float32
bf16 = jnp.bfloat16


def _dot(a, b):
    return jnp.dot(a, b, preferred_element_type=f32)


def _dot_nt(a, b):
    return lax.dot_general(a, b, (((1,), (1,)), ((), ())), preferred_element_type=f32)


def _rms(x, g):
    return x * lax.rsqrt(jnp.mean(x * x, axis=-1, keepdims=True) + NORM_EPS) * g


def _log_sigmoid(z):
    return -(jnp.maximum(-z, 0.0) + jnp.log1p(jnp.exp(-jnp.abs(z))))


def _split3(x):
    hi = x.astype(bf16)
    r1 = x - hi.astype(f32)
    mid = r1.astype(bf16)
    lo = (r1 - mid.astype(f32)).astype(bf16)
    return hi, mid, lo


def _params(n_axes, flags=None):
    return pltpu.CompilerParams(dimension_semantics=("arbitrary",) * n_axes,
                                vmem_limit_bytes=VMEM_LIMIT, flags=flags)


def _const_spec(shape):
    return pl.BlockSpec(shape, lambda *_: (0,) * len(shape), pipeline_mode=pl.Buffered(1))


def _cast_job(src, lead, steps, lin):
    rows, cols = src.shape[-2:]
    slab = rows // steps
    assert slab * steps == rows and slab % 16 == 0
    pos = lambda *idx: jnp.minimum(lin(*idx), steps - 1)
    in_spec = pl.BlockSpec((None,) * len(lead) + (slab, cols), lambda *idx: (*lead, pos(*idx), 0))
    out_spec = pl.BlockSpec((slab, cols), lambda *idx: (pos(*idx), 0))
    return src, in_spec, out_spec, jax.ShapeDtypeStruct((rows, cols), bf16), steps


def _run_casts(srcs, dsts):
    for src_ref, dst_ref in zip(srcs, dsts):
        dst_ref[...] = src_ref[...].astype(bf16)


def _ffn_kernel(*refs, n_prompt, split_in, mix, n_jobs):
    refs = list(refs)
    cast_srcs = cast_dsts = ()
    if n_jobs:
        cast_dsts = refs[-n_jobs:]
        del refs[-n_jobs:]
        n_out = 2 if mix else 1
        cast_srcs = refs[-n_out - n_jobs:-n_out]
        del refs[-n_out - n_jobs:-n_out]
    is_prompt = pl.program_id(0) < n_prompt

    def rows():
        if split_in or mix:
            p_ref, s_ref = refs.pop(0), refs.pop(0)
            return jnp.where(is_prompt, p_ref[...], s_ref[...])
        return refs.pop(0)[...]

    if mix:
        x = refs.pop(0)[...]
        a = rows()
        x = x + _dot(a, refs.pop(0)[...])
    else:
        x = rows()
    g_ref, win_ref, wout_ref = refs[:3]
    outs = refs[3:]
    xn = _rms(x, g_ref[...]).astype(bf16)
    acc = None
    for lo, hi in FF_CHUNKS:
        a = _dot(xn, win_ref[:, lo:hi])
        b = _dot(xn, win_ref[:, D_FF + lo:D_FF + hi])
        if lo == 0:
            _run_casts(cast_srcs, cast_dsts)
        h = (a * jax.nn.sigmoid(a) * b).astype(bf16)
        y = _dot(h, wout_ref[lo:hi, :])
        acc = y if acc is None else acc + y
    y = x + FFN_RESIDUAL * acc
    if mix:
        @pl.when(is_prompt)
        def _():
            outs[0][...] = y

        @pl.when(jnp.logical_not(is_prompt))
        def _():
            outs[1][...] = y
    else:
        outs[0][...] = y


def _ffn(xs, g, w_in, w_out, mix=None, casts=()):
    tm = ROW_TILE
    split_in = isinstance(xs, tuple)
    if mix is None and split_in:
        n_prompt, n_sample = xs[0].shape[0] // tm, xs[1].shape[0] // tm
    elif mix is not None:
        n_prompt, n_sample = mix[0].shape[0] // tm, mix[1].shape[0] // tm
    else:
        n_prompt, n_sample = xs.shape[0] // tm, 0
    assert n_sample in (0, 1)
    steps = n_prompt + n_sample
    row = pl.BlockSpec((tm, D_MODEL), lambda i: (i, 0))
    prompt_row = pl.BlockSpec((tm, D_MODEL), lambda i: (jnp.minimum(i, n_prompt - 1), 0))
    sample_row = pl.BlockSpec((tm, D_MODEL), lambda i: (0, 0))
    w_specs = [_const_spec((1, D_MODEL)), _const_spec((D_MODEL, 2 * D_FF)), _const_spec((D_FF, D_MODEL))]
    stacked = jax.ShapeDtypeStruct((steps * tm, D_MODEL), f32)
    if mix is not None:
        a_p, a_s, wm = mix
        ins = (xs, a_p, a_s, wm)
        specs = [row, prompt_row, sample_row, _const_spec((D_MODEL, D_MODEL))]
        out_specs = [prompt_row, sample_row]
        out_shape = [jax.ShapeDtypeStruct((n_prompt * tm, D_MODEL), f32),
                     jax.ShapeDtypeStruct((n_sample * tm, D_MODEL), f32)]
    elif split_in:
        ins, specs, out_specs, out_shape = tuple(xs), [prompt_row, sample_row], [row], [stacked]
    else:
        ins, specs, out_specs, out_shape = (xs,), [row], [row], [stacked]
    jobs = [_cast_job(src, lead, n, lambda i: i) for src, lead, n in casts]
    kern = functools.partial(_ffn_kernel, n_prompt=n_prompt, split_in=split_in, mix=mix is not None,
                             n_jobs=len(jobs))
    return pl.pallas_call(
        kern, grid=(steps,), in_specs=specs + w_specs + [j[1] for j in jobs],
        out_specs=out_specs + [j[2] for j in jobs], out_shape=out_shape + [j[3] for j in jobs],
        name="ffn" if mix is None else "mix_ffn",
        compiler_params=_params(1))(*ins, g, w_in, w_out, *[j[0] for j in jobs])


def _conv_body(x, g_ref, win_ref, wk_ref, wout_ref, shifted, between=lambda: None):
    xn = _rms(x, g_ref[...]).astype(bf16)
    p = _dot(xn, win_ref[...])
    between()
    gate_b = p[:, :D_MODEL]
    u = p[:, D_MODEL:2 * D_MODEL] * p[:, 2 * D_MODEL:]
    u1, u2 = shifted(u)
    wk = wk_ref[...]
    conv = wk[0:1, :] * u2 + wk[1:2, :] * u1 + wk[2:3, :] * u
    return x + _dot((gate_b * conv).astype(bf16), wout_ref[...]), u


def _conv_kernel(x_ref, g_ref, win_ref, wk_ref, wout_ref, *rest, n_jobs):
    o_ref, st_ref = rest[n_jobs:n_jobs + 2]
    carry_ref = rest[-1]
    tt = x_ref.shape[0]

    @pl.when(pl.program_id(1) == 0)
    def _():
        carry_ref[...] = jnp.zeros(carry_ref.shape, f32)

    def shifted(u):
        prev2 = carry_ref[0:1, :]
        prev1 = carry_ref[1:2, :]
        row = lax.broadcasted_iota(jnp.int32, (tt, 1), 0)
        return (jnp.where(row == 0, prev1, pltpu.roll(u, 1, 0)),
                jnp.where(row == 0, prev2, jnp.where(row == 1, prev1, pltpu.roll(u, 2, 0))))

    o_ref[...], u = _conv_body(x_ref[...], g_ref, win_ref, wk_ref, wout_ref, shifted,
                               lambda: _run_casts(rest[:n_jobs], rest[n_jobs + 2:-1]))
    last = u[tt - 2:tt, :]
    carry_ref[0:2, :] = last
    st_ref[0] = last


def _conv_mixer(x, n_streams, g, win, wk, wout, *, t, casts=()):
    tt = ROW_TILE
    nt = t // tt
    blk = pl.BlockSpec((tt, D_MODEL), lambda i, j: (i * nt + j, 0))
    st = pl.BlockSpec((1, CONV_WIDTH - 1, D_MODEL), lambda i, j: (i, 0, 0))
    specs = [blk, _const_spec((1, D_MODEL)), _const_spec((D_MODEL, 3 * D_MODEL)),
             _const_spec((CONV_WIDTH, D_MODEL)), _const_spec((D_MODEL, D_MODEL))]
    jobs = [_cast_job(src, lead, n, lambda i, j: i * nt + j) for src, lead, n in casts]
    return pl.pallas_call(
        functools.partial(_conv_kernel, n_jobs=len(jobs)), grid=(n_streams, nt),
        in_specs=specs + [j[1] for j in jobs], out_specs=[blk, st] + [j[2] for j in jobs],
        out_shape=[jax.ShapeDtypeStruct(x.shape, f32),
                   jax.ShapeDtypeStruct((n_streams, CONV_WIDTH - 1, D_MODEL), f32)] + [j[3] for j in jobs],
        scratch_shapes=[pltpu.VMEM((8, D_MODEL), f32)],
        name="conv_mixer", compiler_params=_params(2))(x, g, win, wk, wout, *[j[0] for j in jobs])


def _conv_sample_kernel(x_ref, h1_ref, h2_ref, g_ref, win_ref, wk_ref, wout_ref, into_ref, o_ref, u_ref, *, t):
    rows = x_ref.shape[0]

    def shifted(u):
        pos = lax.rem(lax.broadcasted_iota(jnp.int32, (rows, 1), 0), t)
        h1 = h1_ref[...]
        return (jnp.where(pos == 0, h1, pltpu.roll(u, 1, 0)),
                jnp.where(pos == 0, h2_ref[...], jnp.where(pos == 1, h1, pltpu.roll(u, 2, 0))))

    o_ref[...], u_ref[...] = _conv_body(x_ref[...], g_ref, win_ref, wk_ref, wout_ref, shifted)


def _conv_mixer_sample(x, state, g, win, wk, wout, *, t, row0, into):
    n = state.shape[0]
    rows = n * t
    blk = pl.BlockSpec((rows, D_MODEL), lambda i: (row0 // rows, 0))
    hist = pl.BlockSpec((rows, D_MODEL), lambda i: (0, 0))
    y, u = pl.pallas_call(
        functools.partial(_conv_sample_kernel, t=t), grid=(1,),
        in_specs=[blk, hist, hist, _const_spec((1, D_MODEL)), _const_spec((D_MODEL, 3 * D_MODEL)),
                  _const_spec((CONV_WIDTH, D_MODEL)), _const_spec((D_MODEL, D_MODEL)),
                  pl.BlockSpec(memory_space=pl.ANY)],
        out_specs=[blk, hist],
        out_shape=[jax.ShapeDtypeStruct(x.shape, f32), jax.ShapeDtypeStruct((rows, D_MODEL), f32)],
        input_output_aliases={7: 0},
        name="conv_mixer_sample", compiler_params=_params(1))(
            x, jnp.repeat(state[:, 1], t, axis=0), jnp.repeat(state[:, 0], t, axis=0), g, win, wk, wout, into)
    return y, u.reshape(n, t, D_MODEL)[:, t - (CONV_WIDTH - 1):]


def _head_norm(t, gain, seg_ref, exp_ref):
    ms = _dot((t * t).astype(bf16), seg_ref[...])
    r = lax.rsqrt(ms + NORM_EPS)
    r_hi = r.astype(bf16)
    r_lo = (r - r_hi.astype(f32)).astype(bf16)
    rb = _dot(jnp.concatenate([r_hi, r_lo], axis=1), exp_ref[...])
    return t * rb * gain


def _proj_kernel(x_ref, g_ref, w_ref, wft_ref, bcol_ref, brow_ref, seg_ref, exp_ref, gq_ref, gk_ref,
                 k_ref, v_ref, lf_ref, lft_ref, qb_ref, kb_ref, vb_ref):
    xn = _rms(x_ref[...], g_ref[...]).astype(bf16)
    p = _dot(xn, w_ref[...])
    q = _head_norm(p[:, :D_MODEL], gq_ref[...], seg_ref, exp_ref)
    k = _head_norm(p[:, D_MODEL:2 * D_MODEL], gk_ref[...], seg_ref, exp_ref)
    v = p[:, 2 * D_MODEL:3 * D_MODEL]
    k_ref[...] = k
    v_ref[...] = v
    lf_ref[...] = _log_sigmoid(p[:, 3 * D_MODEL:3 * D_MODEL + N_HEADS] + brow_ref[...])
    lft_ref[...] = _log_sigmoid(_dot_nt(wft_ref[...], xn) + bcol_ref[...])
    qb_ref[...] = (q * (HEAD_DIM ** -0.5)).astype(bf16)
    kb_ref[...] = k.astype(bf16)
    vb_ref[...] = v.astype(bf16)


def _attn_proj(x, g, w, wft, bcol, brow, seg, exp, gq, gk, *, rows, row0):
    blk = lambda dt: jax.ShapeDtypeStruct((rows, D_MODEL), dt)
    full = lambda shape: pl.BlockSpec(shape, lambda i: (0, 0))
    return pl.pallas_call(
        _proj_kernel, grid=(1,),
        in_specs=[pl.BlockSpec((rows, D_MODEL), lambda i: (row0 // rows, 0)),
                  _const_spec((1, D_MODEL)), _const_spec((D_MODEL, 3 * D_MODEL + F_PAD)),
                  _const_spec((N_HEADS, D_MODEL)), _const_spec((N_HEADS, 1)), _const_spec((1, N_HEADS)),
                  _const_spec((D_MODEL, LANES)), _const_spec((2 * LANES, D_MODEL)),
                  _const_spec((1, D_MODEL)), _const_spec((1, D_MODEL))],
        out_specs=[full((rows, D_MODEL)), full((rows, D_MODEL)), full((rows, N_HEADS)), full((N_HEADS, rows)),
                   full((rows, D_MODEL)), full((rows, D_MODEL)), full((rows, D_MODEL))],
        out_shape=[blk(f32), blk(f32), jax.ShapeDtypeStruct((rows, N_HEADS), f32),
                   jax.ShapeDtypeStruct((N_HEADS, rows), f32), blk(bf16), blk(bf16), blk(bf16)],
        name="attn_proj", compiler_params=_params(1))(x, g, w, wft, bcol, brow, seg, exp, gq, gk)


def _proj_prompt_kernel(x_ref, g_ref, w_ref, wqvt_ref, wft_ref, bcol_ref, brow_ref, seg_ref, exp_ref,
                        gqc_ref, gk_ref, tril_ref, triu_ref, place_ref,
                        kt_ref, vt_ref, lft_ref, ka_ref, qat_ref, vtb_ref, crow_ref, ccol_ref):
    t = pl.program_id(1)
    tt = x_ref.shape[0]
    n_tiles = tt // ATT_TILE

    @pl.when(t == 0)
    def _():
        crow_ref[...] = jnp.zeros(crow_ref.shape, f32)
        ccol_ref[...] = jnp.zeros(ccol_ref.shape, f32)

    xn = _rms(x_ref[...], g_ref[...]).astype(bf16)
    p = _dot(xn, w_ref[...])
    ft = _dot_nt(wft_ref[...], xn)
    qt = _dot_nt(wqvt_ref[:D_MODEL, :], xn)
    k = _head_norm(p[:, :D_MODEL], gk_ref[...], seg_ref, exp_ref)
    kt_ref[0] = k.T
    lf = _log_sigmoid(p[:, D_MODEL:] + brow_ref[...])

    c3 = _dot(tril_ref[...], jnp.concatenate(_split3(lf), axis=1))
    c = c3[:, :LANES] + c3[:, LANES:2 * LANES] + c3[:, 2 * LANES:] + crow_ref[0:1, :]
    crow_ref[0:1, :] = c[tt - 1:tt, :]
    lane = lax.broadcasted_iota(jnp.int32, (1, LANES), 1)
    hi, mid, lo = [piece.astype(f32) for piece in _split3(c * LOG2_E)]
    packed = jnp.where(lane < N_HEADS, hi, jnp.where(lane < 2 * N_HEADS, pltpu.roll(mid, N_HEADS, 1),
                       jnp.where(lane < 3 * N_HEADS, pltpu.roll(lo, 2 * N_HEADS, 1),
                                 jnp.where(lane == 3 * N_HEADS, -1.0, 0.0))))

    lft = _log_sigmoid(ft + bcol_ref[...])
    lft_ref[0] = lft
    ct3 = _dot(jnp.concatenate(_split3(lft), axis=0), triu_ref[...])
    ct = ct3[:N_HEADS] + ct3[N_HEADS:2 * N_HEADS] + ct3[2 * N_HEADS:] + ccol_ref[:, 0:1]
    ccol_ref[...] = jnp.broadcast_to(ct[:, tt - 1:tt], ccol_ref.shape)
    ct_pieces = [piece.astype(f32) for piece in _split3(ct * LOG2_E)]

    vt = _dot_nt(wqvt_ref[D_MODEL:, :], xn)
    ext = _dot(packed.astype(bf16), place_ref[...])
    vt_ref[0] = vt
    vtb = vt.astype(bf16)
    ones_rows = jnp.where(lax.broadcasted_iota(jnp.int32, (V_ROWS - HEAD_DIM, ATT_TILE), 0) == 0, 1.0, 0.0).astype(bf16)
    for j in range(n_tiles):
        for h in range(N_HEADS):
            vtb_ref[0, j, h * V_ROWS:h * V_ROWS + HEAD_DIM, :] = (
                vtb[h * HEAD_DIM:(h + 1) * HEAD_DIM, j * ATT_TILE:(j + 1) * ATT_TILE])
            vtb_ref[0, j, h * V_ROWS + HEAD_DIM:(h + 1) * V_ROWS, :] = ones_rows

    sub = lax.broadcasted_iota(jnp.int32, (SUBLANES, 1), 0)
    gqc = gqc_ref[...]
    for hp in range(N_HEADS // 2):
        k2 = k[:, hp * LANES:(hp + 1) * LANES]
        for s in range(2):
            h = 2 * hp + s
            base = k2 if s == 0 else pltpu.roll(k2, HEAD_DIM, 1)
            ka_ref[0, h] = jnp.where(lane < HEAD_DIM, base, ext[:, h * LANES:(h + 1) * LANES]).astype(bf16)

            qh = qt[h * HEAD_DIM:(h + 1) * HEAD_DIM, :]
            qn = qh * lax.rsqrt(jnp.mean(qh * qh, axis=0, keepdims=True) + NORM_EPS) * gqc
            aug = jnp.where(sub < 2 * N_AUG, 1.0, 0.0) * jnp.ones((1, tt), f32)
            for n in range(N_AUG):
                aug = jnp.where(sub == n, ct_pieces[n][h:h + 1, :], aug)
            blk = jnp.concatenate([qn * (LOG2_E * HEAD_DIM ** -0.5), aug,
                                   jnp.zeros((LANES - HEAD_DIM - SUBLANES, tt), f32)], axis=0).astype(bf16)
            for j in range(n_tiles):
                qat_ref[0, j, h * LANES:(h + 1) * LANES, :] = blk[:, j * ATT_TILE:(j + 1) * ATT_TILE]


def _attn_proj_prompt(x, g, w, wqvt, wft, bcol, brow, seg, exp, gqc, gk, tril, triu, place, *, b, t):
    tt = ROW_TILE
    nt = tt // ATT_TILE
    time_minor = pl.BlockSpec((1, D_MODEL, tt), lambda i, j: (i, 0, j))
    big = jax.ShapeDtypeStruct((b, D_MODEL, t), f32)
    return pl.pallas_call(
        _proj_prompt_kernel, grid=(b, t // tt),
        in_specs=[pl.BlockSpec((tt, D_MODEL), lambda i, j: (i * (t // tt) + j, 0)),
                  _const_spec((1, D_MODEL)), _const_spec((D_MODEL, D_MODEL + F_PAD)),
                  _const_spec((2 * D_MODEL, D_MODEL)), _const_spec((N_HEADS, D_MODEL)),
                  _const_spec((N_HEADS, 1)), _const_spec((1, F_PAD)),
                  _const_spec((D_MODEL, LANES)), _const_spec((2 * LANES, D_MODEL)),
                  _const_spec((HEAD_DIM, tt)), _const_spec((1, D_MODEL)),
                  _const_spec((tt, tt)), _const_spec((tt, tt)), _const_spec((LANES, N_HEADS * LANES))],
        out_specs=[time_minor, time_minor,
                   pl.BlockSpec((1, N_HEADS, tt), lambda i, j: (i, 0, j)),
                   pl.BlockSpec((1, N_HEADS, tt, LANES), lambda i, j: (i, 0, j, 0)),
                   pl.BlockSpec((1, nt, N_HEADS * LANES, ATT_TILE), lambda i, j: (i, j, 0, 0)),
                   pl.BlockSpec((1, nt, N_HEADS * V_ROWS, ATT_TILE), lambda i, j: (i, j, 0, 0))],
        out_shape=[big, big, jax.ShapeDtypeStruct((b, N_HEADS, t), f32),
                   jax.ShapeDtypeStruct((b, N_HEADS, t, LANES), bf16),
                   jax.ShapeDtypeStruct((b, t // ATT_TILE, N_HEADS * LANES, ATT_TILE), bf16),
                   jax.ShapeDtypeStruct((b, t // ATT_TILE, N_HEADS * V_ROWS, ATT_TILE), bf16)],
        scratch_shapes=[pltpu.VMEM((SUBLANES, LANES), f32), pltpu.VMEM((N_HEADS, LANES), f32)],
        name="attn_proj_prompt", compiler_params=_params(2))(
            x, g, w, wqvt, wft, bcol, brow, seg, exp, gqc, gk, tril, triu, place)


def _scan_kernel(x_ref, tri_ref, o_ref):
    rows = x_ref.shape[0]
    n = x_ref.shape[1] // SCAN_TILE
    carry = jnp.zeros((rows, 1), f32)
    for c in range(n):
        sl = slice(c * SCAN_TILE, (c + 1) * SCAN_TILE)
        cs = _dot(jnp.concatenate(_split3(x_ref[:, sl]), axis=0), tri_ref[...])
        out = cs[:rows] + cs[rows:2 * rows] + cs[2 * rows:] + carry
        o_ref[:, sl] = out
        carry = out[:, SCAN_TILE - 1:SCAN_TILE]


def _cumsum_lanes(lft, tri):
    b, h, t = lft.shape
    blk = pl.BlockSpec((b * h, t), lambda i: (0, 0))
    return pl.pallas_call(
        _scan_kernel, grid=(1,), in_specs=[blk, _const_spec((SCAN_TILE, SCAN_TILE))], out_specs=blk,
        out_shape=jax.ShapeDtypeStruct((b * h, t), f32), name="cumsum_lanes",
        compiler_params=_params(1))(lft.reshape(b * h, t), tri).reshape(b, h, t)


def _attn_kernel(qat_ref, qnext_ref, ka_ref, vt_ref, o_ref, m_ref, acc_ref, sa_ref, sb_ref):
    i = pl.program_id(2)
    tq = qat_ref.shape[3]
    tk = vt_ref.shape[3]
    causal = (lax.broadcasted_iota(jnp.int32, (tk, tq), 0) <= lax.broadcasted_iota(jnp.int32, (tk, tq), 1))
    m_ref[...] = jnp.full(m_ref.shape, NEG_INF, f32)
    acc_ref[...] = jnp.zeros(acc_ref.shape, f32)

    def scores(q_ref, j, s, buf):
        rows = pl.ds(pl.multiple_of(j * tk, tk), tk)
        buf[s] = _dot(ka_ref[0, s, rows, :], q_ref[0, 0, s * LANES:(s + 1) * LANES, :])

    def consume(j, s, buf, masked):
        sc = buf[s]
        if masked:
            sc = jnp.where(causal, sc, NEG_INF)
        m = m_ref[s]
        m_new = jnp.maximum(m, jnp.max(sc, axis=0, keepdims=True))
        alpha = jnp.exp2(m - m_new)
        pr = jnp.exp2(sc - m_new)
        vt = vt_ref[0, j, s * V_ROWS:(s + 1) * V_ROWS, :]
        acc_ref[s] = alpha * acc_ref[s] + _dot(vt, pr.astype(bf16))
        m_ref[s] = m_new

    heads = range(ATT_HEADS)

    def step(j, cur, nxt):
        for s in heads:
            scores(qat_ref, j + 1, s, nxt)
            consume(j, s, cur, False)

    def last_step(cur, nxt):
        for s in heads:
            scores(qnext_ref, 0, s, nxt)
            consume(i, s, cur, True)

    @pl.when(i == 0)
    def _():
        for s in heads:
            scores(qat_ref, 0, s, sa_ref)

    def sweep(a, b):
        def two_steps(t, carry):
            step(2 * t, a, b)
            step(2 * t + 1, b, a)
            return carry

        lax.fori_loop(0, i // 2, two_steps, 0)

        @pl.when(i % 2 == 0)
        def _():
            last_step(a, b)

        @pl.when(i % 2 == 1)
        def _():
            step(i - 1, a, b)
            last_step(b, a)

    first_in_b = ((i * (i + 1)) // 2) % 2

    @pl.when(first_in_b == 0)
    def _():
        sweep(sa_ref, sb_ref)

    @pl.when(first_in_b == 1)
    def _():
        sweep(sb_ref, sa_ref)

    o_t = jnp.concatenate([acc_ref[s, :HEAD_DIM, :] / acc_ref[s, HEAD_DIM:HEAD_DIM + 1, :]
                           for s in range(ATT_HEADS)], axis=0)
    o_ref[0] = o_t.T.astype(bf16)


def _attn_prompt(qat, ka, vt):
    b, nq, _, tq = qat.shape
    s = nq * tq
    g = ATT_HEADS
    return pl.pallas_call(
        _attn_kernel, grid=(b, N_HEADS // g, nq),
        in_specs=[pl.BlockSpec((1, 1, g * LANES, tq), lambda i, h, j: (i, j, h, 0)),
                  pl.BlockSpec((1, 1, g * LANES, tq), lambda i, h, j: (i, jnp.minimum(j + 1, nq - 1), h, 0)),
                  pl.BlockSpec((1, g, s, LANES), lambda i, h, j: (i, h, 0, 0)),
                  pl.BlockSpec((1, nq, g * V_ROWS, tq), lambda i, h, j: (i, 0, h, 0))],
        out_specs=pl.BlockSpec((1, tq, g * HEAD_DIM), lambda i, h, j: (i, j, h)),
        out_shape=jax.ShapeDtypeStruct((b, s, D_MODEL), bf16),
        scratch_shapes=[pltpu.VMEM((g, 1, tq), f32), pltpu.VMEM((g, V_ROWS, tq), f32),
                        pltpu.VMEM((g, tq, tq), f32), pltpu.VMEM((g, tq, tq), f32)],
        name="attn_prompt",
        compiler_params=_params(3))(qat, qat, ka, vt)


def _attn_sample_kernel(q_ref, kn_ref, vn_ref, kp_ref, vp_ref, cq_ref, ck_ref, o_ref):
    t = q_ref.shape[1]
    past = kp_ref.shape[2]
    low = lax.broadcasted_iota(jnp.int32, (1, LANES), 1) < HEAD_DIM
    causal = (lax.broadcasted_iota(jnp.int32, (t, t), 1) <= lax.broadcasted_iota(jnp.int32, (t, t), 0))
    n_pairs = N_HEADS // 2

    def pair_scores(hp):
        lanes = slice(hp * LANES, (hp + 1) * LANES)
        q2 = q_ref[0, :, lanes]
        kp2 = kp_ref[0, lanes, :].astype(bf16)
        kn2 = kn_ref[0, :, lanes]
        out = []
        for sub in range(2):
            qm = jnp.where(low if sub == 0 else jnp.logical_not(low), q2, jnp.zeros_like(q2))
            out.append((_dot(qm, kp2), _dot_nt(qm, kn2)))
        return out

    ahead = pair_scores(0)
    for hp in range(n_pairs):
        raw = ahead
        if hp + 1 < n_pairs:
            ahead = pair_scores(hp + 1)
        lanes = slice(hp * LANES, (hp + 1) * LANES)
        vp2 = vp_ref[0, lanes, :].astype(bf16)
        vn2 = vn_ref[0, :, lanes]
        halves = []
        for sub in range(2):
            h = 2 * hp + sub
            cq = cq_ref[0, :, h:h + 1]
            s_past = raw[sub][0] + cq - ck_ref[0, h:h + 1, :past]
            s_new = raw[sub][1] + cq - ck_ref[0, h:h + 1, past:past + t]
            s_new = jnp.where(causal, s_new, NEG_INF)
            m = jnp.maximum(jnp.max(s_past, axis=1, keepdims=True), jnp.max(s_new, axis=1, keepdims=True))
            p_past = jnp.exp(s_past - m)
            p_new = jnp.exp(s_new - m)
            l = jnp.sum(p_past, axis=1, keepdims=True) + jnp.sum(p_new, axis=1, keepdims=True)
            acc = _dot_nt(p_past.astype(bf16), vp2) + _dot(p_new.astype(bf16), vn2)
            halves.append(acc / l)
        o_ref[0, :, lanes] = jnp.where(low, halves[0], halves[1]).astype(bf16)


def _attn_sample(qb, kb, vb, past_kt, past_vt, c_nat, c_t):
    b, t, _ = qb.shape
    past = past_kt.shape[2]
    new = pl.BlockSpec((1, t, D_MODEL), lambda i: (i, 0, 0))
    old = pl.BlockSpec((1, D_MODEL, past), lambda i: (i, 0, 0))
    return pl.pallas_call(
        _attn_sample_kernel, grid=(b,),
        in_specs=[new, new, new, old, old,
                  pl.BlockSpec((1, t, N_HEADS), lambda i: (i, 0, 0)),
                  pl.BlockSpec((1, N_HEADS, c_t.shape[2]), lambda i: (i, 0, 0))],
        out_specs=new,
        out_shape=jax.ShapeDtypeStruct((b, t, D_MODEL), bf16),
        name="attn_sample", compiler_params=_params(1))(qb, kb, vb, past_kt, past_vt, c_nat, c_t)


def kernel(x_prompt, x_sample, state_conv, cache_k, cache_v, cache_logf, norm_ffn, ffn_w_in, ffn_w_out, norm_mix, conv_w_in, conv_w, conv_w_out, attn_w_in, attn_b_f, q_norm, k_norm, attn_w_out):
    bp, sp, d = x_prompt.shape
    bs, ss, _ = x_sample.shape
    past = cache_k.shape[2]

    n_p, n_s = bp * sp, bs * ss
    def ffn(xs, i, j, w, **kw):
        return _ffn(xs, norm_ffn[i, j].reshape(1, d), *w, **kw)

    w_slabs, wo_slabs = 32, 16
    ffn_casts = lambda i, j: [(ffn_w_in, (i, j), w_slabs), (ffn_w_out, (i, j), wo_slabs)]
    square_cast = lambda w: (w, (0,), w_slabs)

    y, conv_in_b, conv_out_b = ffn((x_prompt.reshape(n_p, d), x_sample.reshape(n_s, d)), 0, 0,
                                   (ffn_w_in[0, 0].astype(bf16), ffn_w_out[0, 0].astype(bf16)),
                                   casts=[square_cast(conv_w_in), square_cast(conv_w_out)])
    cw = (norm_mix[0].reshape(1, d), conv_in_b, conv_w[0], conv_out_b)
    y_mixed, conv_p, *w01 = _conv_mixer(y, bp, *cw, t=sp, casts=ffn_casts(0, 1))
    y, conv_s = _conv_mixer_sample(y, state_conv[0], *cw, t=ss, row0=n_p, into=y_mixed)
    y, *w10 = ffn(y, 0, 1, w01, casts=ffn_casts(1, 0))

    y, *w11, attn_out_b = ffn(y, 1, 0, w10, casts=ffn_casts(1, 1) + [square_cast(attn_w_out)])

    w_in = attn_w_in[0]
    f_pad = jnp.zeros((d, F_PAD - N_HEADS), f32)
    wft = w_in[:, 3 * d:].T.astype(bf16)
    head_of = jnp.arange(d) // HEAD_DIM
    seg = ((head_of[:, None] == jnp.arange(LANES)[None, :]).astype(f32) * (1.0 / HEAD_DIM)).astype(bf16)
    expand = (jnp.arange(LANES)[:, None] == head_of[None, :]).astype(bf16)
    expand2 = jnp.concatenate([expand, expand], axis=0)
    g_mix = norm_mix[1].reshape(1, d)
    bcol = attn_b_f[0].reshape(N_HEADS, 1)
    gk = jnp.tile(k_norm[0], N_HEADS).reshape(1, d)

    tril = jnp.tril(jnp.ones((ROW_TILE, ROW_TILE), f32)).astype(bf16)
    p_row, p_col = jnp.arange(LANES)[:, None], jnp.arange(N_HEADS * LANES)[None, :]
    p_lane = p_col % LANES
    place = jnp.where(((p_row < N_AUG * N_HEADS) & (p_col // LANES == p_row % N_HEADS)
                       & (p_lane == HEAD_DIM + N_AUG + p_row // N_HEADS))
                      | ((p_row == N_AUG * N_HEADS) & (p_lane >= HEAD_DIM) & (p_lane < HEAD_DIM + N_AUG)),
                      -1.0, 0.0).astype(bf16)
    kpt, vpt, fpt, ka, qat, vt = _attn_proj_prompt(
        y, g_mix, jnp.concatenate([w_in[:, d:2 * d], w_in[:, 3 * d:], f_pad], axis=1).astype(bf16),
        jnp.concatenate([w_in[:, :d], w_in[:, 2 * d:3 * d]], axis=1).T.astype(bf16), wft, bcol,
        jnp.concatenate([attn_b_f[0], jnp.zeros((F_PAD - N_HEADS,), f32)]).reshape(1, F_PAD),
        seg, expand2, jnp.broadcast_to(q_norm[0][:, None], (HEAD_DIM, ROW_TILE)), gk, tril, tril.T, place,
        b=bp, t=sp)
    ap = _attn_prompt(qat, ka, vt)

    kq, vq, fq, fqt, qsb, ksb, vsb = _attn_proj(
        y, g_mix, jnp.concatenate([w_in, f_pad], axis=1).astype(bf16), wft, bcol,
        attn_b_f[0].reshape(1, N_HEADS), seg, expand2, jnp.tile(q_norm[0], N_HEADS).reshape(1, d), gk,
        rows=n_s, row0=n_p)
    pad = (-(past + ss)) % SCAN_TILE
    lf_all = jnp.concatenate([jnp.swapaxes(cache_logf[0], 1, 2),
                              jnp.swapaxes(fqt.reshape(N_HEADS, bs, ss), 0, 1),
                              jnp.zeros((bs, N_HEADS, pad), f32)], axis=2)
    cst = _cumsum_lanes(lf_all, jnp.triu(jnp.ones((SCAN_TILE, SCAN_TILE), f32)).astype(bf16))
    per_stream = lambda a: a.reshape(bs, ss, d)
    channel_major = lambda a: jnp.transpose(a, (0, 2, 3, 1)).reshape(bs, d, past)
    a_s = _attn_sample(per_stream(qsb), per_stream(ksb), per_stream(vsb),
                       channel_major(cache_k[0]), channel_major(cache_v[0]),
                       jnp.swapaxes(cst[:, :, past:past + ss], 1, 2), cst)

    yp, ys = ffn(y, 1, 1, w11, mix=(ap.reshape(n_p, d), a_s.reshape(n_s, d), attn_out_b))

    hd = (N_HEADS, HEAD_DIM)
    token_major = lambda a: jnp.transpose(a.reshape(bp, *hd, sp), (0, 3, 1, 2))[None]
    return (yp.reshape(bp, sp, d), ys.reshape(bs, ss, d),
            conv_p[None], conv_s[None],
            token_major(kpt), token_major(vpt), jnp.swapaxes(fpt, 1, 2)[None],
            kq.reshape(1, bs, ss, *hd), vq.reshape(1, bs, ss, *hd), fq.reshape(1, bs, ss, N_HEADS))
```

```python
import functools

import jax
import jax.numpy as jnp
from jax import lax
from jax.experimental import pallas as pl
from jax.experimental.pallas import tpu as pltpu

D_MODEL = 1024
N_HEADS = 16
HEAD_DIM = 64
D_FF = 2816
CONV_WIDTH = 3
NORM_EPS = 1e-6
NEG_INF = -1e30
FFN_RESIDUAL = 0.5

LANES = 128
SUBLANES = 8
F_PAD = LANES
VMEM_LIMIT = 56 * 1024 * 1024
FF_CHUNKS = ((0, 1536), (1536, 2816))
ROW_TILE = 512
ATT_TILE = 256
ATT_HEADS = 16
SAMPLE_HEADS = 8
SCAN_TILE = 256
N_AUG = 3
V_ROWS = HEAD_DIM + 16
LOG2_E = 1.4426950408889634

f32 = jnp.float32
bf16 = jnp.bfloat16


def _dot(a, b):
    return jnp.dot(a, b, preferred_element_type=f32)


def _dot_nt(a, b):
    return lax.dot_general(a, b, (((1,), (1,)), ((), ())), preferred_element_type=f32)


def _rms(x, g):
    return x * lax.rsqrt(jnp.mean(x * x, axis=-1, keepdims=True) + NORM_EPS) * g


def _log_sigmoid(z):
    return -(jnp.maximum(-z, 0.0) + jnp.log1p(jnp.exp(-jnp.abs(z))))


def _split3(x):
    hi = x.astype(bf16)
    r1 = x - hi.astype(f32)
    mid = r1.astype(bf16)
    lo = (r1 - mid.astype(f32)).astype(bf16)
    return hi, mid, lo


def _params(n_axes, flags=None):
    return pltpu.CompilerParams(dimension_semantics=("arbitrary",) * n_axes,
                                vmem_limit_bytes=VMEM_LIMIT, flags=flags)


def _const_spec(shape):
    return pl.BlockSpec(shape, lambda *_: (0,) * len(shape), pipeline_mode=pl.Buffered(1))


def _cast_job(src, lead, steps, lin):
    rows, cols = src.shape[-2:]
    slab = rows // steps
    assert slab * steps == rows and slab % 16 == 0
    pos = lambda *idx: jnp.minimum(lin(*idx), steps - 1)
    in_spec = pl.BlockSpec((None,) * len(lead) + (slab, cols), lambda *idx: (*lead, pos(*idx), 0))
    out_spec = pl.BlockSpec((slab, cols), lambda *idx: (pos(*idx), 0))
    return src, in_spec, out_spec, jax.ShapeDtypeStruct((rows, cols), bf16), steps


def _run_casts(srcs, dsts):
    for src_ref, dst_ref in zip(srcs, dsts):
        dst_ref[...] = src_ref[...].astype(bf16)


def _ffn_kernel(*refs, n_prompt, split_in, mix, n_jobs):
    refs = list(refs)
    cast_srcs = cast_dsts = ()
    if n_jobs:
        cast_dsts = refs[-n_jobs:]
        del refs[-n_jobs:]
        n_out = 2 if mix else 1
        cast_srcs = refs[-n_out - n_jobs:-n_out]
        del refs[-n_out - n_jobs:-n_out]
    is_prompt = pl.program_id(0) < n_prompt

    def rows():
        if split_in or mix:
            p_ref, s_ref = refs.pop(0), refs.pop(0)
            return jnp.where(is_prompt, p_ref[...], s_ref[...])
        return refs.pop(0)[...]

    if mix:
        x = refs.pop(0)[...]
        a = rows()
        x = x + _dot(a, refs.pop(0)[...])
    else:
        x = rows()
    g_ref, win_ref, wout_ref = refs[:3]
    outs = refs[3:]
    xn = _rms(x, g_ref[...]).astype(bf16)
    acc = None
    for lo, hi in FF_CHUNKS:
        a = _dot(xn, win_ref[:, lo:hi])
        b = _dot(xn, win_ref[:, D_FF + lo:D_FF + hi])
        if lo == 0:
            _run_casts(cast_srcs, cast_dsts)
        h = (a * jax.nn.sigmoid(a) * b).astype(bf16)
        y = _dot(h, wout_ref[lo:hi, :])
        acc = y if acc is None else acc + y
    y = x + FFN_RESIDUAL * acc
    if mix:
        @pl.when(is_prompt)
        def _():
            outs[0][...] = y

        @pl.when(jnp.logical_not(is_prompt))
        def _():
            outs[1][...] = y
    else:
        outs[0][...] = y


def _ffn(xs, g, w_in, w_out, mix=None, casts=()):
    tm = ROW_TILE
    split_in = isinstance(xs, tuple)
    if mix is None and split_in:
        n_prompt, n_sample = xs[0].shape[0] // tm, xs[1].shape[0] // tm
    elif mix is not None:
        n_prompt, n_sample = mix[0].shape[0] // tm, mix[1].shape[0] // tm
    else:
        n_prompt, n_sample = xs.shape[0] // tm, 0
    assert n_sample in (0, 1)
    steps = n_prompt + n_sample
    row = pl.BlockSpec((tm, D_MODEL), lambda i: (i, 0))
    prompt_row = pl.BlockSpec((tm, D_MODEL), lambda i: (jnp.minimum(i, n_prompt - 1), 0))
    sample_row = pl.BlockSpec((tm, D_MODEL), lambda i: (0, 0))
    w_specs = [_const_spec((1, D_MODEL)), _const_spec((D_MODEL, 2 * D_FF)), _const_spec((D_FF, D_MODEL))]
    stacked = jax.ShapeDtypeStruct((steps * tm, D_MODEL), f32)
    if mix is not None:
        a_p, a_s, wm = mix
        ins = (xs, a_p, a_s, wm)
        specs = [row, prompt_row, sample_row, _const_spec((D_MODEL, D_MODEL))]
        out_specs = [prompt_row, sample_row]
        out_shape = [jax.ShapeDtypeStruct((n_prompt * tm, D_MODEL), f32),
                     jax.ShapeDtypeStruct((n_sample * tm, D_MODEL), f32)]
    elif split_in:
        ins, specs, out_specs, out_shape = tuple(xs), [prompt_row, sample_row], [row], [stacked]
    else:
        ins, specs, out_specs, out_shape = (xs,), [row], [row], [stacked]
    jobs = [_cast_job(src, lead, n, lambda i: i) for src, lead, n in casts]
    kern = functools.partial(_ffn_kernel, n_prompt=n_prompt, split_in=split_in, mix=mix is not None,
                             n_jobs=len(jobs))
    return pl.pallas_call(
        kern, grid=(steps,), in_specs=specs + w_specs + [j[1] for j in jobs],
        out_specs=out_specs + [j[2] for j in jobs], out_shape=out_shape + [j[3] for j in jobs],
        name="ffn" if mix is None else "mix_ffn",
        compiler_params=_params(1))(*ins, g, w_in, w_out, *[j[0] for j in jobs])


def _conv_body(x, g_ref, win_ref, wk_ref, wout_ref, shifted, between=lambda: None):
    xn = _rms(x, g_ref[...]).astype(bf16)
    p = _dot(xn, win_ref[...])
    between()
    gate_b = p[:, :D_MODEL]
    u = p[:, D_MODEL:2 * D_MODEL] * p[:, 2 * D_MODEL:]
    u1, u2 = shifted(u)
    wk = wk_ref[...]
    conv = wk[0:1, :] * u2 + wk[1:2, :] * u1 + wk[2:3, :] * u
    return x + _dot((gate_b * conv).astype(bf16), wout_ref[...]), u


def _conv_kernel(x_ref, g_ref, win_ref, wk_ref, wout_ref, *rest, n_jobs):
    o_ref, st_ref = rest[n_jobs:n_jobs + 2]
    carry_ref = rest[-1]
    tt = x_ref.shape[0]

    @pl.when(pl.program_id(1) == 0)
    def _():
        carry_ref[...] = jnp.zeros(carry_ref.shape, f32)

    def shifted(u):
        prev2 = carry_ref[0:1, :]
        prev1 = carry_ref[1:2, :]
        row = lax.broadcasted_iota(jnp.int32, (tt, 1), 0)
        return (jnp.where(row == 0, prev1, pltpu.roll(u, 1, 0)),
                jnp.where(row == 0, prev2, jnp.where(row == 1, prev1, pltpu.roll(u, 2, 0))))

    o_ref[...], u = _conv_body(x_ref[...], g_ref, win_ref, wk_ref, wout_ref, shifted,
                               lambda: _run_casts(rest[:n_jobs], rest[n_jobs + 2:-1]))
    last = u[tt - 2:tt, :]
    carry_ref[0:2, :] = last
    st_ref[0] = last


def _conv_mixer(x, n_streams, g, win, wk, wout, *, t, casts=()):
    tt = ROW_TILE
    nt = t // tt
    blk = pl.BlockSpec((tt, D_MODEL), lambda i, j: (i * nt + j, 0))
    st = pl.BlockSpec((1, CONV_WIDTH - 1, D_MODEL), lambda i, j: (i, 0, 0))
    specs = [blk, _const_spec((1, D_MODEL)), _const_spec((D_MODEL, 3 * D_MODEL)),
             _const_spec((CONV_WIDTH, D_MODEL)), _const_spec((D_MODEL, D_MODEL))]
    jobs = [_cast_job(src, lead, n, lambda i, j: i * nt + j) for src, lead, n in casts]
    return pl.pallas_call(
        functools.partial(_conv_kernel, n_jobs=len(jobs)), grid=(n_streams, nt),
        in_specs=specs + [j[1] for j in jobs], out_specs=[blk, st] + [j[2] for j in jobs],
        out_shape=[jax.ShapeDtypeStruct(x.shape, f32),
                   jax.ShapeDtypeStruct((n_streams, CONV_WIDTH - 1, D_MODEL), f32)] + [j[3] for j in jobs],
        scratch_shapes=[pltpu.VMEM((8, D_MODEL), f32)],
        name="conv_mixer", compiler_params=_params(2))(x, g, win, wk, wout, *[j[0] for j in jobs])


def _conv_sample_kernel(x_ref, h1_ref, h2_ref, g_ref, win_ref, wk_ref, wout_ref, into_ref, o_ref, u_ref, *, t):
    rows = x_ref.shape[0]

    def shifted(u):
        pos = lax.rem(lax.broadcasted_iota(jnp.int32, (rows, 1), 0), t)
        h1 = h1_ref[...]
        return (jnp.where(pos == 0, h1, pltpu.roll(u, 1, 0)),
                jnp.where(pos == 0, h2_ref[...], jnp.where(pos == 1, h1, pltpu.roll(u, 2, 0))))

    o_ref[...], u_ref[...] = _conv_body(x_ref[...], g_ref, win_ref, wk_ref, wout_ref, shifted)


def _conv_mixer_sample(x, state, g, win, wk, wout, *, t, row0, into):
    n = state.shape[0]
    rows = n * t
    blk = pl.BlockSpec((rows, D_MODEL), lambda i: (row0 // rows, 0))
    hist = pl.BlockSpec((rows, D_MODEL), lambda i: (0, 0))
    y, u = pl.pallas_call(
        functools.partial(_conv_sample_kernel, t=t), grid=(1,),
        in_specs=[blk, hist, hist, _const_spec((1, D_MODEL)), _const_spec((D_MODEL, 3 * D_MODEL)),
                  _const_spec((CONV_WIDTH, D_MODEL)), _const_spec((D_MODEL, D_MODEL)),
                  pl.BlockSpec(memory_space=pl.ANY)],
        out_specs=[blk, hist],
        out_shape=[jax.ShapeDtypeStruct(x.shape, f32), jax.ShapeDtypeStruct((rows, D_MODEL), f32)],
        input_output_aliases={7: 0},
        name="conv_mixer_sample", compiler_params=_params(1))(
            x, jnp.repeat(state[:, 1], t, axis=0), jnp.repeat(state[:, 0], t, axis=0), g, win, wk, wout, into)
    return y, u.reshape(n, t, D_MODEL)[:, t - (CONV_WIDTH - 1):]


def _head_norm(t, gain, seg_ref, exp_ref):
    ms = _dot((t * t).astype(bf16), seg_ref[...])
    r = lax.rsqrt(ms + NORM_EPS)
    r_hi = r.astype(bf16)
    r_lo = (r - r_hi.astype(f32)).astype(bf16)
    rb = _dot(jnp.concatenate([r_hi, r_lo], axis=1), exp_ref[...])
    return t * rb * gain


def _proj_kernel(x_ref, g_ref, w_ref, wft_ref, bcol_ref, brow_ref, seg_ref, exp_ref, gq_ref, gk_ref,
                 k_ref, v_ref, lf_ref, lft_ref, qb_ref, kb_ref, vb_ref):
    xn = _rms(x_ref[...], g_ref[...]).astype(bf16)
    p = _dot(xn, w_ref[...])
    q = _head_norm(p[:, :D_MODEL], gq_ref[...], seg_ref, exp_ref)
    k = _head_norm(p[:, D_MODEL:2 * D_MODEL], gk_ref[...], seg_ref, exp_ref)
    v = p[:, 2 * D_MODEL:3 * D_MODEL]
    k_ref[...] = k
    v_ref[...] = v
    lf_ref[...] = _log_sigmoid(p[:, 3 * D_MODEL:3 * D_MODEL + N_HEADS] + brow_ref[...])
    lft_ref[...] = _log_sigmoid(_dot_nt(wft_ref[...], xn) + bcol_ref[...])
    qb_ref[...] = (q * (HEAD_DIM ** -0.5)).astype(bf16)
    kb_ref[...] = k.astype(bf16)
    vb_ref[...] = v.astype(bf16)


def _attn_proj(x, g, w, wft, bcol, brow, seg, exp, gq, gk, *, rows, row0):
    blk = lambda dt: jax.ShapeDtypeStruct((rows, D_MODEL), dt)
    full = lambda shape: pl.BlockSpec(shape, lambda i: (0, 0))
    return pl.pallas_call(
        _proj_kernel, grid=(1,),
        in_specs=[pl.BlockSpec((rows, D_MODEL), lambda i: (row0 // rows, 0)),
                  _const_spec((1, D_MODEL)), _const_spec((D_MODEL, 3 * D_MODEL + F_PAD)),
                  _const_spec((N_HEADS, D_MODEL)), _const_spec((N_HEADS, 1)), _const_spec((1, N_HEADS)),
                  _const_spec((D_MODEL, LANES)), _const_spec((2 * LANES, D_MODEL)),
                  _const_spec((1, D_MODEL)), _const_spec((1, D_MODEL))],
        out_specs=[full((rows, D_MODEL)), full((rows, D_MODEL)), full((rows, N_HEADS)), full((N_HEADS, rows)),
                   full((rows, D_MODEL)), full((rows, D_MODEL)), full((rows, D_MODEL))],
        out_shape=[blk(f32), blk(f32), jax.ShapeDtypeStruct((rows, N_HEADS), f32),
                   jax.ShapeDtypeStruct((N_HEADS, rows), f32), blk(bf16), blk(bf16), blk(bf16)],
        name="attn_proj", compiler_params=_params(1))(x, g, w, wft, bcol, brow, seg, exp, gq, gk)


def _proj_prompt_kernel(x_ref, g_ref, wt_ref, bcol_ref, gqc_ref, gkc_ref, triu_ref,
                        kt_ref, vt_ref, lft_ref, ka_ref, qat_ref, vtb_ref, ccol_ref):
    t = pl.program_id(1)
    tt = x_ref.shape[0]
    n_tiles = tt // ATT_TILE

    @pl.when(t == 0)
    def _():
        ccol_ref[...] = jnp.zeros(ccol_ref.shape, f32)

    xn = _rms(x_ref[...], g_ref[...]).astype(bf16)
    z = _dot_nt(wt_ref[...], xn)
    qt = z[:D_MODEL]
    kt = z[D_MODEL:2 * D_MODEL]
    vt = z[2 * D_MODEL:3 * D_MODEL]

    lft = _log_sigmoid(z[3 * D_MODEL:] + bcol_ref[...])
    lft_ref[0] = lft
    ct3 = _dot(jnp.concatenate(_split3(lft), axis=0), triu_ref[...])
    ct = ct3[:N_HEADS] + ct3[N_HEADS:2 * N_HEADS] + ct3[2 * N_HEADS:] + ccol_ref[:, 0:1]
    ccol_ref[...] = jnp.broadcast_to(ct[:, tt - 1:tt], ccol_ref.shape)
    ct_pieces = [piece.astype(f32) for piece in _split3(ct * LOG2_E)]

    vt_ref[0] = vt
    vtb = vt.astype(bf16)
    ones_rows = jnp.where(lax.broadcasted_iota(jnp.int32, (V_ROWS - HEAD_DIM, ATT_TILE), 0) == 0, 1.0, 0.0).astype(bf16)
    for j in range(n_tiles):
        for h in range(N_HEADS):
            vtb_ref[0, j, h * V_ROWS:h * V_ROWS + HEAD_DIM, :] = (
                vtb[h * HEAD_DIM:(h + 1) * HEAD_DIM, j * ATT_TILE:(j + 1) * ATT_TILE])
            vtb_ref[0, j, h * V_ROWS + HEAD_DIM:(h + 1) * V_ROWS, :] = ones_rows

    sub = lax.broadcasted_iota(jnp.int32, (SUBLANES, 1), 0)
    ones = jnp.where(sub < 2 * N_AUG, 1.0, 0.0) * jnp.ones((1, tt), f32)
    pad = jnp.zeros((LANES - HEAD_DIM - SUBLANES, tt), f32)
    gqc = gqc_ref[...]
    gkc = gkc_ref[...]

    def head_rms(th, gain):
        return th * lax.rsqrt(jnp.mean(th * th, axis=0, keepdims=True) + NORM_EPS) * gain

    for h in range(N_HEADS):
        rows = slice(h * HEAD_DIM, (h + 1) * HEAD_DIM)
        kn = head_rms(kt[rows, :], gkc)
        kt_ref[0, rows, :] = kn
        aug_k, aug_q = ones, ones
        for n in range(N_AUG):
            piece = ct_pieces[n][h:h + 1, :]
            aug_k = jnp.where(sub == N_AUG + n, -piece, aug_k)
            aug_q = jnp.where(sub == n, piece, aug_q)
        ka_ref[0, h] = jnp.concatenate([kn, aug_k, pad], axis=0).T.astype(bf16)
        blk = jnp.concatenate([head_rms(qt[rows, :], gqc) * (LOG2_E * HEAD_DIM ** -0.5), aug_q, pad],
                              axis=0).astype(bf16)
        for j in range(n_tiles):
            qat_ref[0, j, h * LANES:(h + 1) * LANES, :] = blk[:, j * ATT_TILE:(j + 1) * ATT_TILE]


def _attn_proj_prompt(x, g, wt, bcol, gqc, gkc, triu, *, b, t):
    tt = ROW_TILE
    nt = tt // ATT_TILE
    time_minor = pl.BlockSpec((1, D_MODEL, tt), lambda i, j: (i, 0, j))
    big = jax.ShapeDtypeStruct((b, D_MODEL, t), f32)
    return pl.pallas_call(
        _proj_prompt_kernel, grid=(b, t // tt),
        in_specs=[pl.BlockSpec((tt, D_MODEL), lambda i, j: (i * (t // tt) + j, 0)),
                  _const_spec((1, D_MODEL)), _const_spec((3 * D_MODEL + N_HEADS, D_MODEL)),
                  _const_spec((N_HEADS, 1)), _const_spec((HEAD_DIM, tt)), _const_spec((HEAD_DIM, tt)),
                  _const_spec((tt, tt))],
        out_specs=[time_minor, time_minor,
                   pl.BlockSpec((1, N_HEADS, tt), lambda i, j: (i, 0, j)),
                   pl.BlockSpec((1, N_HEADS, tt, LANES), lambda i, j: (i, 0, j, 0)),
                   pl.BlockSpec((1, nt, N_HEADS * LANES, ATT_TILE), lambda i, j: (i, j, 0, 0)),
                   pl.BlockSpec((1, nt, N_HEADS * V_ROWS, ATT_TILE), lambda i, j: (i, j, 0, 0))],
        out_shape=[big, big, jax.ShapeDtypeStruct((b, N_HEADS, t), f32),
                   jax.ShapeDtypeStruct((b, N_HEADS, t, LANES), bf16),
                   jax.ShapeDtypeStruct((b, t // ATT_TILE, N_HEADS * LANES, ATT_TILE), bf16),
                   jax.ShapeDtypeStruct((b, t // ATT_TILE, N_HEADS * V_ROWS, ATT_TILE), bf16)],
        scratch_shapes=[pltpu.VMEM((N_HEADS, LANES), f32)],
        name="attn_proj_prompt", compiler_params=_params(2))(x, g, wt, bcol, gqc, gkc, triu)


def _scan_kernel(x_ref, tri_ref, o_ref):
    rows = x_ref.shape[0]
    n = x_ref.shape[1] // SCAN_TILE
    carry = jnp.zeros((rows, 1), f32)
    for c in range(n):
        sl = slice(c * SCAN_TILE, (c + 1) * SCAN_TILE)
        cs = _dot(jnp.concatenate(_split3(x_ref[:, sl]), axis=0), tri_ref[...])
        out = cs[:rows] + cs[rows:2 * rows] + cs[2 * rows:] + carry
        o_ref[:, sl] = out
        carry = out[:, SCAN_TILE - 1:SCAN_TILE]


def _cumsum_lanes(lft, tri):
    b, h, t = lft.shape
    blk = pl.BlockSpec((b * h, t), lambda i: (0, 0))
    return pl.pallas_call(
        _scan_kernel, grid=(1,), in_specs=[blk, _const_spec((SCAN_TILE, SCAN_TILE))], out_specs=blk,
        out_shape=jax.ShapeDtypeStruct((b * h, t), f32), name="cumsum_lanes",
        compiler_params=_params(1))(lft.reshape(b * h, t), tri).reshape(b, h, t)


def _attn_kernel(qat_ref, qnext_ref, ka_ref, vt_ref, o_ref, m_ref, acc_ref, sa_ref, sb_ref):
    i = pl.program_id(2)
    tq = qat_ref.shape[3]
    tk = vt_ref.shape[3]
    causal = (lax.broadcasted_iota(jnp.int32, (tk, tq), 0) <= lax.broadcasted_iota(jnp.int32, (tk, tq), 1))
    m_ref[...] = jnp.full(m_ref.shape, NEG_INF, f32)
    acc_ref[...] = jnp.zeros(acc_ref.shape, f32)

    def scores(q_ref, j, s, buf):
        rows = pl.ds(pl.multiple_of(j * tk, tk), tk)
        buf[s] = _dot(ka_ref[0, s, rows, :], q_ref[0, 0, s * LANES:(s + 1) * LANES, :])

    def consume(j, s, buf, masked):
        sc = buf[s]
        if masked:
            sc = jnp.where(causal, sc, NEG_INF)
        m = m_ref[s]
        m_new = jnp.maximum(m, jnp.max(sc, axis=0, keepdims=True))
        alpha = jnp.exp2(m - m_new)
        pr = jnp.exp2(sc - m_new)
        vt = vt_ref[0, j, s * V_ROWS:(s + 1) * V_ROWS, :]
        acc_ref[s] = alpha * acc_ref[s] + _dot(vt, pr.astype(bf16))
        m_ref[s] = m_new

    heads = range(ATT_HEADS)

    def step(j, cur, nxt):
        for s in heads:
            scores(qat_ref, j + 1, s, nxt)
            consume(j, s, cur, False)

    def last_step(cur, nxt):
        for s in heads:
            scores(qnext_ref, 0, s, nxt)
            consume(i, s, cur, True)

    @pl.when(i == 0)
    def _():
        for s in heads:
            scores(qat_ref, 0, s, sa_ref)

    def sweep(a, b):
        def two_steps(t, carry):
            step(2 * t, a, b)
            step(2 * t + 1, b, a)
            return carry

        lax.fori_loop(0, i // 2, two_steps, 0)

        @pl.when(i % 2 == 0)
        def _():
            last_step(a, b)

        @pl.when(i % 2 == 1)
        def _():
            step(i - 1, a, b)
            last_step(b, a)

    first_in_b = ((i * (i + 1)) // 2) % 2

    @pl.when(first_in_b == 0)
    def _():
        sweep(sa_ref, sb_ref)

    @pl.when(first_in_b == 1)
    def _():
        sweep(sb_ref, sa_ref)

    o_t = jnp.concatenate([acc_ref[s, :HEAD_DIM, :] / acc_ref[s, HEAD_DIM:HEAD_DIM + 1, :]
                           for s in range(ATT_HEADS)], axis=0)
    o_ref[0] = o_t.T.astype(bf16)


def _attn_prompt(qat, ka, vt):
    b, nq, _, tq = qat.shape
    s = nq * tq
    g = ATT_HEADS
    return pl.pallas_call(
        _attn_kernel, grid=(b, N_HEADS // g, nq),
        in_specs=[pl.BlockSpec((1, 1, g * LANES, tq), lambda i, h, j: (i, j, h, 0)),
                  pl.BlockSpec((1, 1, g * LANES, tq), lambda i, h, j: (i, jnp.minimum(j + 1, nq - 1), h, 0)),
                  pl.BlockSpec((1, g, s, LANES), lambda i, h, j: (i, h, 0, 0)),
                  pl.BlockSpec((1, nq, g * V_ROWS, tq), lambda i, h, j: (i, 0, h, 0))],
        out_specs=pl.BlockSpec((1, tq, g * HEAD_DIM), lambda i, h, j: (i, j, h)),
        out_shape=jax.ShapeDtypeStruct((b, s, D_MODEL), bf16),
        scratch_shapes=[pltpu.VMEM((g, 1, tq), f32), pltpu.VMEM((g, V_ROWS, tq), f32),
                        pltpu.VMEM((g, tq, tq), f32), pltpu.VMEM((g, tq, tq), f32)],
        name="attn_prompt",
        compiler_params=_params(3))(qat, qat, ka, vt)


def _attn_sample_kernel(q_ref, kn_ref, vn_ref, kp_ref, vp_ref, cq_ref, ck_ref, o_ref):
    t = q_ref.shape[1]
    past = kp_ref.shape[2]
    low = lax.broadcasted_iota(jnp.int32, (1, LANES), 1) < HEAD_DIM
    causal = (lax.broadcasted_iota(jnp.int32, (t, t), 1) <= lax.broadcasted_iota(jnp.int32, (t, t), 0))
    n_pairs = q_ref.shape[2] // LANES

    def pair_scores(hp):
        lanes = slice(hp * LANES, (hp + 1) * LANES)
        q2 = q_ref[0, :, lanes]
        kp2 = kp_ref[0, lanes, :].astype(bf16)
        kn2 = kn_ref[0, :, lanes]
        out = []
        for sub in range(2):
            qm = jnp.where(low if sub == 0 else jnp.logical_not(low), q2, jnp.zeros_like(q2))
            out.append((_dot(qm, kp2), _dot_nt(qm, kn2)))
        return out

    ahead = pair_scores(0)
    for hp in range(n_pairs):
        raw = ahead
        if hp + 1 < n_pairs:
            ahead = pair_scores(hp + 1)
        lanes = slice(hp * LANES, (hp + 1) * LANES)
        vp2 = vp_ref[0, lanes, :].astype(bf16)
        vn2 = vn_ref[0, :, lanes]
        halves = []
        for sub in range(2):
            h = 2 * hp + sub
            cq = cq_ref[0, 0, :, h:h + 1]
            s_past = raw[sub][0] + cq - ck_ref[0, h:h + 1, :past]
            s_new = raw[sub][1] + cq - ck_ref[0, h:h + 1, past:past + t]
            s_new = jnp.where(causal, s_new, NEG_INF)
            m = jnp.maximum(jnp.max(s_past, axis=1, keepdims=True), jnp.max(s_new, axis=1, keepdims=True))
            p_past = jnp.exp(s_past - m)
            p_new = jnp.exp(s_new - m)
            l = jnp.sum(p_past, axis=1, keepdims=True) + jnp.sum(p_new, axis=1, keepdims=True)
            acc = _dot_nt(p_past.astype(bf16), vp2) + _dot(p_new.astype(bf16), vn2)
            halves.append(acc / l)
        o_ref[0, :, lanes] = jnp.where(low, halves[0], halves[1]).astype(bf16)


def _attn_sample(qb, kb, vb, past_kt, past_vt, c_nat, c_t):
    b, t, _ = qb.shape
    past = past_kt.shape[2]
    g = SAMPLE_HEADS
    width = g * HEAD_DIM
    new = pl.BlockSpec((1, t, width), lambda i, h: (i, 0, h))
    old = pl.BlockSpec((1, width, past), lambda i, h: (i, h, 0))
    return pl.pallas_call(
        _attn_sample_kernel, grid=(b, N_HEADS // g),
        in_specs=[new, new, new, old, old,
                  pl.BlockSpec((1, 1, t, g), lambda i, h: (i, h, 0, 0)),
                  pl.BlockSpec((1, g, c_t.shape[2]), lambda i, h: (i, h, 0))],
        out_specs=new,
        out_shape=jax.ShapeDtypeStruct((b, t, D_MODEL), bf16),
        name="attn_sample", compiler_params=_params(2))(qb, kb, vb, past_kt, past_vt, c_nat, c_t)


def kernel(x_prompt, x_sample, state_conv, cache_k, cache_v, cache_logf, norm_ffn, ffn_w_in, ffn_w_out, norm_mix, conv_w_in, conv_w, conv_w_out, attn_w_in, attn_b_f, q_norm, k_norm, attn_w_out):
    bp, sp, d = x_prompt.shape
    bs, ss, _ = x_sample.shape
    past = cache_k.shape[2]

    n_p, n_s = bp * sp, bs * ss
    def ffn(xs, i, j, w, **kw):
        return _ffn(xs, norm_ffn[i, j].reshape(1, d), *w, **kw)

    w_slabs, wo_slabs = 32, 16
    ffn_casts = lambda i, j: [(ffn_w_in, (i, j), w_slabs), (ffn_w_out, (i, j), wo_slabs)]
    square_cast = lambda w: (w, (0,), w_slabs)

    y, conv_in_b, conv_out_b = ffn((x_prompt.reshape(n_p, d), x_sample.reshape(n_s, d)), 0, 0,
                                   (ffn_w_in[0, 0].astype(bf16), ffn_w_out[0, 0].astype(bf16)),
                                   casts=[square_cast(conv_w_in), square_cast(conv_w_out)])
    cw = (norm_mix[0].reshape(1, d), conv_in_b, conv_w[0], conv_out_b)
    y_mixed, conv_p, *w01 = _conv_mixer(y, bp, *cw, t=sp, casts=ffn_casts(0, 1))
    y, conv_s = _conv_mixer_sample(y, state_conv[0], *cw, t=ss, row0=n_p, into=y_mixed)
    y, *w10 = ffn(y, 0, 1, w01, casts=ffn_casts(1, 0))

    y, *w11, attn_out_b = ffn(y, 1, 0, w10, casts=ffn_casts(1, 1) + [square_cast(attn_w_out)])

    w_in = attn_w_in[0]
    f_pad = jnp.zeros((d, F_PAD - N_HEADS), f32)
    wft = w_in[:, 3 * d:].T.astype(bf16)
    head_of = jnp.arange(d) // HEAD_DIM
    seg = ((head_of[:, None] == jnp.arange(LANES)[None, :]).astype(f32) * (1.0 / HEAD_DIM)).astype(bf16)
    expand = (jnp.arange(LANES)[:, None] == head_of[None, :]).astype(bf16)
    expand2 = jnp.concatenate([expand, expand], axis=0)
    g_mix = norm_mix[1].reshape(1, d)
    bcol = attn_b_f[0].reshape(N_HEADS, 1)
    gk = jnp.tile(k_norm[0], N_HEADS).reshape(1, d)

    per_channel = lambda gain: jnp.broadcast_to(gain[:, None], (HEAD_DIM, ROW_TILE))
    kpt, vpt, fpt, ka, qat, vt = _attn_proj_prompt(
        y, g_mix, w_in.T.astype(bf16), bcol, per_channel(q_norm[0]), per_channel(k_norm[0]),
        jnp.triu(jnp.ones((ROW_TILE, ROW_TILE), f32)).astype(bf16), b=bp, t=sp)
    ap = _attn_prompt(qat, ka, vt)

    kq, vq, fq, fqt, qsb, ksb, vsb = _attn_proj(
        y, g_mix, jnp.concatenate([w_in, f_pad], axis=1).astype(bf16), wft, bcol,
        attn_b_f[0].reshape(1, N_HEADS), seg, expand2, jnp.tile(q_norm[0], N_HEADS).reshape(1, d), gk,
        rows=n_s, row0=n_p)
    pad = (-(past + ss)) % SCAN_TILE
    lf_all = jnp.concatenate([jnp.swapaxes(cache_logf[0], 1, 2),
                              jnp.swapaxes(fqt.reshape(N_HEADS, bs, ss), 0, 1),
                              jnp.zeros((bs, N_HEADS, pad), f32)], axis=2)
    cst = _cumsum_lanes(lf_all, jnp.triu(jnp.ones((SCAN_TILE, SCAN_TILE), f32)).astype(bf16))
    per_stream = lambda a: a.reshape(bs, ss, d)
    channel_major = lambda a: jnp.transpose(a, (0, 2, 3, 1)).reshape(bs, d, past)
    a_s = _attn_sample(per_stream(qsb), per_stream(ksb), per_stream(vsb),
                       channel_major(cache_k[0]), channel_major(cache_v[0]),
                       jnp.swapaxes(cst[:, :, past:past + ss].reshape(bs, -1, SAMPLE_HEADS, ss), 2, 3), cst)

    yp, ys = ffn(y, 1, 1, w11, mix=(ap.reshape(n_p, d), a_s.reshape(n_s, d), attn_out_b))

    hd = (N_HEADS, HEAD_DIM)
    token_major = lambda a: jnp.transpose(a.reshape(bp, *hd, sp), (0, 3, 1, 2))[None]
    return (yp.reshape(bp, sp, d), ys.reshape(bs, ss, d),
            conv_p[None], conv_s[None],
            token_major(kpt), token_major(vpt), jnp.swapaxes(fpt, 1, 2)[None],
            kq.reshape(1, bs, ss, *hd), vq.reshape(1, bs, ss, *hd), fq.reshape(1, bs, ss, N_HEADS))
```

```python
import functools

import jax
import jax.numpy as jnp
from jax import lax
from jax.experimental import pallas as pl
from jax.experimental.pallas import tpu as pltpu

D_MODEL = 1024
N_HEADS = 16
HEAD_DIM = 64
D_FF = 2816
CONV_WIDTH = 3
NORM_EPS = 1e-6
NEG_INF = -1e30
FFN_RESIDUAL = 0.5

LANES = 128
F_ROWS = 128
SUBLANES = 8
VMEM_LIMIT = 56 * 1024 * 1024
FF_CHUNKS = ((0, 1536), (1536, 2816))
ROW_TILE = 512
ATT_TILE = 256
ATT_HEADS = 16
SAMPLE_HEADS = 8
SCAN_TILE = 256
N_AUG = 3
V_ROWS = HEAD_DIM + 16
LOG2_E = 1.4426950408889634

f32 = jnp.float32
bf16 = jnp.bfloat16


def _dot(a, b):
    return jnp.dot(a, b, preferred_element_type=f32)


def _dot_nt(a, b):
    return lax.dot_general(a, b, (((1,), (1,)), ((), ())), preferred_element_type=f32)


def _rms(x, g):
    return x * lax.rsqrt(jnp.mean(x * x, axis=-1, keepdims=True) + NORM_EPS) * g


def _log_sigmoid(z):
    return -(jnp.maximum(-z, 0.0) + jnp.log1p(jnp.exp(-jnp.abs(z))))


def _split3(x):
    hi = x.astype(bf16)
    r1 = x - hi.astype(f32)
    mid = r1.astype(bf16)
    lo = (r1 - mid.astype(f32)).astype(bf16)
    return hi, mid, lo


def _params(n_axes, flags=None):
    return pltpu.CompilerParams(dimension_semantics=("arbitrary",) * n_axes,
                                vmem_limit_bytes=VMEM_LIMIT, flags=flags)


def _const_spec(shape):
    return pl.BlockSpec(shape, lambda *_: (0,) * len(shape), pipeline_mode=pl.Buffered(1))


def _cast_job(src, lead, steps, lin):
    rows, cols = src.shape[-2:]
    slab = rows // steps
    assert slab * steps == rows and slab % 16 == 0
    pos = lambda *idx: jnp.minimum(lin(*idx), steps - 1)
    in_spec = pl.BlockSpec((None,) * len(lead) + (slab, cols), lambda *idx: (*lead, pos(*idx), 0))
    out_spec = pl.BlockSpec((slab, cols), lambda *idx: (pos(*idx), 0))
    return src, in_spec, out_spec, jax.ShapeDtypeStruct((rows, cols), bf16), steps


def _run_casts(srcs, dsts):
    for src_ref, dst_ref in zip(srcs, dsts):
        dst_ref[...] = src_ref[...].astype(bf16)


def _ffn_kernel(*refs, n_prompt, split_in, mix, n_jobs):
    refs = list(refs)
    cast_srcs = cast_dsts = ()
    if n_jobs:
        cast_dsts = refs[-n_jobs:]
        del refs[-n_jobs:]
        n_out = 2 if mix else 1
        cast_srcs = refs[-n_out - n_jobs:-n_out]
        del refs[-n_out - n_jobs:-n_out]
    is_prompt = pl.program_id(0) < n_prompt

    def rows():
        if split_in or mix:
            p_ref, s_ref = refs.pop(0), refs.pop(0)
            return jnp.where(is_prompt, p_ref[...], s_ref[...])
        return refs.pop(0)[...]

    if mix:
        x = refs.pop(0)[...]
        a = rows()
        x = x + _dot(a, refs.pop(0)[...])
    else:
        x = rows()
    g_ref, win_ref, wout_ref = refs[:3]
    outs = refs[3:]
    xn = _rms(x, g_ref[...]).astype(bf16)
    acc = None
    for lo, hi in FF_CHUNKS:
        a = _dot(xn, win_ref[:, lo:hi])
        b = _dot(xn, win_ref[:, D_FF + lo:D_FF + hi])
        if lo == 0:
            _run_casts(cast_srcs, cast_dsts)
        h = (a * jax.nn.sigmoid(a) * b).astype(bf16)
        y = _dot(h, wout_ref[lo:hi, :])
        acc = y if acc is None else acc + y
    y = x + FFN_RESIDUAL * acc
    if mix:
        @pl.when(is_prompt)
        def _():
            outs[0][...] = y

        @pl.when(jnp.logical_not(is_prompt))
        def _():
            outs[1][...] = y
    else:
        outs[0][...] = y


def _ffn(xs, g, w_in, w_out, mix=None, casts=()):
    tm = ROW_TILE
    split_in = isinstance(xs, tuple)
    if mix is None and split_in:
        n_prompt, n_sample = xs[0].shape[0] // tm, xs[1].shape[0] // tm
    elif mix is not None:
        n_prompt, n_sample = mix[0].shape[0] // tm, mix[1].shape[0] // tm
    else:
        n_prompt, n_sample = xs.shape[0] // tm, 0
    assert n_sample in (0, 1)
    steps = n_prompt + n_sample
    row = pl.BlockSpec((tm, D_MODEL), lambda i: (i, 0))
    prompt_row = pl.BlockSpec((tm, D_MODEL), lambda i: (jnp.minimum(i, n_prompt - 1), 0))
    sample_row = pl.BlockSpec((tm, D_MODEL), lambda i: (0, 0))
    w_specs = [_const_spec((1, D_MODEL)), _const_spec((D_MODEL, 2 * D_FF)), _const_spec((D_FF, D_MODEL))]
    stacked = jax.ShapeDtypeStruct((steps * tm, D_MODEL), f32)
    if mix is not None:
        a_p, a_s, wm = mix
        ins = (xs, a_p, a_s, wm)
        specs = [row, prompt_row, sample_row, _const_spec((D_MODEL, D_MODEL))]
        out_specs = [prompt_row, sample_row]
        out_shape = [jax.ShapeDtypeStruct((n_prompt * tm, D_MODEL), f32),
                     jax.ShapeDtypeStruct((n_sample * tm, D_MODEL), f32)]
    elif split_in:
        ins, specs, out_specs, out_shape = tuple(xs), [prompt_row, sample_row], [row], [stacked]
    else:
        ins, specs, out_specs, out_shape = (xs,), [row], [row], [stacked]
    jobs = [_cast_job(src, lead, n, lambda i: i) for src, lead, n in casts]
    kern = functools.partial(_ffn_kernel, n_prompt=n_prompt, split_in=split_in, mix=mix is not None,
                             n_jobs=len(jobs))
    return pl.pallas_call(
        kern, grid=(steps,), in_specs=specs + w_specs + [j[1] for j in jobs],
        out_specs=out_specs + [j[2] for j in jobs], out_shape=out_shape + [j[3] for j in jobs],
        name="ffn" if mix is None else "mix_ffn",
        compiler_params=_params(1))(*ins, g, w_in, w_out, *[j[0] for j in jobs])


def _conv_body(x, g_ref, win_ref, wk_ref, wout_ref, shifted, between=lambda: None):
    xn = _rms(x, g_ref[...]).astype(bf16)
    ch = _dot(xn, win_ref[:, D_MODEL:])
    gate_b = _dot(xn, win_ref[:, :D_MODEL])
    between()
    u = ch[:, :D_MODEL] * ch[:, D_MODEL:]
    u1, u2 = shifted(u)
    wk = wk_ref[...]
    conv = wk[0:1, :] * u2 + wk[1:2, :] * u1 + wk[2:3, :] * u
    return x + _dot((gate_b * conv).astype(bf16), wout_ref[...]), u


def _conv_kernel(x_ref, g_ref, win_ref, wk_ref, wout_ref, *rest, n_jobs):
    o_ref, st_ref = rest[n_jobs:n_jobs + 2]
    carry_ref = rest[-1]
    tt = x_ref.shape[0]

    @pl.when(pl.program_id(1) == 0)
    def _():
        carry_ref[...] = jnp.zeros(carry_ref.shape, f32)

    def shifted(u):
        prev2 = carry_ref[0:1, :]
        prev1 = carry_ref[1:2, :]
        row = lax.broadcasted_iota(jnp.int32, (tt, 1), 0)
        return (jnp.where(row == 0, prev1, pltpu.roll(u, 1, 0)),
                jnp.where(row == 0, prev2, jnp.where(row == 1, prev1, pltpu.roll(u, 2, 0))))

    o_ref[...], u = _conv_body(x_ref[...], g_ref, win_ref, wk_ref, wout_ref, shifted,
                               lambda: _run_casts(rest[:n_jobs], rest[n_jobs + 2:-1]))
    last = u[tt - 2:tt, :]
    carry_ref[0:2, :] = last
    st_ref[0] = last


def _conv_mixer(x, n_streams, g, win, wk, wout, *, t, casts=()):
    tt = ROW_TILE
    nt = t // tt
    blk = pl.BlockSpec((tt, D_MODEL), lambda i, j: (i * nt + j, 0))
    st = pl.BlockSpec((1, CONV_WIDTH - 1, D_MODEL), lambda i, j: (i, 0, 0))
    specs = [blk, _const_spec((1, D_MODEL)), _const_spec((D_MODEL, 3 * D_MODEL)),
             _const_spec((CONV_WIDTH, D_MODEL)), _const_spec((D_MODEL, D_MODEL))]
    jobs = [_cast_job(src, lead, n, lambda i, j: i * nt + j) for src, lead, n in casts]
    return pl.pallas_call(
        functools.partial(_conv_kernel, n_jobs=len(jobs)), grid=(n_streams, nt),
        in_specs=specs + [j[1] for j in jobs], out_specs=[blk, st] + [j[2] for j in jobs],
        out_shape=[jax.ShapeDtypeStruct(x.shape, f32),
                   jax.ShapeDtypeStruct((n_streams, CONV_WIDTH - 1, D_MODEL), f32)] + [j[3] for j in jobs],
        scratch_shapes=[pltpu.VMEM((8, D_MODEL), f32)],
        name="conv_mixer", compiler_params=_params(2))(x, g, win, wk, wout, *[j[0] for j in jobs])


def _conv_sample_kernel(x_ref, h1_ref, h2_ref, g_ref, win_ref, wk_ref, wout_ref, into_ref, o_ref, u_ref, *, t):
    rows = x_ref.shape[0]

    def shifted(u):
        pos = lax.rem(lax.broadcasted_iota(jnp.int32, (rows, 1), 0), t)
        h1 = h1_ref[...]
        return (jnp.where(pos == 0, h1, pltpu.roll(u, 1, 0)),
                jnp.where(pos == 0, h2_ref[...], jnp.where(pos == 1, h1, pltpu.roll(u, 2, 0))))

    o_ref[...], u_ref[...] = _conv_body(x_ref[...], g_ref, win_ref, wk_ref, wout_ref, shifted)


def _conv_mixer_sample(x, state, g, win, wk, wout, *, t, row0, into):
    n = state.shape[0]
    rows = n * t
    blk = pl.BlockSpec((rows, D_MODEL), lambda i: (row0 // rows, 0))
    hist = pl.BlockSpec((rows, D_MODEL), lambda i: (0, 0))
    y, u = pl.pallas_call(
        functools.partial(_conv_sample_kernel, t=t), grid=(1,),
        in_specs=[blk, hist, hist, _const_spec((1, D_MODEL)), _const_spec((D_MODEL, 3 * D_MODEL)),
                  _const_spec((CONV_WIDTH, D_MODEL)), _const_spec((D_MODEL, D_MODEL)),
                  pl.BlockSpec(memory_space=pl.ANY)],
        out_specs=[blk, hist],
        out_shape=[jax.ShapeDtypeStruct(x.shape, f32), jax.ShapeDtypeStruct((rows, D_MODEL), f32)],
        input_output_aliases={7: 0},
        name="conv_mixer_sample", compiler_params=_params(1))(
            x, jnp.repeat(state[:, 1], t, axis=0), jnp.repeat(state[:, 0], t, axis=0), g, win, wk, wout, into)
    return y, u.reshape(n, t, D_MODEL)[:, t - (CONV_WIDTH - 1):]


def _head_norm(t, gain, seg_ref, exp_ref):
    ms = _dot((t * t).astype(bf16), seg_ref[...])
    r = lax.rsqrt(ms + NORM_EPS)
    r_hi = r.astype(bf16)
    r_lo = (r - r_hi.astype(f32)).astype(bf16)
    rb = _dot(jnp.concatenate([r_hi, r_lo], axis=1), exp_ref[...])
    return t * rb * gain


def _proj_kernel(x_ref, g_ref, wt_ref, bcol_ref, brow_ref, seg_ref, exp_ref, gq_ref, gk_ref,
                 k_ref, v_ref, lf_ref, lft_ref, qb_ref, kb_ref, vb_ref):
    xn = _rms(x_ref[...], g_ref[...]).astype(bf16)
    p = _dot_nt(xn, wt_ref[...])
    k = _head_norm(p[:, F_ROWS:F_ROWS + D_MODEL], gk_ref[...], seg_ref, exp_ref)
    q = _head_norm(p[:, F_ROWS + D_MODEL:F_ROWS + 2 * D_MODEL], gq_ref[...], seg_ref, exp_ref)
    v = p[:, F_ROWS + 2 * D_MODEL:]
    k_ref[...] = k
    v_ref[...] = v
    lf_ref[...] = _log_sigmoid(p[:, :N_HEADS] + brow_ref[...])
    lft_ref[...] = _log_sigmoid(_dot_nt(wt_ref[:N_HEADS, :], xn) + bcol_ref[...])
    qb_ref[...] = (q * (HEAD_DIM ** -0.5)).astype(bf16)
    kb_ref[...] = k.astype(bf16)
    vb_ref[...] = v.astype(bf16)


def _attn_proj(x, g, wt, bcol, brow, seg, exp, gq, gk, *, rows, row0):
    blk = lambda dt: jax.ShapeDtypeStruct((rows, D_MODEL), dt)
    full = lambda shape: pl.BlockSpec(shape, lambda i: (0, 0))
    return pl.pallas_call(
        _proj_kernel, grid=(1,),
        in_specs=[pl.BlockSpec((rows, D_MODEL), lambda i: (row0 // rows, 0)),
                  _const_spec((1, D_MODEL)), _const_spec((F_ROWS + 3 * D_MODEL, D_MODEL)),
                  _const_spec((N_HEADS, 1)), _const_spec((1, N_HEADS)),
                  _const_spec((D_MODEL, LANES)), _const_spec((2 * LANES, D_MODEL)),
                  _const_spec((1, D_MODEL)), _const_spec((1, D_MODEL))],
        out_specs=[full((rows, D_MODEL)), full((rows, D_MODEL)), full((rows, N_HEADS)), full((N_HEADS, rows)),
                   full((rows, D_MODEL)), full((rows, D_MODEL)), full((rows, D_MODEL))],
        out_shape=[blk(f32), blk(f32), jax.ShapeDtypeStruct((rows, N_HEADS), f32),
                   jax.ShapeDtypeStruct((N_HEADS, rows), f32), blk(bf16), blk(bf16), blk(bf16)],
        name="attn_proj", compiler_params=_params(1))(x, g, wt, bcol, brow, seg, exp, gq, gk)


def _proj_prompt_kernel(x_ref, g_ref, wt_ref, bcol_ref, gqc_ref, gkc_ref, triu_ref,
                        kt_ref, vt_ref, lft_ref, ka_ref, qat_ref, vtb_ref, ccol_ref):
    t = pl.program_id(1)
    tt = x_ref.shape[0]
    n_tiles = tt // ATT_TILE

    @pl.when(t == 0)
    def _():
        ccol_ref[...] = jnp.zeros(ccol_ref.shape, f32)

    xn = _rms(x_ref[...], g_ref[...]).astype(bf16)
    fk = _dot_nt(wt_ref[:F_ROWS + D_MODEL, :], xn)
    qt = _dot_nt(wt_ref[F_ROWS + D_MODEL:F_ROWS + 2 * D_MODEL, :], xn)
    kt = fk[F_ROWS:]

    lft = _log_sigmoid(fk[:N_HEADS] + bcol_ref[...])
    lft_ref[0] = lft
    ct3 = _dot(jnp.concatenate(_split3(lft), axis=0), triu_ref[...])
    ct = ct3[:N_HEADS] + ct3[N_HEADS:2 * N_HEADS] + ct3[2 * N_HEADS:] + ccol_ref[:, 0:1]
    ccol_ref[...] = jnp.broadcast_to(ct[:, tt - 1:tt], ccol_ref.shape)
    ct_pieces = [piece.astype(f32) for piece in _split3(ct * LOG2_E)]

    vt = _dot_nt(wt_ref[F_ROWS + 2 * D_MODEL:, :], xn)
    vt_ref[0] = vt
    vtb = vt.astype(bf16)
    ones_rows = jnp.where(lax.broadcasted_iota(jnp.int32, (V_ROWS - HEAD_DIM, ATT_TILE), 0) == 0, 1.0, 0.0).astype(bf16)
    for j in range(n_tiles):
        for h in range(N_HEADS):
            vtb_ref[0, j, h * V_ROWS:h * V_ROWS + HEAD_DIM, :] = (
                vtb[h * HEAD_DIM:(h + 1) * HEAD_DIM, j * ATT_TILE:(j + 1) * ATT_TILE])
            vtb_ref[0, j, h * V_ROWS + HEAD_DIM:(h + 1) * V_ROWS, :] = ones_rows

    sub = lax.broadcasted_iota(jnp.int32, (SUBLANES, 1), 0)
    ones = jnp.where(sub < 2 * N_AUG, 1.0, 0.0) * jnp.ones((1, tt), f32)
    pad = jnp.zeros((LANES - HEAD_DIM - SUBLANES, tt), f32)
    gqc = gqc_ref[...]
    gkc = gkc_ref[...]

    def head_rms(th, gain):
        return th * lax.rsqrt(jnp.mean(th * th, axis=0, keepdims=True) + NORM_EPS) * gain

    for h in range(N_HEADS):
        rows = slice(h * HEAD_DIM, (h + 1) * HEAD_DIM)
        kn = head_rms(kt[rows, :], gkc)
        kt_ref[0, rows, :] = kn
        aug_k, aug_q = ones, ones
        for n in range(N_AUG):
            piece = ct_pieces[n][h:h + 1, :]
            aug_k = jnp.where(sub == N_AUG + n, -piece, aug_k)
            aug_q = jnp.where(sub == n, piece, aug_q)
        ka_ref[0, h] = jnp.concatenate([kn, aug_k, pad], axis=0).T.astype(bf16)
        blk = jnp.concatenate([head_rms(qt[rows, :], gqc) * (LOG2_E * HEAD_DIM ** -0.5), aug_q, pad],
                              axis=0).astype(bf16)
        for j in range(n_tiles):
            qat_ref[0, j, h * LANES:(h + 1) * LANES, :] = blk[:, j * ATT_TILE:(j + 1) * ATT_TILE]


def _attn_proj_prompt(x, g, wt, bcol, gqc, gkc, triu, *, b, t):
    tt = ROW_TILE
    nt = tt // ATT_TILE
    time_minor = pl.BlockSpec((1, D_MODEL, tt), lambda i, j: (i, 0, j))
    big = jax.ShapeDtypeStruct((b, D_MODEL, t), f32)
    return pl.pallas_call(
        _proj_prompt_kernel, grid=(b, t // tt),
        in_specs=[pl.BlockSpec((tt, D_MODEL), lambda i, j: (i * (t // tt) + j, 0)),
                  _const_spec((1, D_MODEL)), _const_spec((F_ROWS + 3 * D_MODEL, D_MODEL)),
                  _const_spec((N_HEADS, 1)), _const_spec((HEAD_DIM, tt)), _const_spec((HEAD_DIM, tt)),
                  _const_spec((tt, tt))],
        out_specs=[time_minor, time_minor,
                   pl.BlockSpec((1, N_HEADS, tt), lambda i, j: (i, 0, j)),
                   pl.BlockSpec((1, N_HEADS, tt, LANES), lambda i, j: (i, 0, j, 0)),
                   pl.BlockSpec((1, nt, N_HEADS * LANES, ATT_TILE), lambda i, j: (i, j, 0, 0)),
                   pl.BlockSpec((1, nt, N_HEADS * V_ROWS, ATT_TILE), lambda i, j: (i, j, 0, 0))],
        out_shape=[big, big, jax.ShapeDtypeStruct((b, N_HEADS, t), f32),
                   jax.ShapeDtypeStruct((b, N_HEADS, t, LANES), bf16),
                   jax.ShapeDtypeStruct((b, t // ATT_TILE, N_HEADS * LANES, ATT_TILE), bf16),
                   jax.ShapeDtypeStruct((b, t // ATT_TILE, N_HEADS * V_ROWS, ATT_TILE), bf16)],
        scratch_shapes=[pltpu.VMEM((N_HEADS, LANES), f32)],
        name="attn_proj_prompt", compiler_params=_params(2))(x, g, wt, bcol, gqc, gkc, triu)


def _scan_kernel(x_ref, tri_ref, o_ref):
    rows = x_ref.shape[0]
    n = x_ref.shape[1] // SCAN_TILE
    carry = jnp.zeros((rows, 1), f32)
    for c in range(n):
        sl = slice(c * SCAN_TILE, (c + 1) * SCAN_TILE)
        cs = _dot(jnp.concatenate(_split3(x_ref[:, sl]), axis=0), tri_ref[...])
        out = cs[:rows] + cs[rows:2 * rows] + cs[2 * rows:] + carry
        o_ref[:, sl] = out
        carry = out[:, SCAN_TILE - 1:SCAN_TILE]


def _cumsum_lanes(lft, tri):
    b, h, t = lft.shape
    blk = pl.BlockSpec((b * h, t), lambda i: (0, 0))
    return pl.pallas_call(
        _scan_kernel, grid=(1,), in_specs=[blk, _const_spec((SCAN_TILE, SCAN_TILE))], out_specs=blk,
        out_shape=jax.ShapeDtypeStruct((b * h, t), f32), name="cumsum_lanes",
        compiler_params=_params(1))(lft.reshape(b * h, t), tri).reshape(b, h, t)


def _attn_kernel(qat_ref, qnext_ref, ka_ref, vt_ref, o_ref, m_ref, acc_ref, sa_ref, sb_ref):
    i = pl.program_id(2)
    tq = qat_ref.shape[3]
    tk = vt_ref.shape[3]
    causal = (lax.broadcasted_iota(jnp.int32, (tk, tq), 0) <= lax.broadcasted_iota(jnp.int32, (tk, tq), 1))
    m_ref[...] = jnp.full(m_ref.shape, NEG_INF, f32)
    acc_ref[...] = jnp.zeros(acc_ref.shape, f32)

    def scores(q_ref, j, s, buf):
        rows = pl.ds(pl.multiple_of(j * tk, tk), tk)
        buf[s] = _dot(ka_ref[0, s, rows, :], q_ref[0, 0, s * LANES:(s + 1) * LANES, :])

    def consume(j, s, buf, masked):
        sc = buf[s]
        if masked:
            sc = jnp.where(causal, sc, NEG_INF)
        m = m_ref[s]
        m_new = jnp.maximum(m, jnp.max(sc, axis=0, keepdims=True))
        alpha = jnp.exp2(m - m_new)
        pr = jnp.exp2(sc - m_new)
        vt = vt_ref[0, j, s * V_ROWS:(s + 1) * V_ROWS, :]
        acc_ref[s] = alpha * acc_ref[s] + _dot(vt, pr.astype(bf16))
        m_ref[s] = m_new

    heads = range(ATT_HEADS)

    def step(j, cur, nxt):
        for s in heads:
            scores(qat_ref, j + 1, s, nxt)
            consume(j, s, cur, False)

    def last_step(cur, nxt):
        for s in heads:
            scores(qnext_ref, 0, s, nxt)
            consume(i, s, cur, True)

    @pl.when(i == 0)
    def _():
        for s in heads:
            scores(qat_ref, 0, s, sa_ref)

    def sweep(a, b):
        def two_steps(t, carry):
            step(2 * t, a, b)
            step(2 * t + 1, b, a)
            return carry

        lax.fori_loop(0, i // 2, two_steps, 0)

        @pl.when(i % 2 == 0)
        def _():
            last_step(a, b)

        @pl.when(i % 2 == 1)
        def _():
            step(i - 1, a, b)
            last_step(b, a)

    first_in_b = ((i * (i + 1)) // 2) % 2

    @pl.when(first_in_b == 0)
    def _():
        sweep(sa_ref, sb_ref)

    @pl.when(first_in_b == 1)
    def _():
        sweep(sb_ref, sa_ref)

    o_t = jnp.concatenate([acc_ref[s, :HEAD_DIM, :] / acc_ref[s, HEAD_DIM:HEAD_DIM + 1, :]
                           for s in range(ATT_HEADS)], axis=0)
    o_ref[0] = o_t.T.astype(bf16)


def _attn_prompt(qat, ka, vt):
    b, nq, _, tq = qat.shape
    s = nq * tq
    g = ATT_HEADS
    return pl.pallas_call(
        _attn_kernel, grid=(b, N_HEADS // g, nq),
        in_specs=[pl.BlockSpec((1, 1, g * LANES, tq), lambda i, h, j: (i, j, h, 0)),
                  pl.BlockSpec((1, 1, g * LANES, tq), lambda i, h, j: (i, jnp.minimum(j + 1, nq - 1), h, 0)),
                  pl.BlockSpec((1, g, s, LANES), lambda i, h, j: (i, h, 0, 0)),
                  pl.BlockSpec((1, nq, g * V_ROWS, tq), lambda i, h, j: (i, 0, h, 0))],
        out_specs=pl.BlockSpec((1, tq, g * HEAD_DIM), lambda i, h, j: (i, j, h)),
        out_shape=jax.ShapeDtypeStruct((b, s, D_MODEL), bf16),
        scratch_shapes=[pltpu.VMEM((g, 1, tq), f32), pltpu.VMEM((g, V_ROWS, tq), f32),
                        pltpu.VMEM((g, tq, tq), f32), pltpu.VMEM((g, tq, tq), f32)],
        name="attn_prompt",
        compiler_params=_params(3))(qat, qat, ka, vt)


def _attn_sample_kernel(q_ref, kn_ref, vn_ref, kp_ref, vp_ref, cq_ref, ck_ref, o_ref):
    t = q_ref.shape[1]
    past = kp_ref.shape[2]
    low = lax.broadcasted_iota(jnp.int32, (1, LANES), 1) < HEAD_DIM
    causal = (lax.broadcasted_iota(jnp.int32, (t, t), 1) <= lax.broadcasted_iota(jnp.int32, (t, t), 0))
    n_pairs = q_ref.shape[2] // LANES

    def pair_scores(hp):
        lanes = slice(hp * LANES, (hp + 1) * LANES)
        q2 = q_ref[0, :, lanes]
        kp2 = kp_ref[0, lanes, :].astype(bf16)
        kn2 = kn_ref[0, :, lanes]
        out = []
        for sub in range(2):
            qm = jnp.where(low if sub == 0 else jnp.logical_not(low), q2, jnp.zeros_like(q2))
            out.append((_dot(qm, kp2), _dot_nt(qm, kn2)))
        return out

    ahead = pair_scores(0)
    for hp in range(n_pairs):
        raw = ahead
        if hp + 1 < n_pairs:
            ahead = pair_scores(hp + 1)
        lanes = slice(hp * LANES, (hp + 1) * LANES)
        vp2 = vp_ref[0, lanes, :].astype(bf16)
        vn2 = vn_ref[0, :, lanes]
        halves = []
        for sub in range(2):
            h = 2 * hp + sub
            cq = cq_ref[0, 0, :, h:h + 1]
            s_past = raw[sub][0] + cq - ck_ref[0, h:h + 1, :past]
            s_new = raw[sub][1] + cq - ck_ref[0, h:h + 1, past:past + t]
            s_new = jnp.where(causal, s_new, NEG_INF)
            m = jnp.maximum(jnp.max(s_past, axis=1, keepdims=True), jnp.max(s_new, axis=1, keepdims=True))
            p_past = jnp.exp(s_past - m)
            p_new = jnp.exp(s_new - m)
            l = jnp.sum(p_past, axis=1, keepdims=True) + jnp.sum(p_new, axis=1, keepdims=True)
            acc = _dot_nt(p_past.astype(bf16), vp2) + _dot(p_new.astype(bf16), vn2)
            halves.append(acc / l)
        o_ref[0, :, lanes] = jnp.where(low, halves[0], halves[1]).astype(bf16)


def _attn_sample(qb, kb, vb, past_kt, past_vt, c_nat, c_t):
    b, t, _ = qb.shape
    past = past_kt.shape[2]
    g = SAMPLE_HEADS
    width = g * HEAD_DIM
    new = pl.BlockSpec((1, t, width), lambda i, h: (i, 0, h))
    old = pl.BlockSpec((1, width, past), lambda i, h: (i, h, 0))
    return pl.pallas_call(
        _attn_sample_kernel, grid=(b, N_HEADS // g),
        in_specs=[new, new, new, old, old,
                  pl.BlockSpec((1, 1, t, g), lambda i, h: (i, h, 0, 0)),
                  pl.BlockSpec((1, g, c_t.shape[2]), lambda i, h: (i, h, 0))],
        out_specs=new,
        out_shape=jax.ShapeDtypeStruct((b, t, D_MODEL), bf16),
        name="attn_sample", compiler_params=_params(2))(qb, kb, vb, past_kt, past_vt, c_nat, c_t)


def kernel(x_prompt, x_sample, state_conv, cache_k, cache_v, cache_logf, norm_ffn, ffn_w_in, ffn_w_out, norm_mix, conv_w_in, conv_w, conv_w_out, attn_w_in, attn_b_f, q_norm, k_norm, attn_w_out):
    bp, sp, d = x_prompt.shape
    bs, ss, _ = x_sample.shape
    past = cache_k.shape[2]

    n_p, n_s = bp * sp, bs * ss
    def ffn(xs, i, j, w, **kw):
        return _ffn(xs, norm_ffn[i, j].reshape(1, d), *w, **kw)

    w_slabs, wo_slabs = 32, 16
    ffn_casts = lambda i, j: [(ffn_w_in, (i, j), w_slabs), (ffn_w_out, (i, j), wo_slabs)]
    square_cast = lambda w: (w, (0,), w_slabs)

    y, conv_in_b, conv_out_b = ffn((x_prompt.reshape(n_p, d), x_sample.reshape(n_s, d)), 0, 0,
                                   (ffn_w_in[0, 0].astype(bf16), ffn_w_out[0, 0].astype(bf16)),
                                   casts=[square_cast(conv_w_in), square_cast(conv_w_out)])
    cw = (norm_mix[0].reshape(1, d), conv_in_b, conv_w[0], conv_out_b)
    y_mixed, conv_p, *w01 = _conv_mixer(y, bp, *cw, t=sp, casts=ffn_casts(0, 1))
    y, conv_s = _conv_mixer_sample(y, state_conv[0], *cw, t=ss, row0=n_p, into=y_mixed)
    y, *w10 = ffn(y, 0, 1, w01, casts=ffn_casts(1, 0))

    y, *w11, attn_out_b = ffn(y, 1, 0, w10, casts=ffn_casts(1, 1) + [square_cast(attn_w_out)])

    w_t = attn_w_in[0].T
    wt = jnp.concatenate([w_t[3 * d:], jnp.zeros((F_ROWS - N_HEADS, d), f32), w_t[d:2 * d], w_t[:d], w_t[2 * d:3 * d]],
                         axis=0).astype(bf16)
    head_of = jnp.arange(d) // HEAD_DIM
    seg = ((head_of[:, None] == jnp.arange(LANES)[None, :]).astype(f32) * (1.0 / HEAD_DIM)).astype(bf16)
    expand = (jnp.arange(LANES)[:, None] == head_of[None, :]).astype(bf16)
    expand2 = jnp.concatenate([expand, expand], axis=0)
    g_mix = norm_mix[1].reshape(1, d)
    bcol = attn_b_f[0].reshape(N_HEADS, 1)
    gk = jnp.tile(k_norm[0], N_HEADS).reshape(1, d)

    per_channel = lambda gain: jnp.broadcast_to(gain[:, None], (HEAD_DIM, ROW_TILE))
    kpt, vpt, fpt, ka, qat, vt = _attn_proj_prompt(
        y, g_mix, wt, bcol, per_channel(q_norm[0]), per_channel(k_norm[0]),
        jnp.triu(jnp.ones((ROW_TILE, ROW_TILE), f32)).astype(bf16), b=bp, t=sp)
    ap = _attn_prompt(qat, ka, vt)

    kq, vq, fq, fqt, qsb, ksb, vsb = _attn_proj(
        y, g_mix, wt, bcol, attn_b_f[0].reshape(1, N_HEADS), seg, expand2, jnp.tile(q_norm[0], N_HEADS).reshape(1, d), gk,
        rows=n_s, row0=n_p)
    pad = (-(past + ss)) % SCAN_TILE
    lf_all = jnp.concatenate([jnp.swapaxes(cache_logf[0], 1, 2),
                              jnp.swapaxes(fqt.reshape(N_HEADS, bs, ss), 0, 1),
                              jnp.zeros((bs, N_HEADS, pad), f32)], axis=2)
    cst = _cumsum_lanes(lf_all, jnp.triu(jnp.ones((SCAN_TILE, SCAN_TILE), f32)).astype(bf16))
    per_stream = lambda a: a.reshape(bs, ss, d)
    channel_major = lambda a: jnp.transpose(a, (0, 2, 3, 1)).reshape(bs, d, past)
    a_s = _attn_sample(per_stream(qsb), per_stream(ksb), per_stream(vsb),
                       channel_major(cache_k[0]), channel_major(cache_v[0]),
                       jnp.swapaxes(cst[:, :, past:past + ss].reshape(bs, -1, SAMPLE_HEADS, ss), 2, 3), cst)

    yp, ys = ffn(y, 1, 1, w11, mix=(ap.reshape(n_p, d), a_s.reshape(n_s, d), attn_out_b))

    hd = (N_HEADS, HEAD_DIM)
    token_major = lambda a: jnp.transpose(a.reshape(bp, *hd, sp), (0, 3, 1, 2))[None]
    return (yp.reshape(bp, sp, d), ys.reshape(bs, ss, d),
            conv_p[None], conv_s[None],
            token_major(kpt), token_major(vpt), jnp.swapaxes(fpt, 1, 2)[None],
            kq.reshape(1, bs, ss, *hd), vq.reshape(1, bs, ss, *hd), fq.reshape(1, bs, ss, N_HEADS))
```

```python
import functools

import jax
import jax.numpy as jnp
from jax import lax
from jax.experimental import pallas as pl
from jax.experimental.pallas import tpu as pltpu

D_MODEL = 1024
N_HEADS = 16
HEAD_DIM = 64
D_FF = 2816
CONV_WIDTH = 3
NORM_EPS = 1e-6
NEG_INF = -1e30
FFN_RESIDUAL = 0.5

LANES = 128
F_ROWS = 128
SUBLANES = 8
VMEM_LIMIT = 56 * 1024 * 1024
FF_CHUNKS = ((0, 1536), (1536, 2816))
ROW_TILE = 512
CONV_TILE = 1024
ATT_TILE = 256
ATT_HEADS = 16
SAMPLE_HEADS = 8
SCAN_TILE = 256
N_AUG = 3
V_ROWS = HEAD_DIM + 16
LOG2_E = 1.4426950408889634

f32 = jnp.float32
bf16 = jnp.bfloat16


def _dot(a, b):
    return jnp.dot(a, b, preferred_element_type=f32)


def _dot_nt(a, b):
    return lax.dot_general(a, b, (((1,), (1,)), ((), ())), preferred_element_type=f32)


def _rms(x, g):
    return x * lax.rsqrt(jnp.mean(x * x, axis=-1, keepdims=True) + NORM_EPS) * g


def _log_sigmoid(z):
    return -(jnp.maximum(-z, 0.0) + jnp.log1p(jnp.exp(-jnp.abs(z))))


def _split3(x):
    hi = x.astype(bf16)
    r1 = x - hi.astype(f32)
    mid = r1.astype(bf16)
    lo = (r1 - mid.astype(f32)).astype(bf16)
    return hi, mid, lo


def _params(n_axes):
    return pltpu.CompilerParams(dimension_semantics=("arbitrary",) * n_axes, vmem_limit_bytes=VMEM_LIMIT)


def _const_spec(shape):
    return pl.BlockSpec(shape, lambda *_: (0,) * len(shape), pipeline_mode=pl.Buffered(1))


def _cast_job(src, lead, steps, lin):
    rows, cols = src.shape[-2:]
    slab = rows // steps
    assert slab * steps == rows and slab % 16 == 0
    pos = lambda *idx: jnp.minimum(lin(*idx), steps - 1)
    in_spec = pl.BlockSpec((None,) * len(lead) + (slab, cols), lambda *idx: (*lead, pos(*idx), 0))
    out_spec = pl.BlockSpec((slab, cols), lambda *idx: (pos(*idx), 0))
    return src, in_spec, out_spec, jax.ShapeDtypeStruct((rows, cols), bf16)


def _run_casts(srcs, dsts):
    for src_ref, dst_ref in zip(srcs, dsts):
        dst_ref[...] = src_ref[...].astype(bf16)


def _ffn_kernel(*refs, n_prompt, split_in, mix, n_jobs):
    refs = list(refs)
    cast_srcs = cast_dsts = ()
    if n_jobs:
        cast_dsts = refs[-n_jobs:]
        del refs[-n_jobs:]
        n_out = 2 if mix else 1
        cast_srcs = refs[-n_out - n_jobs:-n_out]
        del refs[-n_out - n_jobs:-n_out]
    is_prompt = pl.program_id(0) < n_prompt

    def rows():
        if split_in or mix:
            p_ref, s_ref = refs.pop(0), refs.pop(0)
            return jnp.where(is_prompt, p_ref[...], s_ref[...])
        return refs.pop(0)[...]

    if mix:
        x = refs.pop(0)[...]
        a = rows()
        x = x + _dot(a, refs.pop(0)[...])
    else:
        x = rows()
    g_ref, win_ref, wout_ref = refs[:3]
    outs = refs[3:]
    xn = _rms(x, g_ref[...]).astype(bf16)
    acc = None
    for lo, hi in FF_CHUNKS:
        a = _dot(xn, win_ref[:, lo:hi])
        b = _dot(xn, win_ref[:, D_FF + lo:D_FF + hi])
        if lo == 0:
            _run_casts(cast_srcs, cast_dsts)
        h = (a * jax.nn.sigmoid(a) * b).astype(bf16)
        y = _dot(h, wout_ref[lo:hi, :])
        acc = y if acc is None else acc + y
    y = x + FFN_RESIDUAL * acc
    if mix:
        @pl.when(is_prompt)
        def _():
            outs[0][...] = y

        @pl.when(jnp.logical_not(is_prompt))
        def _():
            outs[1][...] = y
    else:
        outs[0][...] = y


def _ffn(xs, g, w_in, w_out, mix=None, casts=()):
    tm = ROW_TILE
    split_in = isinstance(xs, tuple)
    if mix is None and split_in:
        n_prompt, n_sample = xs[0].shape[0] // tm, xs[1].shape[0] // tm
    elif mix is not None:
        n_prompt, n_sample = mix[0].shape[0] // tm, mix[1].shape[0] // tm
    else:
        n_prompt, n_sample = xs.shape[0] // tm, 0
    assert n_sample in (0, 1)
    steps = n_prompt + n_sample
    row = pl.BlockSpec((tm, D_MODEL), lambda i: (i, 0))
    prompt_row = pl.BlockSpec((tm, D_MODEL), lambda i: (jnp.minimum(i, n_prompt - 1), 0))
    sample_row = pl.BlockSpec((tm, D_MODEL), lambda i: (0, 0))
    w_specs = [_const_spec((1, D_MODEL)), _const_spec((D_MODEL, 2 * D_FF)), _const_spec((D_FF, D_MODEL))]
    stacked = jax.ShapeDtypeStruct((steps * tm, D_MODEL), f32)
    if mix is not None:
        a_p, a_s, wm = mix
        ins = (xs, a_p, a_s, wm)
        specs = [row, prompt_row, sample_row, _const_spec((D_MODEL, D_MODEL))]
        out_specs = [prompt_row, sample_row]
        out_shape = [jax.ShapeDtypeStruct((n_prompt * tm, D_MODEL), f32),
                     jax.ShapeDtypeStruct((n_sample * tm, D_MODEL), f32)]
    elif split_in:
        ins, specs, out_specs, out_shape = tuple(xs), [prompt_row, sample_row], [row], [stacked]
    else:
        ins, specs, out_specs, out_shape = (xs,), [row], [row], [stacked]
    jobs = [_cast_job(src, lead, n, lambda i: i) for src, lead, n in casts]
    kern = functools.partial(_ffn_kernel, n_prompt=n_prompt, split_in=split_in, mix=mix is not None,
                             n_jobs=len(jobs))
    return pl.pallas_call(
        kern, grid=(steps,), in_specs=specs + w_specs + [j[1] for j in jobs],
        out_specs=out_specs + [j[2] for j in jobs], out_shape=out_shape + [j[3] for j in jobs],
        name="ffn" if mix is None else "mix_ffn",
        compiler_params=_params(1))(*ins, g, w_in, w_out, *[j[0] for j in jobs])


def _conv_body(x, g_ref, win_ref, wk_ref, wout_ref, shifted, between=lambda: None):
    xn = _rms(x, g_ref[...]).astype(bf16)
    ch = _dot(xn, win_ref[:, D_MODEL:])
    gate_b = _dot(xn, win_ref[:, :D_MODEL])
    between()
    u = ch[:, :D_MODEL] * ch[:, D_MODEL:]
    u1, u2 = shifted(u)
    wk = wk_ref[...]
    conv = wk[0:1, :] * u2 + wk[1:2, :] * u1 + wk[2:3, :] * u
    return x + _dot((gate_b * conv).astype(bf16), wout_ref[...]), u


def _conv_kernel(x_ref, g_ref, win_ref, wk_ref, wout_ref, *rest, n_jobs):
    o_ref, st_ref = rest[n_jobs:n_jobs + 2]
    carry_ref = rest[-1]
    tt = x_ref.shape[0]

    @pl.when(pl.program_id(1) == 0)
    def _():
        carry_ref[...] = jnp.zeros(carry_ref.shape, f32)

    def shifted(u):
        prev2 = carry_ref[0:1, :]
        prev1 = carry_ref[1:2, :]
        row = lax.broadcasted_iota(jnp.int32, (tt, 1), 0)
        return (jnp.where(row == 0, prev1, pltpu.roll(u, 1, 0)),
                jnp.where(row == 0, prev2, jnp.where(row == 1, prev1, pltpu.roll(u, 2, 0))))

    o_ref[...], u = _conv_body(x_ref[...], g_ref, win_ref, wk_ref, wout_ref, shifted,
                               lambda: _run_casts(rest[:n_jobs], rest[n_jobs + 2:-1]))
    last = u[tt - 2:tt, :]
    carry_ref[0:2, :] = last
    st_ref[0] = last


def _conv_mixer(x, n_streams, g, win, wk, wout, *, t, casts=()):
    tt = CONV_TILE
    nt = t // tt
    blk = pl.BlockSpec((tt, D_MODEL), lambda i, j: (i * nt + j, 0))
    st = pl.BlockSpec((1, CONV_WIDTH - 1, D_MODEL), lambda i, j: (i, 0, 0))
    specs = [blk, _const_spec((1, D_MODEL)), _const_spec((D_MODEL, 3 * D_MODEL)),
             _const_spec((CONV_WIDTH, D_MODEL)), _const_spec((D_MODEL, D_MODEL))]
    jobs = [_cast_job(src, lead, n, lambda i, j: i * nt + j) for src, lead, n in casts]
    return pl.pallas_call(
        functools.partial(_conv_kernel, n_jobs=len(jobs)), grid=(n_streams, nt),
        in_specs=specs + [j[1] for j in jobs], out_specs=[blk, st] + [j[2] for j in jobs],
        out_shape=[jax.ShapeDtypeStruct(x.shape, f32),
                   jax.ShapeDtypeStruct((n_streams, CONV_WIDTH - 1, D_MODEL), f32)] + [j[3] for j in jobs],
        scratch_shapes=[pltpu.VMEM((8, D_MODEL), f32)],
        name="conv_mixer", compiler_params=_params(2))(x, g, win, wk, wout, *[j[0] for j in jobs])


def _conv_sample_kernel(x_ref, h1_ref, h2_ref, g_ref, win_ref, wk_ref, wout_ref, into_ref, o_ref, u_ref, *, t):
    rows = x_ref.shape[0]

    def shifted(u):
        pos = lax.rem(lax.broadcasted_iota(jnp.int32, (rows, 1), 0), t)
        h1 = h1_ref[...]
        return (jnp.where(pos == 0, h1, pltpu.roll(u, 1, 0)),
                jnp.where(pos == 0, h2_ref[...], jnp.where(pos == 1, h1, pltpu.roll(u, 2, 0))))

    o_ref[...], u_ref[...] = _conv_body(x_ref[...], g_ref, win_ref, wk_ref, wout_ref, shifted)


def _conv_mixer_sample(x, state, g, win, wk, wout, *, t, row0, into):
    n = state.shape[0]
    rows = n * t
    blk = pl.BlockSpec((rows, D_MODEL), lambda i: (row0 // rows, 0))
    hist = pl.BlockSpec((rows, D_MODEL), lambda i: (0, 0))
    y, u = pl.pallas_call(
        functools.partial(_conv_sample_kernel, t=t), grid=(1,),
        in_specs=[blk, hist, hist, _const_spec((1, D_MODEL)), _const_spec((D_MODEL, 3 * D_MODEL)),
                  _const_spec((CONV_WIDTH, D_MODEL)), _const_spec((D_MODEL, D_MODEL)),
                  pl.BlockSpec(memory_space=pl.ANY)],
        out_specs=[blk, hist],
        out_shape=[jax.ShapeDtypeStruct(x.shape, f32), jax.ShapeDtypeStruct((rows, D_MODEL), f32)],
        input_output_aliases={7: 0},
        name="conv_mixer_sample", compiler_params=_params(1))(
            x, jnp.repeat(state[:, 1], t, axis=0), jnp.repeat(state[:, 0], t, axis=0), g, win, wk, wout, into)
    return y, u.reshape(n, t, D_MODEL)[:, t - (CONV_WIDTH - 1):]


def _head_norm(t, gain, seg_ref, exp_ref):
    ms = _dot((t * t).astype(bf16), seg_ref[...])
    r = lax.rsqrt(ms + NORM_EPS)
    r_hi = r.astype(bf16)
    r_lo = (r - r_hi.astype(f32)).astype(bf16)
    rb = _dot(jnp.concatenate([r_hi, r_lo], axis=1), exp_ref[...])
    return t * rb * gain


def _proj_kernel(x_ref, g_ref, wt_ref, bcol_ref, brow_ref, seg_ref, exp_ref, gq_ref, gk_ref,
                 k_ref, v_ref, lf_ref, lft_ref, qb_ref, kb_ref, vb_ref):
    xn = _rms(x_ref[...], g_ref[...]).astype(bf16)
    p = _dot_nt(xn, wt_ref[...])
    k = _head_norm(p[:, F_ROWS:F_ROWS + D_MODEL], gk_ref[...], seg_ref, exp_ref)
    q = _head_norm(p[:, F_ROWS + D_MODEL:F_ROWS + 2 * D_MODEL], gq_ref[...], seg_ref, exp_ref)
    v = p[:, F_ROWS + 2 * D_MODEL:]
    k_ref[...] = k
    v_ref[...] = v
    lf_ref[...] = _log_sigmoid(p[:, :N_HEADS] + brow_ref[...])
    lft_ref[...] = _log_sigmoid(_dot_nt(wt_ref[:N_HEADS, :], xn) + bcol_ref[...])
    qb_ref[...] = (q * (HEAD_DIM ** -0.5)).astype(bf16)
    kb_ref[...] = k.astype(bf16)
    vb_ref[...] = v.astype(bf16)


def _attn_proj(x, g, wt, bcol, brow, seg, exp, gq, gk, *, rows, row0):
    blk = lambda dt: jax.ShapeDtypeStruct((rows, D_MODEL), dt)
    full = lambda shape: pl.BlockSpec(shape, lambda i: (0, 0))
    return pl.pallas_call(
        _proj_kernel, grid=(1,),
        in_specs=[pl.BlockSpec((rows, D_MODEL), lambda i: (row0 // rows, 0)),
                  _const_spec((1, D_MODEL)), _const_spec((F_ROWS + 3 * D_MODEL, D_MODEL)),
                  _const_spec((N_HEADS, 1)), _const_spec((1, N_HEADS)),
                  _const_spec((D_MODEL, LANES)), _const_spec((2 * LANES, D_MODEL)),
                  _const_spec((1, D_MODEL)), _const_spec((1, D_MODEL))],
        out_specs=[full((rows, D_MODEL)), full((rows, D_MODEL)), full((rows, N_HEADS)), full((N_HEADS, rows)),
                   full((rows, D_MODEL)), full((rows, D_MODEL)), full((rows, D_MODEL))],
        out_shape=[blk(f32), blk(f32), jax.ShapeDtypeStruct((rows, N_HEADS), f32),
                   jax.ShapeDtypeStruct((N_HEADS, rows), f32), blk(bf16), blk(bf16), blk(bf16)],
        name="attn_proj", compiler_params=_params(1))(x, g, wt, bcol, brow, seg, exp, gq, gk)


def _proj_prompt_kernel(x_ref, g_ref, wt_ref, bcol_ref, gqc_ref, gkc_ref, triu_ref,
                        kt_ref, vt_ref, lft_ref, ka_ref, qat_ref, vtb_ref, ccol_ref):
    t = pl.program_id(1)
    tt = x_ref.shape[0]
    n_tiles = tt // ATT_TILE

    @pl.when(t == 0)
    def _():
        ccol_ref[...] = jnp.zeros(ccol_ref.shape, f32)

    xn = _rms(x_ref[...], g_ref[...]).astype(bf16)
    fk = _dot_nt(wt_ref[:F_ROWS + D_MODEL, :], xn)
    qt = _dot_nt(wt_ref[F_ROWS + D_MODEL:F_ROWS + 2 * D_MODEL, :], xn)
    kt = fk[F_ROWS:]

    lft = _log_sigmoid(fk[:N_HEADS] + bcol_ref[...])
    lft_ref[0] = lft
    ct3 = _dot(jnp.concatenate(_split3(lft), axis=0), triu_ref[...])
    ct = ct3[:N_HEADS] + ct3[N_HEADS:2 * N_HEADS] + ct3[2 * N_HEADS:] + ccol_ref[:, 0:1]
    ccol_ref[...] = jnp.broadcast_to(ct[:, tt - 1:tt], ccol_ref.shape)
    ct_pieces = [piece.astype(f32) for piece in _split3(ct * LOG2_E)]

    vt = _dot_nt(wt_ref[F_ROWS + 2 * D_MODEL:, :], xn)
    vt_ref[0] = vt
    vtb = vt.astype(bf16)
    ones_rows = jnp.where(lax.broadcasted_iota(jnp.int32, (V_ROWS - HEAD_DIM, ATT_TILE), 0) == 0, 1.0, 0.0).astype(bf16)
    for j in range(n_tiles):
        for h in range(N_HEADS):
            vtb_ref[0, j, h * V_ROWS:h * V_ROWS + HEAD_DIM, :] = (
                vtb[h * HEAD_DIM:(h + 1) * HEAD_DIM, j * ATT_TILE:(j + 1) * ATT_TILE])
            vtb_ref[0, j, h * V_ROWS + HEAD_DIM:(h + 1) * V_ROWS, :] = ones_rows

    sub = lax.broadcasted_iota(jnp.int32, (SUBLANES, 1), 0)
    ones = jnp.where(sub < 2 * N_AUG, 1.0, 0.0) * jnp.ones((1, tt), f32)
    pad = jnp.zeros((LANES - HEAD_DIM - SUBLANES, tt), f32)
    gqc = gqc_ref[...]
    gkc = gkc_ref[...]

    def head_rms(th, gain):
        return th * lax.rsqrt(jnp.mean(th * th, axis=0, keepdims=True) + NORM_EPS) * gain

    for h in range(N_HEADS):
        rows = slice(h * HEAD_DIM, (h + 1) * HEAD_DIM)
        kn = head_rms(kt[rows, :], gkc)
        kt_ref[0, rows, :] = kn
        aug_k, aug_q = ones, ones
        for n in range(N_AUG):
            piece = ct_pieces[n][h:h + 1, :]
            aug_k = jnp.where(sub == N_AUG + n, -piece, aug_k)
            aug_q = jnp.where(sub == n, piece, aug_q)
        ka_ref[0, h] = jnp.concatenate([kn, aug_k, pad], axis=0).T.astype(bf16)
        blk = jnp.concatenate([head_rms(qt[rows, :], gqc) * (LOG2_E * HEAD_DIM ** -0.5), aug_q, pad],
                              axis=0).astype(bf16)
        for j in range(n_tiles):
            qat_ref[0, j, h * LANES:(h + 1) * LANES, :] = blk[:, j * ATT_TILE:(j + 1) * ATT_TILE]


def _attn_proj_prompt(x, g, wt, bcol, gqc, gkc, triu, *, b, t):
    tt = ROW_TILE
    nt = tt // ATT_TILE
    time_minor = pl.BlockSpec((1, D_MODEL, tt), lambda i, j: (i, 0, j))
    big = jax.ShapeDtypeStruct((b, D_MODEL, t), f32)
    return pl.pallas_call(
        _proj_prompt_kernel, grid=(b, t // tt),
        in_specs=[pl.BlockSpec((tt, D_MODEL), lambda i, j: (i * (t // tt) + j, 0)),
                  _const_spec((1, D_MODEL)), _const_spec((F_ROWS + 3 * D_MODEL, D_MODEL)),
                  _const_spec((N_HEADS, 1)), _const_spec((HEAD_DIM, tt)), _const_spec((HEAD_DIM, tt)),
                  _const_spec((tt, tt))],
        out_specs=[time_minor, time_minor,
                   pl.BlockSpec((1, N_HEADS, tt), lambda i, j: (i, 0, j)),
                   pl.BlockSpec((1, N_HEADS, tt, LANES), lambda i, j: (i, 0, j, 0)),
                   pl.BlockSpec((1, nt, N_HEADS * LANES, ATT_TILE), lambda i, j: (i, j, 0, 0)),
                   pl.BlockSpec((1, nt, N_HEADS * V_ROWS, ATT_TILE), lambda i, j: (i, j, 0, 0))],
        out_shape=[big, big, jax.ShapeDtypeStruct((b, N_HEADS, t), f32),
                   jax.ShapeDtypeStruct((b, N_HEADS, t, LANES), bf16),
                   jax.ShapeDtypeStruct((b, t // ATT_TILE, N_HEADS * LANES, ATT_TILE), bf16),
                   jax.ShapeDtypeStruct((b, t // ATT_TILE, N_HEADS * V_ROWS, ATT_TILE), bf16)],
        scratch_shapes=[pltpu.VMEM((N_HEADS, LANES), f32)],
        name="attn_proj_prompt", compiler_params=_params(2))(x, g, wt, bcol, gqc, gkc, triu)


def _scan_kernel(x_ref, tri_ref, o_ref):
    rows = x_ref.shape[0]
    n = x_ref.shape[1] // SCAN_TILE
    carry = jnp.zeros((rows, 1), f32)
    for c in range(n):
        sl = slice(c * SCAN_TILE, (c + 1) * SCAN_TILE)
        cs = _dot(jnp.concatenate(_split3(x_ref[:, sl]), axis=0), tri_ref[...])
        out = cs[:rows] + cs[rows:2 * rows] + cs[2 * rows:] + carry
        o_ref[:, sl] = out
        carry = out[:, SCAN_TILE - 1:SCAN_TILE]


def _cumsum_lanes(lft, tri):
    b, h, t = lft.shape
    blk = pl.BlockSpec((b * h, t), lambda i: (0, 0))
    return pl.pallas_call(
        _scan_kernel, grid=(1,), in_specs=[blk, _const_spec((SCAN_TILE, SCAN_TILE))], out_specs=blk,
        out_shape=jax.ShapeDtypeStruct((b * h, t), f32), name="cumsum_lanes",
        compiler_params=_params(1))(lft.reshape(b * h, t), tri).reshape(b, h, t)


def _attn_kernel(qat_ref, qnext_ref, ka_ref, vt_ref, o_ref, m_ref, acc_ref, sa_ref, sb_ref):
    i = pl.program_id(2)
    tq = qat_ref.shape[3]
    tk = vt_ref.shape[3]
    causal = (lax.broadcasted_iota(jnp.int32, (tk, tq), 0) <= lax.broadcasted_iota(jnp.int32, (tk, tq), 1))
    m_ref[...] = jnp.full(m_ref.shape, NEG_INF, f32)
    acc_ref[...] = jnp.zeros(acc_ref.shape, f32)

    def scores(q_ref, j, s, buf):
        rows = pl.ds(pl.multiple_of(j * tk, tk), tk)
        buf[s] = _dot(ka_ref[0, s, rows, :], q_ref[0, 0, s * LANES:(s + 1) * LANES, :])

    def consume(j, s, buf, masked):
        sc = buf[s]
        if masked:
            sc = jnp.where(causal, sc, NEG_INF)
        m = m_ref[s]
        m_new = jnp.maximum(m, jnp.max(sc, axis=0, keepdims=True))
        alpha = jnp.exp2(m - m_new)
        pr = jnp.exp2(sc - m_new)
        vt = vt_ref[0, j, s * V_ROWS:(s + 1) * V_ROWS, :]
        acc_ref[s] = alpha * acc_ref[s] + _dot(vt, pr.astype(bf16))
        m_ref[s] = m_new

    heads = range(ATT_HEADS)

    def step(j, cur, nxt):
        for s in heads:
            scores(qat_ref, j + 1, s, nxt)
            consume(j, s, cur, False)

    def last_step(cur, nxt):
        for s in heads:
            scores(qnext_ref, 0, s, nxt)
            consume(i, s, cur, True)

    @pl.when(i == 0)
    def _():
        for s in heads:
            scores(qat_ref, 0, s, sa_ref)

    def sweep(a, b):
        def two_steps(t, carry):
            step(2 * t, a, b)
            step(2 * t + 1, b, a)
            return carry

        lax.fori_loop(0, i // 2, two_steps, 0)

        @pl.when(i % 2 == 0)
        def _():
            last_step(a, b)

        @pl.when(i % 2 == 1)
        def _():
            step(i - 1, a, b)
            last_step(b, a)

    first_in_b = ((i * (i + 1)) // 2) % 2

    @pl.when(first_in_b == 0)
    def _():
        sweep(sa_ref, sb_ref)

    @pl.when(first_in_b == 1)
    def _():
        sweep(sb_ref, sa_ref)

    o_t = jnp.concatenate([acc_ref[s, :HEAD_DIM, :] / acc_ref[s, HEAD_DIM:HEAD_DIM + 1, :]
                           for s in range(ATT_HEADS)], axis=0)
    o_ref[0] = o_t.T.astype(bf16)


def _attn_prompt(qat, ka, vt):
    b, nq, _, tq = qat.shape
    s = nq * tq
    g = ATT_HEADS
    return pl.pallas_call(
        _attn_kernel, grid=(b, N_HEADS // g, nq),
        in_specs=[pl.BlockSpec((1, 1, g * LANES, tq), lambda i, h, j: (i, j, h, 0)),
                  pl.BlockSpec((1, 1, g * LANES, tq), lambda i, h, j: (i, jnp.minimum(j + 1, nq - 1), h, 0)),
                  pl.BlockSpec((1, g, s, LANES), lambda i, h, j: (i, h, 0, 0)),
                  pl.BlockSpec((1, nq, g * V_ROWS, tq), lambda i, h, j: (i, 0, h, 0))],
        out_specs=pl.BlockSpec((1, tq, g * HEAD_DIM), lambda i, h, j: (i, j, h)),
        out_shape=jax.ShapeDtypeStruct((b, s, D_MODEL), bf16),
        scratch_shapes=[pltpu.VMEM((g, 1, tq), f32), pltpu.VMEM((g, V_ROWS, tq), f32),
                        pltpu.VMEM((g, tq, tq), f32), pltpu.VMEM((g, tq, tq), f32)],
        name="attn_prompt",
        compiler_params=_params(3))(qat, qat, ka, vt)


def _attn_sample_kernel(q_ref, kn_ref, vn_ref, kp_ref, vp_ref, cq_ref, ck_ref, o_ref):
    t = q_ref.shape[1]
    past = kp_ref.shape[2]
    low = lax.broadcasted_iota(jnp.int32, (1, LANES), 1) < HEAD_DIM
    causal = (lax.broadcasted_iota(jnp.int32, (t, t), 1) <= lax.broadcasted_iota(jnp.int32, (t, t), 0))
    n_pairs = q_ref.shape[2] // LANES

    def pair_scores(hp):
        lanes = slice(hp * LANES, (hp + 1) * LANES)
        q2 = q_ref[0, :, lanes]
        kp2 = kp_ref[0, lanes, :].astype(bf16)
        kn2 = kn_ref[0, :, lanes]
        out = []
        for sub in range(2):
            qm = jnp.where(low if sub == 0 else jnp.logical_not(low), q2, jnp.zeros_like(q2))
            out.append((_dot(qm, kp2), _dot_nt(qm, kn2)))
        return out

    ahead = pair_scores(0)
    for hp in range(n_pairs):
        raw = ahead
        if hp + 1 < n_pairs:
            ahead = pair_scores(hp + 1)
        lanes = slice(hp * LANES, (hp + 1) * LANES)
        vp2 = vp_ref[0, lanes, :].astype(bf16)
        vn2 = vn_ref[0, :, lanes]
        halves = []
        for sub in range(2):
            h = 2 * hp + sub
            cq = cq_ref[0, 0, :, h:h + 1]
            s_past = raw[sub][0] + cq - ck_ref[0, h:h + 1, :past]
            s_new = raw[sub][1] + cq - ck_ref[0, h:h + 1, past:past + t]
            s_new = jnp.where(causal, s_new, NEG_INF)
            m = jnp.maximum(jnp.max(s_past, axis=1, keepdims=True), jnp.max(s_new, axis=1, keepdims=True))
            p_past = jnp.exp(s_past - m)
            p_new = jnp.exp(s_new - m)
            l = jnp.sum(p_past, axis=1, keepdims=True) + jnp.sum(p_new, axis=1, keepdims=True)
            acc = _dot_nt(p_past.astype(bf16), vp2) + _dot(p_new.astype(bf16), vn2)
            halves.append(acc / l)
        o_ref[0, :, lanes] = jnp.where(low, halves[0], halves[1]).astype(bf16)


def _attn_sample(qb, kb, vb, past_kt, past_vt, c_nat, c_t):
    b, t, _ = qb.shape
    past = past_kt.shape[2]
    g = SAMPLE_HEADS
    width = g * HEAD_DIM
    new = pl.BlockSpec((1, t, width), lambda i, h: (i, 0, h))
    old = pl.BlockSpec((1, width, past), lambda i, h: (i, h, 0))
    return pl.pallas_call(
        _attn_sample_kernel, grid=(b, N_HEADS // g),
        in_specs=[new, new, new, old, old,
                  pl.BlockSpec((1, 1, t, g), lambda i, h: (i, h, 0, 0)),
                  pl.BlockSpec((1, g, c_t.shape[2]), lambda i, h: (i, h, 0))],
        out_specs=new,
        out_shape=jax.ShapeDtypeStruct((b, t, D_MODEL), bf16),
        name="attn_sample", compiler_params=_params(2))(qb, kb, vb, past_kt, past_vt, c_nat, c_t)


def kernel(x_prompt, x_sample, state_conv, cache_k, cache_v, cache_logf, norm_ffn, ffn_w_in, ffn_w_out, norm_mix, conv_w_in, conv_w, conv_w_out, attn_w_in, attn_b_f, q_norm, k_norm, attn_w_out):
    bp, sp, d = x_prompt.shape
    bs, ss, _ = x_sample.shape
    past = cache_k.shape[2]

    n_p, n_s = bp * sp, bs * ss
    def ffn(xs, i, j, w, **kw):
        return _ffn(xs, norm_ffn[i, j].reshape(1, d), *w, **kw)

    w_slabs, wo_slabs = 32, 16
    ffn_casts = lambda i, j: [(ffn_w_in, (i, j), w_slabs), (ffn_w_out, (i, j), wo_slabs)]
    square_cast = lambda w: (w, (0,), w_slabs)

    y, conv_in_b, conv_out_b = ffn((x_prompt.reshape(n_p, d), x_sample.reshape(n_s, d)), 0, 0,
                                   (ffn_w_in[0, 0].astype(bf16), ffn_w_out[0, 0].astype(bf16)),
                                   casts=[square_cast(conv_w_in), square_cast(conv_w_out)])
    cw = (norm_mix[0].reshape(1, d), conv_in_b, conv_w[0], conv_out_b)
    y_mixed, conv_p, *w01 = _conv_mixer(y, bp, *cw, t=sp,
                                        casts=[(ffn_w_in, (0, 1), wo_slabs), (ffn_w_out, (0, 1), wo_slabs)])
    y, conv_s = _conv_mixer_sample(y, state_conv[0], *cw, t=ss, row0=n_p, into=y_mixed)
    y, *w10 = ffn(y, 0, 1, w01, casts=ffn_casts(1, 0))

    y, *w11, attn_out_b = ffn(y, 1, 0, w10, casts=ffn_casts(1, 1) + [square_cast(attn_w_out)])

    w_t = attn_w_in[0].T
    wt = jnp.concatenate([w_t[3 * d:], jnp.zeros((F_ROWS - N_HEADS, d), f32), w_t[d:2 * d], w_t[:d], w_t[2 * d:3 * d]],
                         axis=0).astype(bf16)
    head_of = jnp.arange(d) // HEAD_DIM
    seg = ((head_of[:, None] == jnp.arange(LANES)[None, :]).astype(f32) * (1.0 / HEAD_DIM)).astype(bf16)
    expand = (jnp.arange(LANES)[:, None] == head_of[None, :]).astype(bf16)
    expand2 = jnp.concatenate([expand, expand], axis=0)
    g_mix = norm_mix[1].reshape(1, d)
    bcol = attn_b_f[0].reshape(N_HEADS, 1)
    gk = jnp.tile(k_norm[0], N_HEADS).reshape(1, d)

    per_channel = lambda gain: jnp.broadcast_to(gain[:, None], (HEAD_DIM, ROW_TILE))
    kpt, vpt, fpt, ka, qat, vt = _attn_proj_prompt(
        y, g_mix, wt, bcol, per_channel(q_norm[0]), per_channel(k_norm[0]),
        jnp.triu(jnp.ones((ROW_TILE, ROW_TILE), f32)).astype(bf16), b=bp, t=sp)
    ap = _attn_prompt(qat, ka, vt)

    kq, vq, fq, fqt, qsb, ksb, vsb = _attn_proj(
        y, g_mix, wt, bcol, attn_b_f[0].reshape(1, N_HEADS), seg, expand2, jnp.tile(q_norm[0], N_HEADS).reshape(1, d), gk,
        rows=n_s, row0=n_p)
    pad = (-(past + ss)) % SCAN_TILE
    lf_all = jnp.concatenate([jnp.swapaxes(cache_logf[0], 1, 2),
                              jnp.swapaxes(fqt.reshape(N_HEADS, bs, ss), 0, 1),
                              jnp.zeros((bs, N_HEADS, pad), f32)], axis=2)
    cst = _cumsum_lanes(lf_all, jnp.triu(jnp.ones((SCAN_TILE, SCAN_TILE), f32)).astype(bf16))
    per_stream = lambda a: a.reshape(bs, ss, d)
    channel_major = lambda a: jnp.transpose(a, (0, 2, 3, 1)).reshape(bs, d, past)
    a_s = _attn_sample(per_stream(qsb), per_stream(ksb), per_stream(vsb),
                       channel_major(cache_k[0]), channel_major(cache_v[0]),
                       jnp.swapaxes(cst[:, :, past:past + ss].reshape(bs, -1, SAMPLE_HEADS, ss), 2, 3), cst)

    yp, ys = ffn(y, 1, 1, w11, mix=(ap.reshape(n_p, d), a_s.reshape(n_s, d), attn_out_b))

    hd = (N_HEADS, HEAD_DIM)
    token_major = lambda a: jnp.transpose(a.reshape(bp, *hd, sp), (0, 3, 1, 2))[None]
    return (yp.reshape(bp, sp, d), ys.reshape(bs, ss, d),
            conv_p[None], conv_s[None],
            token_major(kpt), token_major(vpt), jnp.swapaxes(fpt, 1, 2)[None],
            kq.reshape(1, bs, ss, *hd), vq.reshape(1, bs, ss, *hd), fq.reshape(1, bs, ss, N_HEADS))
```

```python
import functools

import jax
import jax.numpy as jnp
from jax import lax
from jax.experimental import pallas as pl
from jax.experimental.pallas import tpu as pltpu

D_MODEL = 1024
N_HEADS = 16
HEAD_DIM = 64
D_FF = 2816
CONV_WIDTH = 3
NORM_EPS = 1e-6
NEG_INF = -1e30
FFN_RESIDUAL = 0.5

LANES = 128
F_ROWS = 128
SUBLANES = 8
BF16_ROWS = 16
VMEM_LIMIT = 56 * 1024 * 1024
FF_CHUNKS = ((0, 1536), (1536, 2816))
ROW_TILE = 512
CONV_TILE = 1024
ATT_TILE = 256
ATT_HEADS = 16
SAMPLE_HEADS = 8
SCAN_TILE = 256
N_AUG = 3
V_ROWS = HEAD_DIM + BF16_ROWS
LOG2_E = 1.4426950408889634

f32 = jnp.float32
bf16 = jnp.bfloat16


def _dot(a, b):
    return jnp.dot(a, b, preferred_element_type=f32)


def _dot_nt(a, b):
    return lax.dot_general(a, b, (((1,), (1,)), ((), ())), preferred_element_type=f32)


def _rms(x, g):
    return x * lax.rsqrt(jnp.mean(x * x, axis=-1, keepdims=True) + NORM_EPS) * g


def _log_sigmoid(z):
    return -(jnp.maximum(-z, 0.0) + jnp.log1p(jnp.exp(-jnp.abs(z))))


def _split3(x):
    hi = x.astype(bf16)
    r1 = x - hi.astype(f32)
    mid = r1.astype(bf16)
    lo = (r1 - mid.astype(f32)).astype(bf16)
    return hi, mid, lo


def _params(n_axes):
    return pltpu.CompilerParams(dimension_semantics=("arbitrary",) * n_axes, vmem_limit_bytes=VMEM_LIMIT)


def _const_spec(shape):
    return pl.BlockSpec(shape, lambda *_: (0,) * len(shape), pipeline_mode=pl.Buffered(1))


def _cast_job(src, lead, steps, lin):
    rows, cols = src.shape[-2:]
    slab = rows // steps
    assert slab * steps == rows and slab % BF16_ROWS == 0
    pos = lambda *idx: jnp.minimum(lin(*idx), steps - 1)
    in_spec = pl.BlockSpec((None,) * len(lead) + (slab, cols), lambda *idx: (*lead, pos(*idx), 0))
    out_spec = pl.BlockSpec((slab, cols), lambda *idx: (pos(*idx), 0))
    return src, in_spec, out_spec, jax.ShapeDtypeStruct((rows, cols), bf16)


def _run_casts(srcs, dsts):
    for src_ref, dst_ref in zip(srcs, dsts):
        dst_ref[...] = src_ref[...].astype(bf16)


def _ffn_kernel(*refs, n_prompt, split_in, mix, n_jobs):
    refs = list(refs)
    cast_srcs = cast_dsts = ()
    if n_jobs:
        cast_dsts = refs[-n_jobs:]
        del refs[-n_jobs:]
        n_out = 2 if mix else 1
        cast_srcs = refs[-n_out - n_jobs:-n_out]
        del refs[-n_out - n_jobs:-n_out]
    is_prompt = pl.program_id(0) < n_prompt

    def rows():
        if split_in or mix:
            p_ref, s_ref = refs.pop(0), refs.pop(0)
            return jnp.where(is_prompt, p_ref[...], s_ref[...])
        return refs.pop(0)[...]

    if mix:
        x = refs.pop(0)[...]
        a = rows()
        x = x + _dot(a, refs.pop(0)[...])
    else:
        x = rows()
    g_ref, win_ref, wout_ref = refs[:3]
    outs = refs[3:]
    xn = _rms(x, g_ref[...]).astype(bf16)
    acc = None
    for lo, hi in FF_CHUNKS:
        a = _dot(xn, win_ref[:, lo:hi])
        b = _dot(xn, win_ref[:, D_FF + lo:D_FF + hi])
        if lo == 0:
            _run_casts(cast_srcs, cast_dsts)
        h = (a * jax.nn.sigmoid(a) * b).astype(bf16)
        y = _dot(h, wout_ref[lo:hi, :])
        acc = y if acc is None else acc + y
    y = x + FFN_RESIDUAL * acc
    if mix:
        @pl.when(is_prompt)
        def _():
            outs[0][...] = y

        @pl.when(jnp.logical_not(is_prompt))
        def _():
            outs[1][...] = y
    else:
        outs[0][...] = y


def _ffn(xs, g, w_in, w_out, mix=None, casts=()):
    tm = ROW_TILE
    split_in = isinstance(xs, tuple)
    if mix is None and split_in:
        n_prompt, n_sample = xs[0].shape[0] // tm, xs[1].shape[0] // tm
    elif mix is not None:
        n_prompt, n_sample = mix[0].shape[0] // tm, mix[1].shape[0] // tm
    else:
        n_prompt, n_sample = xs.shape[0] // tm, 0
    assert n_sample in (0, 1)
    steps = n_prompt + n_sample
    row = pl.BlockSpec((tm, D_MODEL), lambda i: (i, 0))
    prompt_row = pl.BlockSpec((tm, D_MODEL), lambda i: (jnp.minimum(i, n_prompt - 1), 0))
    sample_row = pl.BlockSpec((tm, D_MODEL), lambda i: (0, 0))
    w_specs = [_const_spec((1, D_MODEL)), _const_spec((D_MODEL, 2 * D_FF)), _const_spec((D_FF, D_MODEL))]
    stacked = jax.ShapeDtypeStruct((steps * tm, D_MODEL), f32)
    if mix is not None:
        a_p, a_s, wm = mix
        ins = (xs, a_p, a_s, wm)
        specs = [row, prompt_row, sample_row, _const_spec((D_MODEL, D_MODEL))]
        out_specs = [prompt_row, sample_row]
        out_shape = [jax.ShapeDtypeStruct((n_prompt * tm, D_MODEL), f32),
                     jax.ShapeDtypeStruct((n_sample * tm, D_MODEL), f32)]
    elif split_in:
        ins, specs, out_specs, out_shape = tuple(xs), [prompt_row, sample_row], [row], [stacked]
    else:
        ins, specs, out_specs, out_shape = (xs,), [row], [row], [stacked]
    jobs = [_cast_job(src, lead, n, lambda i: i) for src, lead, n in casts]
    kern = functools.partial(_ffn_kernel, n_prompt=n_prompt, split_in=split_in, mix=mix is not None,
                             n_jobs=len(jobs))
    return pl.pallas_call(
        kern, grid=(steps,), in_specs=specs + w_specs + [j[1] for j in jobs],
        out_specs=out_specs + [j[2] for j in jobs], out_shape=out_shape + [j[3] for j in jobs],
        name="ffn" if mix is None else "mix_ffn",
        compiler_params=_params(1))(*ins, g, w_in, w_out, *[j[0] for j in jobs])


def _conv_body(x, g_ref, win_ref, wk_ref, wout_ref, shifted, between=lambda: None):
    xn = _rms(x, g_ref[...]).astype(bf16)
    ch = _dot(xn, win_ref[:, D_MODEL:])
    gate_b = _dot(xn, win_ref[:, :D_MODEL])
    between()
    u = ch[:, :D_MODEL] * ch[:, D_MODEL:]
    u1, u2 = shifted(u)
    wk = wk_ref[...]
    conv = wk[0:1, :] * u2 + wk[1:2, :] * u1 + wk[2:3, :] * u
    return x + _dot((gate_b * conv).astype(bf16), wout_ref[...]), u


def _conv_kernel(x_ref, g_ref, win_ref, wk_ref, wout_ref, *rest, n_jobs):
    o_ref, st_ref = rest[n_jobs:n_jobs + 2]
    carry_ref = rest[-1]
    tt = x_ref.shape[0]

    @pl.when(pl.program_id(1) == 0)
    def _():
        carry_ref[...] = jnp.zeros(carry_ref.shape, f32)

    def shifted(u):
        prev2 = carry_ref[0:1, :]
        prev1 = carry_ref[1:2, :]
        row = lax.broadcasted_iota(jnp.int32, (tt, 1), 0)
        return (jnp.where(row == 0, prev1, pltpu.roll(u, 1, 0)),
                jnp.where(row == 0, prev2, jnp.where(row == 1, prev1, pltpu.roll(u, 2, 0))))

    o_ref[...], u = _conv_body(x_ref[...], g_ref, win_ref, wk_ref, wout_ref, shifted,
                               lambda: _run_casts(rest[:n_jobs], rest[n_jobs + 2:-1]))
    last = u[tt - 2:tt, :]
    carry_ref[0:2, :] = last
    st_ref[0] = last


def _conv_mixer(x, n_streams, g, win, wk, wout, *, t, casts=()):
    tt = CONV_TILE
    nt = t // tt
    blk = pl.BlockSpec((tt, D_MODEL), lambda i, j: (i * nt + j, 0))
    st = pl.BlockSpec((1, CONV_WIDTH - 1, D_MODEL), lambda i, j: (i, 0, 0))
    specs = [blk, _const_spec((1, D_MODEL)), _const_spec((D_MODEL, 3 * D_MODEL)),
             _const_spec((CONV_WIDTH, D_MODEL)), _const_spec((D_MODEL, D_MODEL))]
    jobs = [_cast_job(src, lead, n, lambda i, j: i * nt + j) for src, lead, n in casts]
    return pl.pallas_call(
        functools.partial(_conv_kernel, n_jobs=len(jobs)), grid=(n_streams, nt),
        in_specs=specs + [j[1] for j in jobs], out_specs=[blk, st] + [j[2] for j in jobs],
        out_shape=[jax.ShapeDtypeStruct(x.shape, f32),
                   jax.ShapeDtypeStruct((n_streams, CONV_WIDTH - 1, D_MODEL), f32)] + [j[3] for j in jobs],
        scratch_shapes=[pltpu.VMEM((SUBLANES, D_MODEL), f32)], input_output_aliases={0: 0},
        name="conv_mixer", compiler_params=_params(2))(x, g, win, wk, wout, *[j[0] for j in jobs])


def _conv_sample_kernel(x_ref, h1_ref, h2_ref, g_ref, win_ref, wk_ref, wout_ref, o_ref, u_ref, *, t):
    rows = x_ref.shape[0]

    def shifted(u):
        pos = lax.rem(lax.broadcasted_iota(jnp.int32, (rows, 1), 0), t)
        h1 = h1_ref[...]
        return (jnp.where(pos == 0, h1, pltpu.roll(u, 1, 0)),
                jnp.where(pos == 0, h2_ref[...], jnp.where(pos == 1, h1, pltpu.roll(u, 2, 0))))

    o_ref[...], u_ref[...] = _conv_body(x_ref[...], g_ref, win_ref, wk_ref, wout_ref, shifted)


def _conv_mixer_sample(x, state, g, win, wk, wout, *, t, row0):
    n = state.shape[0]
    rows = n * t
    blk = pl.BlockSpec((rows, D_MODEL), lambda i: (row0 // rows, 0))
    hist = pl.BlockSpec((rows, D_MODEL), lambda i: (0, 0))
    y, u = pl.pallas_call(
        functools.partial(_conv_sample_kernel, t=t), grid=(1,),
        in_specs=[blk, hist, hist, _const_spec((1, D_MODEL)), _const_spec((D_MODEL, 3 * D_MODEL)),
                  _const_spec((CONV_WIDTH, D_MODEL)), _const_spec((D_MODEL, D_MODEL))],
        out_specs=[blk, hist],
        out_shape=[jax.ShapeDtypeStruct(x.shape, f32), jax.ShapeDtypeStruct((rows, D_MODEL), f32)],
        input_output_aliases={0: 0},
        name="conv_mixer_sample", compiler_params=_params(1))(
            x, jnp.repeat(state[:, 1], t, axis=0), jnp.repeat(state[:, 0], t, axis=0), g, win, wk, wout)
    return y, u.reshape(n, t, D_MODEL)[:, t - (CONV_WIDTH - 1):]


def _head_norm(t, gain, seg_ref, exp_ref):
    ms = _dot((t * t).astype(bf16), seg_ref[...])
    r = lax.rsqrt(ms + NORM_EPS)
    r_hi = r.astype(bf16)
    r_lo = (r - r_hi.astype(f32)).astype(bf16)
    rb = _dot(jnp.concatenate([r_hi, r_lo], axis=1), exp_ref[...])
    return t * rb * gain


def _proj_kernel(x_ref, g_ref, wt_ref, bcol_ref, brow_ref, seg_ref, exp_ref, gq_ref, gk_ref,
                 k_ref, v_ref, lf_ref, lft_ref, qb_ref, kb_ref, vb_ref):
    xn = _rms(x_ref[...], g_ref[...]).astype(bf16)
    p = _dot_nt(xn, wt_ref[...])
    k = _head_norm(p[:, F_ROWS:F_ROWS + D_MODEL], gk_ref[...], seg_ref, exp_ref)
    q = _head_norm(p[:, F_ROWS + D_MODEL:F_ROWS + 2 * D_MODEL], gq_ref[...], seg_ref, exp_ref)
    v = p[:, F_ROWS + 2 * D_MODEL:]
    k_ref[...] = k
    v_ref[...] = v
    lf_ref[...] = _log_sigmoid(p[:, :N_HEADS] + brow_ref[...])
    lft_ref[...] = _log_sigmoid(_dot_nt(wt_ref[:N_HEADS, :], xn) + bcol_ref[...])
    qb_ref[...] = (q * (HEAD_DIM ** -0.5)).astype(bf16)
    kb_ref[...] = k.astype(bf16)
    vb_ref[...] = v.astype(bf16)


def _attn_proj(x, g, wt, bcol, brow, seg, exp, gq, gk, *, rows, row0):
    blk = lambda dt: jax.ShapeDtypeStruct((rows, D_MODEL), dt)
    full = lambda shape: pl.BlockSpec(shape, lambda i: (0, 0))
    return pl.pallas_call(
        _proj_kernel, grid=(1,),
        in_specs=[pl.BlockSpec((rows, D_MODEL), lambda i: (row0 // rows, 0)),
                  _const_spec((1, D_MODEL)), _const_spec((F_ROWS + 3 * D_MODEL, D_MODEL)),
                  _const_spec((N_HEADS, 1)), _const_spec((1, N_HEADS)),
                  _const_spec((D_MODEL, LANES)), _const_spec((2 * LANES, D_MODEL)),
                  _const_spec((1, D_MODEL)), _const_spec((1, D_MODEL))],
        out_specs=[full((rows, D_MODEL)), full((rows, D_MODEL)), full((rows, N_HEADS)), full((N_HEADS, rows)),
                   full((rows, D_MODEL)), full((rows, D_MODEL)), full((rows, D_MODEL))],
        out_shape=[blk(f32), blk(f32), jax.ShapeDtypeStruct((rows, N_HEADS), f32),
                   jax.ShapeDtypeStruct((N_HEADS, rows), f32), blk(bf16), blk(bf16), blk(bf16)],
        name="attn_proj", compiler_params=_params(1))(x, g, wt, bcol, brow, seg, exp, gq, gk)


def _proj_prompt_kernel(x_ref, g_ref, wt_ref, bcol_ref, gqc_ref, gkc_ref, triu_ref,
                        kt_ref, vt_ref, lft_ref, ka_ref, qat_ref, vtb_ref, ccol_ref):
    t = pl.program_id(1)
    tt = x_ref.shape[0]
    n_tiles = tt // ATT_TILE

    @pl.when(t == 0)
    def _():
        ccol_ref[...] = jnp.zeros(ccol_ref.shape, f32)

    xn = _rms(x_ref[...], g_ref[...]).astype(bf16)
    fk = _dot_nt(wt_ref[:F_ROWS + D_MODEL, :], xn)
    qt = _dot_nt(wt_ref[F_ROWS + D_MODEL:F_ROWS + 2 * D_MODEL, :], xn)
    kt = fk[F_ROWS:]

    lft = _log_sigmoid(fk[:N_HEADS] + bcol_ref[...])
    lft_ref[0] = lft
    ct3 = _dot(jnp.concatenate(_split3(lft), axis=0), triu_ref[...])
    ct = ct3[:N_HEADS] + ct3[N_HEADS:2 * N_HEADS] + ct3[2 * N_HEADS:] + ccol_ref[:, 0:1]
    ccol_ref[...] = jnp.broadcast_to(ct[:, tt - 1:tt], ccol_ref.shape)
    ct_pieces = [piece.astype(f32) for piece in _split3(ct * LOG2_E)]

    vt = _dot_nt(wt_ref[F_ROWS + 2 * D_MODEL:, :], xn)
    vt_ref[0] = vt
    vtb = vt.astype(bf16)
    ones_rows = jnp.where(lax.broadcasted_iota(jnp.int32, (V_ROWS - HEAD_DIM, ATT_TILE), 0) == 0, 1.0, 0.0).astype(bf16)
    for j in range(n_tiles):
        for h in range(N_HEADS):
            vtb_ref[0, j, h * V_ROWS:h * V_ROWS + HEAD_DIM, :] = (
                vtb[h * HEAD_DIM:(h + 1) * HEAD_DIM, j * ATT_TILE:(j + 1) * ATT_TILE])
            vtb_ref[0, j, h * V_ROWS + HEAD_DIM:(h + 1) * V_ROWS, :] = ones_rows

    sub = lax.broadcasted_iota(jnp.int32, (SUBLANES, 1), 0)
    ones = jnp.where(sub < 2 * N_AUG, 1.0, 0.0) * jnp.ones((1, tt), f32)
    pad = jnp.zeros((LANES - HEAD_DIM - SUBLANES, tt), f32)
    gqc = gqc_ref[...]
    gkc = gkc_ref[...]

    def head_rms(th, gain):
        return th * lax.rsqrt(jnp.mean(th * th, axis=0, keepdims=True) + NORM_EPS) * gain

    for h in range(N_HEADS):
        rows = slice(h * HEAD_DIM, (h + 1) * HEAD_DIM)
        kn = head_rms(kt[rows, :], gkc)
        kt_ref[0, rows, :] = kn
        aug_k, aug_q = ones, ones
        for n in range(N_AUG):
            piece = ct_pieces[n][h:h + 1, :]
            aug_k = jnp.where(sub == N_AUG + n, -piece, aug_k)
            aug_q = jnp.where(sub == n, piece, aug_q)
        ka_ref[0, h] = jnp.concatenate([kn, aug_k, pad], axis=0).T.astype(bf16)
        blk = jnp.concatenate([head_rms(qt[rows, :], gqc) * (LOG2_E * HEAD_DIM ** -0.5), aug_q, pad],
                              axis=0).astype(bf16)
        for j in range(n_tiles):
            qat_ref[0, j, h * LANES:(h + 1) * LANES, :] = blk[:, j * ATT_TILE:(j + 1) * ATT_TILE]


def _attn_proj_prompt(x, g, wt, bcol, gqc, gkc, triu, *, b, t):
    tt = ROW_TILE
    nt = tt // ATT_TILE
    time_minor = pl.BlockSpec((1, D_MODEL, tt), lambda i, j: (i, 0, j))
    big = jax.ShapeDtypeStruct((b, D_MODEL, t), f32)
    return pl.pallas_call(
        _proj_prompt_kernel, grid=(b, t // tt),
        in_specs=[pl.BlockSpec((tt, D_MODEL), lambda i, j: (i * (t // tt) + j, 0)),
                  _const_spec((1, D_MODEL)), _const_spec((F_ROWS + 3 * D_MODEL, D_MODEL)),
                  _const_spec((N_HEADS, 1)), _const_spec((HEAD_DIM, tt)), _const_spec((HEAD_DIM, tt)),
                  _const_spec((tt, tt))],
        out_specs=[time_minor, time_minor,
                   pl.BlockSpec((1, N_HEADS, tt), lambda i, j: (i, 0, j)),
                   pl.BlockSpec((1, N_HEADS, tt, LANES), lambda i, j: (i, 0, j, 0)),
                   pl.BlockSpec((1, nt, N_HEADS * LANES, ATT_TILE), lambda i, j: (i, j, 0, 0)),
                   pl.BlockSpec((1, nt, N_HEADS * V_ROWS, ATT_TILE), lambda i, j: (i, j, 0, 0))],
        out_shape=[big, big, jax.ShapeDtypeStruct((b, N_HEADS, t), f32),
                   jax.ShapeDtypeStruct((b, N_HEADS, t, LANES), bf16),
                   jax.ShapeDtypeStruct((b, t // ATT_TILE, N_HEADS * LANES, ATT_TILE), bf16),
                   jax.ShapeDtypeStruct((b, t // ATT_TILE, N_HEADS * V_ROWS, ATT_TILE), bf16)],
        scratch_shapes=[pltpu.VMEM((N_HEADS, LANES), f32)],
        name="attn_proj_prompt", compiler_params=_params(2))(x, g, wt, bcol, gqc, gkc, triu)


def _scan_kernel(x_ref, tri_ref, o_ref):
    rows = x_ref.shape[0]
    n = x_ref.shape[1] // SCAN_TILE
    carry = jnp.zeros((rows, 1), f32)
    for c in range(n):
        sl = slice(c * SCAN_TILE, (c + 1) * SCAN_TILE)
        cs = _dot(jnp.concatenate(_split3(x_ref[:, sl]), axis=0), tri_ref[...])
        out = cs[:rows] + cs[rows:2 * rows] + cs[2 * rows:] + carry
        o_ref[:, sl] = out
        carry = out[:, SCAN_TILE - 1:SCAN_TILE]


def _cumsum_lanes(lft, tri):
    b, h, t = lft.shape
    blk = pl.BlockSpec((b * h, t), lambda i: (0, 0))
    return pl.pallas_call(
        _scan_kernel, grid=(1,), in_specs=[blk, _const_spec((SCAN_TILE, SCAN_TILE))], out_specs=blk,
        out_shape=jax.ShapeDtypeStruct((b * h, t), f32), name="cumsum_lanes",
        compiler_params=_params(1))(lft.reshape(b * h, t), tri).reshape(b, h, t)


def _attn_kernel(qat_ref, qnext_ref, ka_ref, vt_ref, o_ref, m_ref, acc_ref, sa_ref, sb_ref):
    i = pl.program_id(2)
    tq = qat_ref.shape[3]
    tk = vt_ref.shape[3]
    causal = (lax.broadcasted_iota(jnp.int32, (tk, tq), 0) <= lax.broadcasted_iota(jnp.int32, (tk, tq), 1))
    m_ref[...] = jnp.full(m_ref.shape, NEG_INF, f32)
    acc_ref[...] = jnp.zeros(acc_ref.shape, f32)

    def scores(q_ref, j, s, buf):
        rows = pl.ds(pl.multiple_of(j * tk, tk), tk)
        buf[s] = _dot(ka_ref[0, s, rows, :], q_ref[0, 0, s * LANES:(s + 1) * LANES, :])

    def consume(j, s, buf, masked):
        sc = buf[s]
        if masked:
            sc = jnp.where(causal, sc, NEG_INF)
        m = m_ref[s]
        m_new = jnp.maximum(m, jnp.max(sc, axis=0, keepdims=True))
        alpha = jnp.exp2(m - m_new)
        pr = jnp.exp2(sc - m_new)
        vt = vt_ref[0, j, s * V_ROWS:(s + 1) * V_ROWS, :]
        acc_ref[s] = alpha * acc_ref[s] + _dot(vt, pr.astype(bf16))
        m_ref[s] = m_new

    heads = range(ATT_HEADS)

    def step(j, cur, nxt):
        for s in heads:
            scores(qat_ref, j + 1, s, nxt)
            consume(j, s, cur, False)

    def last_step(cur, nxt):
        for s in heads:
            scores(qnext_ref, 0, s, nxt)
            consume(i, s, cur, True)

    @pl.when(i == 0)
    def _():
        for s in heads:
            scores(qat_ref, 0, s, sa_ref)

    def sweep(a, b):
        def two_steps(t, carry):
            step(2 * t, a, b)
            step(2 * t + 1, b, a)
            return carry

        lax.fori_loop(0, i // 2, two_steps, 0)

        @pl.when(i % 2 == 0)
        def _():
            last_step(a, b)

        @pl.when(i % 2 == 1)
        def _():
            step(i - 1, a, b)
            last_step(b, a)

    first_in_b = ((i * (i + 1)) // 2) % 2

    @pl.when(first_in_b == 0)
    def _():
        sweep(sa_ref, sb_ref)

    @pl.when(first_in_b == 1)
    def _():
        sweep(sb_ref, sa_ref)

    o_t = jnp.concatenate([acc_ref[s, :HEAD_DIM, :] / acc_ref[s, HEAD_DIM:HEAD_DIM + 1, :]
                           for s in range(ATT_HEADS)], axis=0)
    o_ref[0] = o_t.T.astype(bf16)


def _attn_prompt(qat, ka, vt):
    b, nq, _, tq = qat.shape
    s = nq * tq
    g = ATT_HEADS
    return pl.pallas_call(
        _attn_kernel, grid=(b, N_HEADS // g, nq),
        in_specs=[pl.BlockSpec((1, 1, g * LANES, tq), lambda i, h, j: (i, j, h, 0)),
                  pl.BlockSpec((1, 1, g * LANES, tq), lambda i, h, j: (i, jnp.minimum(j + 1, nq - 1), h, 0)),
                  pl.BlockSpec((1, g, s, LANES), lambda i, h, j: (i, h, 0, 0)),
                  pl.BlockSpec((1, nq, g * V_ROWS, tq), lambda i, h, j: (i, 0, h, 0))],
        out_specs=pl.BlockSpec((1, tq, g * HEAD_DIM), lambda i, h, j: (i, j, h)),
        out_shape=jax.ShapeDtypeStruct((b, s, D_MODEL), bf16),
        scratch_shapes=[pltpu.VMEM((g, 1, tq), f32), pltpu.VMEM((g, V_ROWS, tq), f32),
                        pltpu.VMEM((g, tq, tq), f32), pltpu.VMEM((g, tq, tq), f32)],
        name="attn_prompt",
        compiler_params=_params(3))(qat, qat, ka, vt)


def _attn_sample_kernel(q_ref, kn_ref, vn_ref, kp_ref, vp_ref, cq_ref, ck_ref, o_ref):
    t = q_ref.shape[1]
    past = kp_ref.shape[2]
    low = lax.broadcasted_iota(jnp.int32, (1, LANES), 1) < HEAD_DIM
    causal = (lax.broadcasted_iota(jnp.int32, (t, t), 1) <= lax.broadcasted_iota(jnp.int32, (t, t), 0))
    n_pairs = q_ref.shape[2] // LANES

    def pair_scores(hp):
        lanes = slice(hp * LANES, (hp + 1) * LANES)
        q2 = q_ref[0, :, lanes]
        kp2 = kp_ref[0, lanes, :].astype(bf16)
        kn2 = kn_ref[0, :, lanes]
        out = []
        for sub in range(2):
            qm = jnp.where(low if sub == 0 else jnp.logical_not(low), q2, jnp.zeros_like(q2))
            out.append((_dot(qm, kp2), _dot_nt(qm, kn2)))
        return out

    ahead = pair_scores(0)
    for hp in range(n_pairs):
        raw = ahead
        if hp + 1 < n_pairs:
            ahead = pair_scores(hp + 1)
        lanes = slice(hp * LANES, (hp + 1) * LANES)
        vp2 = vp_ref[0, lanes, :].astype(bf16)
        vn2 = vn_ref[0, :, lanes]
        halves = []
        for sub in range(2):
            h = 2 * hp + sub
            cq = cq_ref[0, 0, :, h:h + 1]
            s_past = raw[sub][0] + cq - ck_ref[0, h:h + 1, :past]
            s_new = raw[sub][1] + cq - ck_ref[0, h:h + 1, past:past + t]
            s_new = jnp.where(causal, s_new, NEG_INF)
            m = jnp.maximum(jnp.max(s_past, axis=1, keepdims=True), jnp.max(s_new, axis=1, keepdims=True))
            p_past = jnp.exp(s_past - m)
            p_new = jnp.exp(s_new - m)
            l = jnp.sum(p_past, axis=1, keepdims=True) + jnp.sum(p_new, axis=1, keepdims=True)
            acc = _dot_nt(p_past.astype(bf16), vp2) + _dot(p_new.astype(bf16), vn2)
            halves.append(acc / l)
        o_ref[0, :, lanes] = jnp.where(low, halves[0], halves[1]).astype(bf16)


def _attn_sample(qb, kb, vb, past_kt, past_vt, c_nat, c_t):
    b, t, _ = qb.shape
    past = past_kt.shape[2]
    g = SAMPLE_HEADS
    width = g * HEAD_DIM
    new = pl.BlockSpec((1, t, width), lambda i, h: (i, 0, h))
    old = pl.BlockSpec((1, width, past), lambda i, h: (i, h, 0))
    return pl.pallas_call(
        _attn_sample_kernel, grid=(b, N_HEADS // g),
        in_specs=[new, new, new, old, old,
                  pl.BlockSpec((1, 1, t, g), lambda i, h: (i, h, 0, 0)),
                  pl.BlockSpec((1, g, c_t.shape[2]), lambda i, h: (i, h, 0))],
        out_specs=new,
        out_shape=jax.ShapeDtypeStruct((b, t, D_MODEL), bf16),
        name="attn_sample", compiler_params=_params(2))(qb, kb, vb, past_kt, past_vt, c_nat, c_t)


def kernel(x_prompt, x_sample, state_conv, cache_k, cache_v, cache_logf, norm_ffn, ffn_w_in, ffn_w_out, norm_mix, conv_w_in, conv_w, conv_w_out, attn_w_in, attn_b_f, q_norm, k_norm, attn_w_out):
    bp, sp, d = x_prompt.shape
    bs, ss, _ = x_sample.shape
    past = cache_k.shape[2]

    n_p, n_s = bp * sp, bs * ss
    def ffn(xs, i, j, w, **kw):
        return _ffn(xs, norm_ffn[i, j].reshape(1, d), *w, **kw)

    w_slabs, wo_slabs = 32, 16
    ffn_casts = lambda i, j: [(ffn_w_in, (i, j), w_slabs), (ffn_w_out, (i, j), wo_slabs)]
    square_cast = lambda w: (w, (0,), w_slabs)

    y, conv_in_b, conv_out_b = ffn((x_prompt.reshape(n_p, d), x_sample.reshape(n_s, d)), 0, 0,
                                   (ffn_w_in[0, 0].astype(bf16), ffn_w_out[0, 0].astype(bf16)),
                                   casts=[square_cast(conv_w_in), square_cast(conv_w_out)])
    cw = (norm_mix[0].reshape(1, d), conv_in_b, conv_w[0], conv_out_b)
    y_mixed, conv_p, *w01 = _conv_mixer(y, bp, *cw, t=sp,
                                        casts=[(ffn_w_in, (0, 1), wo_slabs), (ffn_w_out, (0, 1), wo_slabs)])
    y, conv_s = _conv_mixer_sample(y_mixed, state_conv[0], *cw, t=ss, row0=n_p)
    y, *w10 = ffn(y, 0, 1, w01, casts=ffn_casts(1, 0))

    y, *w11, attn_out_b = ffn(y, 1, 0, w10, casts=ffn_casts(1, 1) + [square_cast(attn_w_out)])

    w_t = attn_w_in[0].T
    wt = jnp.concatenate([w_t[3 * d:], jnp.zeros((F_ROWS - N_HEADS, d), f32), w_t[d:2 * d], w_t[:d], w_t[2 * d:3 * d]],
                         axis=0).astype(bf16)
    head_of = jnp.arange(d) // HEAD_DIM
    seg = ((head_of[:, None] == jnp.arange(LANES)[None, :]).astype(f32) * (1.0 / HEAD_DIM)).astype(bf16)
    expand = (jnp.arange(LANES)[:, None] == head_of[None, :]).astype(bf16)
    expand2 = jnp.concatenate([expand, expand], axis=0)
    g_mix = norm_mix[1].reshape(1, d)
    bcol = attn_b_f[0].reshape(N_HEADS, 1)
    gk = jnp.tile(k_norm[0], N_HEADS).reshape(1, d)

    per_channel = lambda gain: jnp.broadcast_to(gain[:, None], (HEAD_DIM, ROW_TILE))
    kpt, vpt, fpt, ka, qat, vt = _attn_proj_prompt(
        y, g_mix, wt, bcol, per_channel(q_norm[0]), per_channel(k_norm[0]),
        jnp.triu(jnp.ones((ROW_TILE, ROW_TILE), f32)).astype(bf16), b=bp, t=sp)
    ap = _attn_prompt(qat, ka, vt)

    kq, vq, fq, fqt, qsb, ksb, vsb = _attn_proj(
        y, g_mix, wt, bcol, attn_b_f[0].reshape(1, N_HEADS), seg, expand2, jnp.tile(q_norm[0], N_HEADS).reshape(1, d), gk,
        rows=n_s, row0=n_p)
    pad = (-(past + ss)) % SCAN_TILE
    lf_all = jnp.concatenate([jnp.swapaxes(cache_logf[0], 1, 2),
                              jnp.swapaxes(fqt.reshape(N_HEADS, bs, ss), 0, 1),
                              jnp.zeros((bs, N_HEADS, pad), f32)], axis=2)
    cst = _cumsum_lanes(lf_all, jnp.triu(jnp.ones((SCAN_TILE, SCAN_TILE), f32)).astype(bf16))
    per_stream = lambda a: a.reshape(bs, ss, d)
    channel_major = lambda a: jnp.transpose(a, (0, 2, 3, 1)).reshape(bs, d, past)
    a_s = _attn_sample(per_stream(qsb), per_stream(ksb), per_stream(vsb),
                       channel_major(cache_k[0]), channel_major(cache_v[0]),
                       jnp.swapaxes(cst[:, :, past:past + ss].reshape(bs, -1, SAMPLE_HEADS, ss), 2, 3), cst)

    yp, ys = ffn(y, 1, 1, w11, mix=(ap.reshape(n_p, d), a_s.reshape(n_s, d), attn_out_b))

    hd = (N_HEADS, HEAD_DIM)
    token_major = lambda a: jnp.transpose(a.reshape(bp, *hd, sp), (0, 3, 1, 2))[None]
    return (yp.reshape(bp, sp, d), ys.reshape(bs, ss, d),
            conv_p[None], conv_s[None],
            token_major(kpt), token_major(vpt), jnp.swapaxes(fpt, 1, 2)[None],
            kq.reshape(1, bs, ss, *hd), vq.reshape(1, bs, ss, *hd), fq.reshape(1, bs, ss, N_HEADS))
```

```python
import functools

import jax
import jax.numpy as jnp
from jax import lax
from jax.experimental import pallas as pl
from jax.experimental.pallas import tpu as pltpu

D_MODEL = 1024
N_HEADS = 16
HEAD_DIM = 64
D_FF = 2816
CONV_WIDTH = 3
NORM_EPS = 1e-6
NEG_INF = -1e30
FFN_RESIDUAL = 0.5

LANES = 128
F_ROWS = 128
SUBLANES = 8
BF16_ROWS = 16
VMEM_LIMIT = 56 * 1024 * 1024
FF_CHUNKS = ((0, 1536), (1536, 2816))
ROW_TILE = 512
CONV_TILE = 1024
ATT_TILE = 256
ATT_HEADS = 16
SAMPLE_HEADS = 8
RING_SLOTS = 3
SCAN_TILE = 256
N_AUG = 3
V_ROWS = HEAD_DIM + BF16_ROWS
LOG2_E = 1.4426950408889634

f32 = jnp.float32
bf16 = jnp.bfloat16


def _dot(a, b):
    return jnp.dot(a, b, preferred_element_type=f32)


def _dot_nt(a, b):
    return lax.dot_general(a, b, (((1,), (1,)), ((), ())), preferred_element_type=f32)


def _rms(x, g):
    return x * lax.rsqrt(jnp.mean(x * x, axis=-1, keepdims=True) + NORM_EPS) * g


def _log_sigmoid(z):
    return -(jnp.maximum(-z, 0.0) + jnp.log1p(jnp.exp(-jnp.abs(z))))


def _split3(x):
    hi = x.astype(bf16)
    r1 = x - hi.astype(f32)
    mid = r1.astype(bf16)
    lo = (r1 - mid.astype(f32)).astype(bf16)
    return hi, mid, lo


def _params(n_axes):
    return pltpu.CompilerParams(dimension_semantics=("arbitrary",) * n_axes, vmem_limit_bytes=VMEM_LIMIT)


def _const_spec(shape):
    return pl.BlockSpec(shape, lambda *_: (0,) * len(shape), pipeline_mode=pl.Buffered(1))


def _cast_job(src, lead, steps, lin):
    rows, cols = src.shape[-2:]
    slab = rows // steps
    assert slab * steps == rows and slab % BF16_ROWS == 0
    pos = lambda *idx: jnp.minimum(lin(*idx), steps - 1)
    in_spec = pl.BlockSpec((None,) * len(lead) + (slab, cols), lambda *idx: (*lead, pos(*idx), 0))
    out_spec = pl.BlockSpec((slab, cols), lambda *idx: (pos(*idx), 0))
    return src, in_spec, out_spec, jax.ShapeDtypeStruct((rows, cols), bf16)


def _run_casts(srcs, dsts):
    for src_ref, dst_ref in zip(srcs, dsts):
        dst_ref[...] = src_ref[...].astype(bf16)


def _ffn_kernel(*refs, n_prompt, split_in, mix, n_jobs):
    refs = list(refs)
    cast_srcs = cast_dsts = ()
    if n_jobs:
        cast_dsts = refs[-n_jobs:]
        del refs[-n_jobs:]
        n_out = 2 if mix else 1
        cast_srcs = refs[-n_out - n_jobs:-n_out]
        del refs[-n_out - n_jobs:-n_out]
    is_prompt = pl.program_id(0) < n_prompt

    def rows():
        if split_in or mix:
            p_ref, s_ref = refs.pop(0), refs.pop(0)
            return jnp.where(is_prompt, p_ref[...], s_ref[...])
        return refs.pop(0)[...]

    if mix:
        x = refs.pop(0)[...]
        a = rows()
        x = x + _dot(a, refs.pop(0)[...])
    else:
        x = rows()
    g_ref, win_ref, wout_ref = refs[:3]
    outs = refs[3:]
    xn = _rms(x, g_ref[...]).astype(bf16)
    acc = None
    for lo, hi in FF_CHUNKS:
        a = _dot(xn, win_ref[:, lo:hi])
        b = _dot(xn, win_ref[:, D_FF + lo:D_FF + hi])
        if lo == 0:
            _run_casts(cast_srcs, cast_dsts)
        h = (a * jax.nn.sigmoid(a) * b).astype(bf16)
        y = _dot(h, wout_ref[lo:hi, :])
        acc = y if acc is None else acc + y
    y = x + FFN_RESIDUAL * acc
    if mix:
        @pl.when(is_prompt)
        def _():
            outs[0][...] = y

        @pl.when(jnp.logical_not(is_prompt))
        def _():
            outs[1][...] = y
    else:
        outs[0][...] = y


def _ffn(xs, g, w_in, w_out, mix=None, casts=()):
    tm = ROW_TILE
    split_in = isinstance(xs, tuple)
    if mix is None and split_in:
        n_prompt, n_sample = xs[0].shape[0] // tm, xs[1].shape[0] // tm
    elif mix is not None:
        n_prompt, n_sample = mix[0].shape[0] // tm, mix[1].shape[0] // tm
    else:
        n_prompt, n_sample = xs.shape[0] // tm, 0
    assert n_sample in (0, 1)
    steps = n_prompt + n_sample
    row = pl.BlockSpec((tm, D_MODEL), lambda i: (i, 0))
    prompt_row = pl.BlockSpec((tm, D_MODEL), lambda i: (jnp.minimum(i, n_prompt - 1), 0))
    sample_row = pl.BlockSpec((tm, D_MODEL), lambda i: (0, 0))
    w_specs = [_const_spec((1, D_MODEL)), _const_spec((D_MODEL, 2 * D_FF)), _const_spec((D_FF, D_MODEL))]
    stacked = jax.ShapeDtypeStruct((steps * tm, D_MODEL), f32)
    if mix is not None:
        a_p, a_s, wm = mix
        ins = (xs, a_p, a_s, wm)
        specs = [row, prompt_row, sample_row, _const_spec((D_MODEL, D_MODEL))]
        out_specs = [prompt_row, sample_row]
        out_shape = [jax.ShapeDtypeStruct((n_prompt * tm, D_MODEL), f32),
                     jax.ShapeDtypeStruct((n_sample * tm, D_MODEL), f32)]
    elif split_in:
        ins, specs, out_specs, out_shape = tuple(xs), [prompt_row, sample_row], [row], [stacked]
    else:
        ins, specs, out_specs, out_shape = (xs,), [row], [row], [stacked]
    jobs = [_cast_job(src, lead, n, lambda i: i) for src, lead, n in casts]
    kern = functools.partial(_ffn_kernel, n_prompt=n_prompt, split_in=split_in, mix=mix is not None,
                             n_jobs=len(jobs))
    return pl.pallas_call(
        kern, grid=(steps,), in_specs=specs + w_specs + [j[1] for j in jobs],
        out_specs=out_specs + [j[2] for j in jobs], out_shape=out_shape + [j[3] for j in jobs],
        name="ffn" if mix is None else "mix_ffn",
        compiler_params=_params(1))(*ins, g, w_in, w_out, *[j[0] for j in jobs])


def _conv_body(x, g_ref, win_ref, wk_ref, wout_ref, shifted, between=lambda: None):
    xn = _rms(x, g_ref[...]).astype(bf16)
    ch = _dot(xn, win_ref[:, D_MODEL:])
    gate_b = _dot(xn, win_ref[:, :D_MODEL])
    between()
    u = ch[:, :D_MODEL] * ch[:, D_MODEL:]
    u1, u2 = shifted(u)
    wk = wk_ref[...]
    conv = wk[0:1, :] * u2 + wk[1:2, :] * u1 + wk[2:3, :] * u
    return x + _dot((gate_b * conv).astype(bf16), wout_ref[...]), u


def _conv_kernel(x_ref, g_ref, win_ref, wk_ref, wout_ref, *rest, n_jobs):
    o_ref, st_ref = rest[n_jobs:n_jobs + 2]
    carry_ref = rest[-1]
    tt = x_ref.shape[0]

    @pl.when(pl.program_id(1) == 0)
    def _():
        carry_ref[...] = jnp.zeros(carry_ref.shape, f32)

    def shifted(u):
        prev2 = carry_ref[0:1, :]
        prev1 = carry_ref[1:2, :]
        row = lax.broadcasted_iota(jnp.int32, (tt, 1), 0)
        return (jnp.where(row == 0, prev1, pltpu.roll(u, 1, 0)),
                jnp.where(row == 0, prev2, jnp.where(row == 1, prev1, pltpu.roll(u, 2, 0))))

    o_ref[...], u = _conv_body(x_ref[...], g_ref, win_ref, wk_ref, wout_ref, shifted,
                               lambda: _run_casts(rest[:n_jobs], rest[n_jobs + 2:-1]))
    last = u[tt - 2:tt, :]
    carry_ref[0:2, :] = last
    st_ref[0] = last


def _conv_mixer(x, n_streams, g, win, wk, wout, *, t, casts=()):
    tt = CONV_TILE
    nt = t // tt
    blk = pl.BlockSpec((tt, D_MODEL), lambda i, j: (i * nt + j, 0))
    st = pl.BlockSpec((1, CONV_WIDTH - 1, D_MODEL), lambda i, j: (i, 0, 0))
    specs = [blk, _const_spec((1, D_MODEL)), _const_spec((D_MODEL, 3 * D_MODEL)),
             _const_spec((CONV_WIDTH, D_MODEL)), _const_spec((D_MODEL, D_MODEL))]
    jobs = [_cast_job(src, lead, n, lambda i, j: i * nt + j) for src, lead, n in casts]
    return pl.pallas_call(
        functools.partial(_conv_kernel, n_jobs=len(jobs)), grid=(n_streams, nt),
        in_specs=specs + [j[1] for j in jobs], out_specs=[blk, st] + [j[2] for j in jobs],
        out_shape=[jax.ShapeDtypeStruct(x.shape, f32),
                   jax.ShapeDtypeStruct((n_streams, CONV_WIDTH - 1, D_MODEL), f32)] + [j[3] for j in jobs],
        scratch_shapes=[pltpu.VMEM((SUBLANES, D_MODEL), f32)], input_output_aliases={0: 0},
        name="conv_mixer", compiler_params=_params(2))(x, g, win, wk, wout, *[j[0] for j in jobs])


def _conv_sample_kernel(x_ref, h1_ref, h2_ref, g_ref, win_ref, wk_ref, wout_ref, o_ref, u_ref, *, t):
    rows = x_ref.shape[0]

    def shifted(u):
        pos = lax.rem(lax.broadcasted_iota(jnp.int32, (rows, 1), 0), t)
        h1 = h1_ref[...]
        return (jnp.where(pos == 0, h1, pltpu.roll(u, 1, 0)),
                jnp.where(pos == 0, h2_ref[...], jnp.where(pos == 1, h1, pltpu.roll(u, 2, 0))))

    o_ref[...], u_ref[...] = _conv_body(x_ref[...], g_ref, win_ref, wk_ref, wout_ref, shifted)


def _conv_mixer_sample(x, state, g, win, wk, wout, *, t, row0):
    n = state.shape[0]
    rows = n * t
    blk = pl.BlockSpec((rows, D_MODEL), lambda i: (row0 // rows, 0))
    hist = pl.BlockSpec((rows, D_MODEL), lambda i: (0, 0))
    y, u = pl.pallas_call(
        functools.partial(_conv_sample_kernel, t=t), grid=(1,),
        in_specs=[blk, hist, hist, _const_spec((1, D_MODEL)), _const_spec((D_MODEL, 3 * D_MODEL)),
                  _const_spec((CONV_WIDTH, D_MODEL)), _const_spec((D_MODEL, D_MODEL))],
        out_specs=[blk, hist],
        out_shape=[jax.ShapeDtypeStruct(x.shape, f32), jax.ShapeDtypeStruct((rows, D_MODEL), f32)],
        input_output_aliases={0: 0},
        name="conv_mixer_sample", compiler_params=_params(1))(
            x, jnp.repeat(state[:, 1], t, axis=0), jnp.repeat(state[:, 0], t, axis=0), g, win, wk, wout)
    return y, u.reshape(n, t, D_MODEL)[:, t - (CONV_WIDTH - 1):]


def _head_norm(t, gain, seg_ref, exp_ref):
    ms = _dot((t * t).astype(bf16), seg_ref[...])
    r = lax.rsqrt(ms + NORM_EPS)
    r_hi = r.astype(bf16)
    r_lo = (r - r_hi.astype(f32)).astype(bf16)
    rb = _dot(jnp.concatenate([r_hi, r_lo], axis=1), exp_ref[...])
    return t * rb * gain


def _proj_kernel(x_ref, g_ref, wt_ref, bcol_ref, brow_ref, seg_ref, exp_ref, gq_ref, gk_ref,
                 k_ref, v_ref, lf_ref, lft_ref, qb_ref, kb_ref, vb_ref):
    xn = _rms(x_ref[...], g_ref[...]).astype(bf16)
    p = _dot_nt(xn, wt_ref[...])
    k = _head_norm(p[:, F_ROWS:F_ROWS + D_MODEL], gk_ref[...], seg_ref, exp_ref)
    q = _head_norm(p[:, F_ROWS + D_MODEL:F_ROWS + 2 * D_MODEL], gq_ref[...], seg_ref, exp_ref)
    v = p[:, F_ROWS + 2 * D_MODEL:]
    k_ref[...] = k
    v_ref[...] = v
    lf_ref[...] = _log_sigmoid(p[:, :N_HEADS] + brow_ref[...])
    lft_ref[...] = _log_sigmoid(_dot_nt(wt_ref[:N_HEADS, :], xn) + bcol_ref[...])
    qb_ref[...] = (q * (HEAD_DIM ** -0.5)).astype(bf16)
    kb_ref[...] = k.astype(bf16)
    vb_ref[...] = v.astype(bf16)


def _attn_proj(x, g, wt, bcol, brow, seg, exp, gq, gk, *, rows, row0):
    blk = lambda dt: jax.ShapeDtypeStruct((rows, D_MODEL), dt)
    full = lambda shape: pl.BlockSpec(shape, lambda i: (0, 0))
    return pl.pallas_call(
        _proj_kernel, grid=(1,),
        in_specs=[pl.BlockSpec((rows, D_MODEL), lambda i: (row0 // rows, 0)),
                  _const_spec((1, D_MODEL)), _const_spec((F_ROWS + 3 * D_MODEL, D_MODEL)),
                  _const_spec((N_HEADS, 1)), _const_spec((1, N_HEADS)),
                  _const_spec((D_MODEL, LANES)), _const_spec((2 * LANES, D_MODEL)),
                  _const_spec((1, D_MODEL)), _const_spec((1, D_MODEL))],
        out_specs=[full((rows, D_MODEL)), full((rows, D_MODEL)), full((rows, N_HEADS)), full((N_HEADS, rows)),
                   full((rows, D_MODEL)), full((rows, D_MODEL)), full((rows, D_MODEL))],
        out_shape=[blk(f32), blk(f32), jax.ShapeDtypeStruct((rows, N_HEADS), f32),
                   jax.ShapeDtypeStruct((N_HEADS, rows), f32), blk(bf16), blk(bf16), blk(bf16)],
        name="attn_proj", compiler_params=_params(1))(x, g, wt, bcol, brow, seg, exp, gq, gk)


def _proj_prompt_kernel(x_ref, g_ref, wt_ref, bcol_ref, gqc_ref, gkc_ref, triu_ref,
                        kt_ref, vt_ref, lft_ref, ka_ref, qat_ref, vtb_ref, ccol_ref):
    t = pl.program_id(1)
    tt = x_ref.shape[0]
    n_tiles = tt // ATT_TILE

    @pl.when(t == 0)
    def _():
        ccol_ref[...] = jnp.zeros(ccol_ref.shape, f32)

    xn = _rms(x_ref[...], g_ref[...]).astype(bf16)
    fk = _dot_nt(wt_ref[:F_ROWS + D_MODEL, :], xn)
    qt = _dot_nt(wt_ref[F_ROWS + D_MODEL:F_ROWS + 2 * D_MODEL, :], xn)
    kt = fk[F_ROWS:]

    lft = _log_sigmoid(fk[:N_HEADS] + bcol_ref[...])
    lft_ref[0] = lft
    ct3 = _dot(jnp.concatenate(_split3(lft), axis=0), triu_ref[...])
    ct = ct3[:N_HEADS] + ct3[N_HEADS:2 * N_HEADS] + ct3[2 * N_HEADS:] + ccol_ref[:, 0:1]
    ccol_ref[...] = jnp.broadcast_to(ct[:, tt - 1:tt], ccol_ref.shape)
    ct_pieces = [piece.astype(f32) for piece in _split3(ct * LOG2_E)]

    vt = _dot_nt(wt_ref[F_ROWS + 2 * D_MODEL:, :], xn)
    vt_ref[0] = vt
    vtb = vt.astype(bf16)
    ones_rows = jnp.where(lax.broadcasted_iota(jnp.int32, (V_ROWS - HEAD_DIM, ATT_TILE), 0) == 0, 1.0, 0.0).astype(bf16)
    for j in range(n_tiles):
        for h in range(N_HEADS):
            vtb_ref[0, j, h * V_ROWS:h * V_ROWS + HEAD_DIM, :] = (
                vtb[h * HEAD_DIM:(h + 1) * HEAD_DIM, j * ATT_TILE:(j + 1) * ATT_TILE])
            vtb_ref[0, j, h * V_ROWS + HEAD_DIM:(h + 1) * V_ROWS, :] = ones_rows

    sub = lax.broadcasted_iota(jnp.int32, (SUBLANES, 1), 0)
    ones = jnp.where(sub < 2 * N_AUG, 1.0, 0.0) * jnp.ones((1, tt), f32)
    pad = jnp.zeros((LANES - HEAD_DIM - SUBLANES, tt), f32)
    gqc = gqc_ref[...]
    gkc = gkc_ref[...]

    def head_rms(th, gain):
        return th * lax.rsqrt(jnp.mean(th * th, axis=0, keepdims=True) + NORM_EPS) * gain

    for h in range(N_HEADS):
        rows = slice(h * HEAD_DIM, (h + 1) * HEAD_DIM)
        kn = head_rms(kt[rows, :], gkc)
        kt_ref[0, rows, :] = kn
        aug_k, aug_q = ones, ones
        for n in range(N_AUG):
            piece = ct_pieces[n][h:h + 1, :]
            aug_k = jnp.where(sub == N_AUG + n, -piece, aug_k)
            aug_q = jnp.where(sub == n, piece, aug_q)
        ka_ref[0, h] = jnp.concatenate([kn, aug_k, pad], axis=0).T.astype(bf16)
        blk = jnp.concatenate([head_rms(qt[rows, :], gqc) * (LOG2_E * HEAD_DIM ** -0.5), aug_q, pad],
                              axis=0).astype(bf16)
        for j in range(n_tiles):
            qat_ref[0, j, h * LANES:(h + 1) * LANES, :] = blk[:, j * ATT_TILE:(j + 1) * ATT_TILE]


def _attn_proj_prompt(x, g, wt, bcol, gqc, gkc, triu, *, b, t):
    tt = ROW_TILE
    nt = tt // ATT_TILE
    time_minor = pl.BlockSpec((1, D_MODEL, tt), lambda i, j: (i, 0, j))
    big = jax.ShapeDtypeStruct((b, D_MODEL, t), f32)
    return pl.pallas_call(
        _proj_prompt_kernel, grid=(b, t // tt),
        in_specs=[pl.BlockSpec((tt, D_MODEL), lambda i, j: (i * (t // tt) + j, 0)),
                  _const_spec((1, D_MODEL)), _const_spec((F_ROWS + 3 * D_MODEL, D_MODEL)),
                  _const_spec((N_HEADS, 1)), _const_spec((HEAD_DIM, tt)), _const_spec((HEAD_DIM, tt)),
                  _const_spec((tt, tt))],
        out_specs=[time_minor, time_minor,
                   pl.BlockSpec((1, N_HEADS, tt), lambda i, j: (i, 0, j)),
                   pl.BlockSpec((1, N_HEADS, tt, LANES), lambda i, j: (i, 0, j, 0)),
                   pl.BlockSpec((1, nt, N_HEADS * LANES, ATT_TILE), lambda i, j: (i, j, 0, 0)),
                   pl.BlockSpec((1, nt, N_HEADS * V_ROWS, ATT_TILE), lambda i, j: (i, j, 0, 0))],
        out_shape=[big, big, jax.ShapeDtypeStruct((b, N_HEADS, t), f32),
                   jax.ShapeDtypeStruct((b, N_HEADS, t, LANES), bf16),
                   jax.ShapeDtypeStruct((b, t // ATT_TILE, N_HEADS * LANES, ATT_TILE), bf16),
                   jax.ShapeDtypeStruct((b, t // ATT_TILE, N_HEADS * V_ROWS, ATT_TILE), bf16)],
        scratch_shapes=[pltpu.VMEM((N_HEADS, LANES), f32)],
        name="attn_proj_prompt", compiler_params=_params(2))(x, g, wt, bcol, gqc, gkc, triu)


def _scan_kernel(x_ref, tri_ref, o_ref):
    rows = x_ref.shape[0]
    n = x_ref.shape[1] // SCAN_TILE
    carry = jnp.zeros((rows, 1), f32)
    for c in range(n):
        sl = slice(c * SCAN_TILE, (c + 1) * SCAN_TILE)
        cs = _dot(jnp.concatenate(_split3(x_ref[:, sl]), axis=0), tri_ref[...])
        out = cs[:rows] + cs[rows:2 * rows] + cs[2 * rows:] + carry
        o_ref[:, sl] = out
        carry = out[:, SCAN_TILE - 1:SCAN_TILE]


def _cumsum_lanes(lft, tri):
    b, h, t = lft.shape
    blk = pl.BlockSpec((b * h, t), lambda i: (0, 0))
    return pl.pallas_call(
        _scan_kernel, grid=(1,), in_specs=[blk, _const_spec((SCAN_TILE, SCAN_TILE))], out_specs=blk,
        out_shape=jax.ShapeDtypeStruct((b * h, t), f32), name="cumsum_lanes",
        compiler_params=_params(1))(lft.reshape(b * h, t), tri).reshape(b, h, t)


def _attn_kernel(qat_ref, qnext_ref, ka_ref, vt_ref, o_ref, m_ref, acc_ref, sa_ref, sb_ref):
    i = pl.program_id(2)
    tq = qat_ref.shape[3]
    tk = vt_ref.shape[3]
    causal = (lax.broadcasted_iota(jnp.int32, (tk, tq), 0) <= lax.broadcasted_iota(jnp.int32, (tk, tq), 1))
    m_ref[...] = jnp.full(m_ref.shape, NEG_INF, f32)
    acc_ref[...] = jnp.zeros(acc_ref.shape, f32)

    def scores(q_ref, j, s, buf):
        rows = pl.ds(pl.multiple_of(j * tk, tk), tk)
        buf[s] = _dot(ka_ref[0, s, rows, :], q_ref[0, 0, s * LANES:(s + 1) * LANES, :])

    def consume(j, s, buf, masked):
        sc = buf[s]
        if masked:
            sc = jnp.where(causal, sc, NEG_INF)
        m = m_ref[s]
        m_new = jnp.maximum(m, jnp.max(sc, axis=0, keepdims=True))
        alpha = jnp.exp2(m - m_new)
        pr = jnp.exp2(sc - m_new)
        vt = vt_ref[0, j, s * V_ROWS:(s + 1) * V_ROWS, :]
        acc_ref[s] = alpha * acc_ref[s] + _dot(vt, pr.astype(bf16))
        m_ref[s] = m_new

    heads = range(ATT_HEADS)

    def step(j, cur, nxt):
        for s in heads:
            scores(qat_ref, j + 1, s, nxt)
            consume(j, s, cur, False)

    def last_step(cur, nxt):
        for s in heads:
            scores(qnext_ref, 0, s, nxt)
            consume(i, s, cur, True)

    @pl.when(i == 0)
    def _():
        for s in heads:
            scores(qat_ref, 0, s, sa_ref)

    def sweep(a, b):
        def two_steps(t, carry):
            step(2 * t, a, b)
            step(2 * t + 1, b, a)
            return carry

        lax.fori_loop(0, i // 2, two_steps, 0)

        @pl.when(i % 2 == 0)
        def _():
            last_step(a, b)

        @pl.when(i % 2 == 1)
        def _():
            step(i - 1, a, b)
            last_step(b, a)

    first_in_b = ((i * (i + 1)) // 2) % 2

    @pl.when(first_in_b == 0)
    def _():
        sweep(sa_ref, sb_ref)

    @pl.when(first_in_b == 1)
    def _():
        sweep(sb_ref, sa_ref)

    o_t = jnp.concatenate([acc_ref[s, :HEAD_DIM, :] / acc_ref[s, HEAD_DIM:HEAD_DIM + 1, :]
                           for s in range(ATT_HEADS)], axis=0)
    o_ref[0] = o_t.T.astype(bf16)


def _attn_prompt(qat, ka, vt):
    b, nq, _, tq = qat.shape
    s = nq * tq
    g = ATT_HEADS
    return pl.pallas_call(
        _attn_kernel, grid=(b, N_HEADS // g, nq),
        in_specs=[pl.BlockSpec((1, 1, g * LANES, tq), lambda i, h, j: (i, j, h, 0)),
                  pl.BlockSpec((1, 1, g * LANES, tq), lambda i, h, j: (i, jnp.minimum(j + 1, nq - 1), h, 0)),
                  pl.BlockSpec((1, g, s, LANES), lambda i, h, j: (i, h, 0, 0)),
                  pl.BlockSpec((1, nq, g * V_ROWS, tq), lambda i, h, j: (i, 0, h, 0))],
        out_specs=pl.BlockSpec((1, tq, g * HEAD_DIM), lambda i, h, j: (i, j, h)),
        out_shape=jax.ShapeDtypeStruct((b, s, D_MODEL), bf16),
        scratch_shapes=[pltpu.VMEM((g, 1, tq), f32), pltpu.VMEM((g, V_ROWS, tq), f32),
                        pltpu.VMEM((g, tq, tq), f32), pltpu.VMEM((g, tq, tq), f32)],
        name="attn_prompt",
        compiler_params=_params(3))(qat, qat, ka, vt)


def _attn_sample_kernel(q_ref, kn_ref, vn_ref, kp_hbm, vp_hbm, cq_ref, ck_ref, o_ref, kbuf, vbuf, sems):
    t = q_ref.shape[1]
    width, past = kbuf.shape[1], kbuf.shape[2]
    groups = kp_hbm.shape[1] // width
    n = pl.program_id(0) * groups + pl.program_id(1)
    n_steps = kp_hbm.shape[0] * groups

    def copies(step, slot):
        stream, group = step // groups, step % groups
        rows = pl.ds(pl.multiple_of(group * width, width), width)
        return [pltpu.make_async_copy(src.at[stream, rows, :], buf.at[slot], sems.at[which, slot])
                for which, (src, buf) in enumerate(((kp_hbm, kbuf), (vp_hbm, vbuf)))]

    @pl.when(n == 0)
    def _():
        for step in range(RING_SLOTS - 1):
            for c in copies(step, step):
                c.start()

    @pl.when(n + RING_SLOTS - 1 < n_steps)
    def _():
        for c in copies(n + RING_SLOTS - 1, (n + RING_SLOTS - 1) % RING_SLOTS):
            c.start()

    slot = n % RING_SLOTS
    for c in copies(n, slot):
        c.wait()

    low = lax.broadcasted_iota(jnp.int32, (1, LANES), 1) < HEAD_DIM
    causal = (lax.broadcasted_iota(jnp.int32, (t, t), 1) <= lax.broadcasted_iota(jnp.int32, (t, t), 0))
    n_pairs = q_ref.shape[2] // LANES

    def pair_scores(hp):
        lanes = slice(hp * LANES, (hp + 1) * LANES)
        q2 = q_ref[0, :, lanes]
        kp2 = kbuf[slot, lanes, :].astype(bf16)
        kn2 = kn_ref[0, :, lanes]
        out = []
        for sub in range(2):
            qm = jnp.where(low if sub == 0 else jnp.logical_not(low), q2, jnp.zeros_like(q2))
            out.append((_dot(qm, kp2), _dot_nt(qm, kn2)))
        return out

    ahead = pair_scores(0)
    for hp in range(n_pairs):
        raw = ahead
        if hp + 1 < n_pairs:
            ahead = pair_scores(hp + 1)
        lanes = slice(hp * LANES, (hp + 1) * LANES)
        vp2 = vbuf[slot, lanes, :].astype(bf16)
        vn2 = vn_ref[0, :, lanes]
        halves = []
        for sub in range(2):
            h = 2 * hp + sub
            cq = cq_ref[0, 0, :, h:h + 1]
            s_past = raw[sub][0] + cq - ck_ref[0, h:h + 1, :past]
            s_new = raw[sub][1] + cq - ck_ref[0, h:h + 1, past:past + t]
            s_new = jnp.where(causal, s_new, NEG_INF)
            m = jnp.maximum(jnp.max(s_past, axis=1, keepdims=True), jnp.max(s_new, axis=1, keepdims=True))
            p_past = jnp.exp(s_past - m)
            p_new = jnp.exp(s_new - m)
            l = jnp.sum(p_past, axis=1, keepdims=True) + jnp.sum(p_new, axis=1, keepdims=True)
            acc = _dot_nt(p_past.astype(bf16), vp2) + _dot(p_new.astype(bf16), vn2)
            halves.append(acc / l)
        o_ref[0, :, lanes] = jnp.where(low, halves[0], halves[1]).astype(bf16)


def _attn_sample(qb, kb, vb, past_kt, past_vt, c_nat, c_t):
    b, t, _ = qb.shape
    past = past_kt.shape[2]
    g = SAMPLE_HEADS
    width = g * HEAD_DIM
    new = pl.BlockSpec((1, t, width), lambda i, h: (i, 0, h))
    old = pl.BlockSpec(memory_space=pl.ANY)
    return pl.pallas_call(
        _attn_sample_kernel, grid=(b, N_HEADS // g),
        in_specs=[new, new, new, old, old,
                  pl.BlockSpec((1, 1, t, g), lambda i, h: (i, h, 0, 0)),
                  pl.BlockSpec((1, g, c_t.shape[2]), lambda i, h: (i, h, 0))],
        out_specs=new,
        out_shape=jax.ShapeDtypeStruct((b, t, D_MODEL), bf16),
        scratch_shapes=[pltpu.VMEM((RING_SLOTS, width, past), f32), pltpu.VMEM((RING_SLOTS, width, past), f32),
                        pltpu.SemaphoreType.DMA((2, RING_SLOTS))],
        name="attn_sample", compiler_params=_params(2))(qb, kb, vb, past_kt, past_vt, c_nat, c_t)


def kernel(x_prompt, x_sample, state_conv, cache_k, cache_v, cache_logf, norm_ffn, ffn_w_in, ffn_w_out, norm_mix, conv_w_in, conv_w, conv_w_out, attn_w_in, attn_b_f, q_norm, k_norm, attn_w_out):
    bp, sp, d = x_prompt.shape
    bs, ss, _ = x_sample.shape
    past = cache_k.shape[2]

    n_p, n_s = bp * sp, bs * ss
    def ffn(xs, i, j, w, **kw):
        return _ffn(xs, norm_ffn[i, j].reshape(1, d), *w, **kw)

    w_slabs, wo_slabs = 32, 16
    ffn_casts = lambda i, j: [(ffn_w_in, (i, j), w_slabs), (ffn_w_out, (i, j), wo_slabs)]
    square_cast = lambda w: (w, (0,), w_slabs)

    y, conv_in_b, conv_out_b = ffn((x_prompt.reshape(n_p, d), x_sample.reshape(n_s, d)), 0, 0,
                                   (ffn_w_in[0, 0].astype(bf16), ffn_w_out[0, 0].astype(bf16)),
                                   casts=[square_cast(conv_w_in), square_cast(conv_w_out)])
    cw = (norm_mix[0].reshape(1, d), conv_in_b, conv_w[0], conv_out_b)
    y_mixed, conv_p, *w01 = _conv_mixer(y, bp, *cw, t=sp,
                                        casts=[(ffn_w_in, (0, 1), wo_slabs), (ffn_w_out, (0, 1), wo_slabs)])
    y, conv_s = _conv_mixer_sample(y_mixed, state_conv[0], *cw, t=ss, row0=n_p)
    y, *w10 = ffn(y, 0, 1, w01, casts=ffn_casts(1, 0))

    y, *w11, attn_out_b = ffn(y, 1, 0, w10, casts=ffn_casts(1, 1) + [square_cast(attn_w_out)])

    w_t = attn_w_in[0].T
    wt = jnp.concatenate([w_t[3 * d:], jnp.zeros((F_ROWS - N_HEADS, d), f32), w_t[d:2 * d], w_t[:d], w_t[2 * d:3 * d]],
                         axis=0).astype(bf16)
    head_of = jnp.arange(d) // HEAD_DIM
    seg = ((head_of[:, None] == jnp.arange(LANES)[None, :]).astype(f32) * (1.0 / HEAD_DIM)).astype(bf16)
    expand = (jnp.arange(LANES)[:, None] == head_of[None, :]).astype(bf16)
    expand2 = jnp.concatenate([expand, expand], axis=0)
    g_mix = norm_mix[1].reshape(1, d)
    bcol = attn_b_f[0].reshape(N_HEADS, 1)
    gk = jnp.tile(k_norm[0], N_HEADS).reshape(1, d)

    per_channel = lambda gain: jnp.broadcast_to(gain[:, None], (HEAD_DIM, ROW_TILE))
    kpt, vpt, fpt, ka, qat, vt = _attn_proj_prompt(
        y, g_mix, wt, bcol, per_channel(q_norm[0]), per_channel(k_norm[0]),
        jnp.triu(jnp.ones((ROW_TILE, ROW_TILE), f32)).astype(bf16), b=bp, t=sp)
    ap = _attn_prompt(qat, ka, vt)

    kq, vq, fq, fqt, qsb, ksb, vsb = _attn_proj(
        y, g_mix, wt, bcol, attn_b_f[0].reshape(1, N_HEADS), seg, expand2, jnp.tile(q_norm[0], N_HEADS).reshape(1, d), gk,
        rows=n_s, row0=n_p)
    pad = (-(past + ss)) % SCAN_TILE
    lf_all = jnp.concatenate([jnp.swapaxes(cache_logf[0], 1, 2),
                              jnp.swapaxes(fqt.reshape(N_HEADS, bs, ss), 0, 1),
                              jnp.zeros((bs, N_HEADS, pad), f32)], axis=2)
    cst = _cumsum_lanes(lf_all, jnp.triu(jnp.ones((SCAN_TILE, SCAN_TILE), f32)).astype(bf16))
    per_stream = lambda a: a.reshape(bs, ss, d)
    channel_major = lambda a: jnp.transpose(a, (0, 2, 3, 1)).reshape(bs, d, past)
    a_s = _attn_sample(per_stream(qsb), per_stream(ksb), per_stream(vsb),
                       channel_major(cache_k[0]), channel_major(cache_v[0]),
                       jnp.swapaxes(cst[:, :, past:past + ss].reshape(bs, -1, SAMPLE_HEADS, ss), 2, 3), cst)

    yp, ys = ffn(y, 1, 1, w11, mix=(ap.reshape(n_p, d), a_s.reshape(n_s, d), attn_out_b))

    hd = (N_HEADS, HEAD_DIM)
    token_major = lambda a: jnp.transpose(a.reshape(bp, *hd, sp), (0, 3, 1, 2))[None]
    return (yp.reshape(bp, sp, d), ys.reshape(bs, ss, d),
            conv_p[None], conv_s[None],
            token_major(kpt), token_major(vpt), jnp.swapaxes(fpt, 1, 2)[None],
            kq.reshape(1, bs, ss, *hd), vq.reshape(1, bs, ss, *hd), fq.reshape(1, bs, ss, N_HEADS))
```

```python
import functools

import jax
import jax.numpy as jnp
from jax import lax
from jax.experimental import pallas as pl
from jax.experimental.pallas import tpu as pltpu

D_MODEL = 1024
N_HEADS = 16
HEAD_DIM = 64
D_FF = 2816
CONV_WIDTH = 3
NORM_EPS = 1e-6
NEG_INF = -1e30
FFN_RESIDUAL = 0.5

LANES = 128
F_ROWS = 128
SUBLANES = 8
BF16_ROWS = 16
VMEM_LIMIT = 56 * 1024 * 1024
FF_CHUNKS = ((0, 1536), (1536, 2816))
ROW_TILE = 512
CONV_TILE = 1024
ATT_TILE = 256
ATT_HEADS = 16
SAMPLE_HEADS = 8
SCAN_TILE = 256
N_AUG = 3
V_ROWS = HEAD_DIM + BF16_ROWS
LOG2_E = 1.4426950408889634

f32 = jnp.float32
bf16 = jnp.bfloat16


def _dot(a, b):
    return jnp.dot(a, b, preferred_element_type=f32)


def _dot_nt(a, b):
    return lax.dot_general(a, b, (((1,), (1,)), ((), ())), preferred_element_type=f32)


def _rms(x, g):
    return x * lax.rsqrt(jnp.mean(x * x, axis=-1, keepdims=True) + NORM_EPS) * g


def _log_sigmoid(z):
    return -(jnp.maximum(-z, 0.0) + jnp.log1p(jnp.exp(-jnp.abs(z))))


def _split3(x):
    hi = x.astype(bf16)
    r1 = x - hi.astype(f32)
    mid = r1.astype(bf16)
    lo = (r1 - mid.astype(f32)).astype(bf16)
    return hi, mid, lo


def _params(n_axes):
    return pltpu.CompilerParams(dimension_semantics=("arbitrary",) * n_axes, vmem_limit_bytes=VMEM_LIMIT)


def _const_spec(shape):
    return pl.BlockSpec(shape, lambda *_: (0,) * len(shape), pipeline_mode=pl.Buffered(1))


def _cast_job(src, lead, steps, lin):
    rows, cols = src.shape[-2:]
    slab = rows // steps
    assert slab * steps == rows and slab % BF16_ROWS == 0
    pos = lambda *idx: jnp.minimum(lin(*idx), steps - 1)
    in_spec = pl.BlockSpec((None,) * len(lead) + (slab, cols), lambda *idx: (*lead, pos(*idx), 0))
    out_spec = pl.BlockSpec((slab, cols), lambda *idx: (pos(*idx), 0))
    return src, in_spec, out_spec, jax.ShapeDtypeStruct((rows, cols), bf16)


def _run_casts(srcs, dsts):
    for src_ref, dst_ref in zip(srcs, dsts):
        dst_ref[...] = src_ref[...].astype(bf16)


def _ffn_kernel(*refs, n_prompt, split_in, mix, n_jobs):
    refs = list(refs)
    cast_srcs = cast_dsts = ()
    if n_jobs:
        cast_dsts = refs[-n_jobs:]
        del refs[-n_jobs:]
        n_out = 2 if mix else 1
        cast_srcs = refs[-n_out - n_jobs:-n_out]
        del refs[-n_out - n_jobs:-n_out]
    is_prompt = pl.program_id(0) < n_prompt

    def rows():
        if split_in or mix:
            p_ref, s_ref = refs.pop(0), refs.pop(0)
            return jnp.where(is_prompt, p_ref[...], s_ref[...])
        return refs.pop(0)[...]

    if mix:
        x = refs.pop(0)[...]
        a = rows()
        x = x + _dot(a, refs.pop(0)[...])
    else:
        x = rows()
    g_ref, win_ref, wout_ref = refs[:3]
    outs = refs[3:]
    xn = _rms(x, g_ref[...]).astype(bf16)
    acc = None
    for lo, hi in FF_CHUNKS:
        a = _dot(xn, win_ref[:, lo:hi])
        b = _dot(xn, win_ref[:, D_FF + lo:D_FF + hi])
        if lo == 0:
            _run_casts(cast_srcs, cast_dsts)
        h = (a * jax.nn.sigmoid(a) * b).astype(bf16)
        y = _dot(h, wout_ref[lo:hi, :])
        acc = y if acc is None else acc + y
    y = x + FFN_RESIDUAL * acc
    if mix:
        @pl.when(is_prompt)
        def _():
            outs[0][...] = y

        @pl.when(jnp.logical_not(is_prompt))
        def _():
            outs[1][...] = y
    else:
        outs[0][...] = y


def _ffn(xs, g, w_in, w_out, mix=None, casts=()):
    tm = ROW_TILE
    split_in = isinstance(xs, tuple)
    if mix is None and split_in:
        n_prompt, n_sample = xs[0].shape[0] // tm, xs[1].shape[0] // tm
    elif mix is not None:
        n_prompt, n_sample = mix[0].shape[0] // tm, mix[1].shape[0] // tm
    else:
        n_prompt, n_sample = xs.shape[0] // tm, 0
    assert n_sample in (0, 1)
    steps = n_prompt + n_sample
    row = pl.BlockSpec((tm, D_MODEL), lambda i: (i, 0))
    prompt_row = pl.BlockSpec((tm, D_MODEL), lambda i: (jnp.minimum(i, n_prompt - 1), 0))
    sample_row = pl.BlockSpec((tm, D_MODEL), lambda i: (0, 0))
    w_specs = [_const_spec((1, D_MODEL)), _const_spec((D_MODEL, 2 * D_FF)), _const_spec((D_FF, D_MODEL))]
    stacked = jax.ShapeDtypeStruct((steps * tm, D_MODEL), f32)
    if mix is not None:
        a_p, a_s, wm = mix
        ins = (xs, a_p, a_s, wm)
        specs = [row, prompt_row, sample_row, _const_spec((D_MODEL, D_MODEL))]
        out_specs = [prompt_row, sample_row]
        out_shape = [jax.ShapeDtypeStruct((n_prompt * tm, D_MODEL), f32),
                     jax.ShapeDtypeStruct((n_sample * tm, D_MODEL), f32)]
    elif split_in:
        ins, specs, out_specs, out_shape = tuple(xs), [prompt_row, sample_row], [row], [stacked]
    else:
        ins, specs, out_specs, out_shape = (xs,), [row], [row], [stacked]
    jobs = [_cast_job(src, lead, n, lambda i: i) for src, lead, n in casts]
    kern = functools.partial(_ffn_kernel, n_prompt=n_prompt, split_in=split_in, mix=mix is not None,
                             n_jobs=len(jobs))
    return pl.pallas_call(
        kern, grid=(steps,), in_specs=specs + w_specs + [j[1] for j in jobs],
        out_specs=out_specs + [j[2] for j in jobs], out_shape=out_shape + [j[3] for j in jobs],
        name="ffn" if mix is None else "mix_ffn",
        compiler_params=_params(1))(*ins, g, w_in, w_out, *[j[0] for j in jobs])


def _conv_body(x, g_ref, win_ref, wk_ref, wout_ref, shifted, between=lambda: None):
    xn = _rms(x, g_ref[...]).astype(bf16)
    ch = _dot(xn, win_ref[:, D_MODEL:])
    gate_b = _dot(xn, win_ref[:, :D_MODEL])
    between()
    u = ch[:, :D_MODEL] * ch[:, D_MODEL:]
    u1, u2 = shifted(u)
    wk = wk_ref[...]
    conv = wk[0:1, :] * u2 + wk[1:2, :] * u1 + wk[2:3, :] * u
    return x + _dot((gate_b * conv).astype(bf16), wout_ref[...]), u


def _conv_kernel(x_ref, g_ref, win_ref, wk_ref, wout_ref, *rest, n_jobs):
    o_ref, st_ref = rest[n_jobs:n_jobs + 2]
    carry_ref = rest[-1]
    tt = x_ref.shape[0]

    @pl.when(pl.program_id(1) == 0)
    def _():
        carry_ref[...] = jnp.zeros(carry_ref.shape, f32)

    def shifted(u):
        prev2 = carry_ref[0:1, :]
        prev1 = carry_ref[1:2, :]
        row = lax.broadcasted_iota(jnp.int32, (tt, 1), 0)
        return (jnp.where(row == 0, prev1, pltpu.roll(u, 1, 0)),
                jnp.where(row == 0, prev2, jnp.where(row == 1, prev1, pltpu.roll(u, 2, 0))))

    o_ref[...], u = _conv_body(x_ref[...], g_ref, win_ref, wk_ref, wout_ref, shifted,
                               lambda: _run_casts(rest[:n_jobs], rest[n_jobs + 2:-1]))
    last = u[tt - 2:tt, :]
    carry_ref[0:2, :] = last
    st_ref[0] = last


def _conv_mixer(x, n_streams, g, win, wk, wout, *, t, casts=()):
    tt = CONV_TILE
    nt = t // tt
    blk = pl.BlockSpec((tt, D_MODEL), lambda i, j: (i * nt + j, 0))
    st = pl.BlockSpec((1, CONV_WIDTH - 1, D_MODEL), lambda i, j: (i, 0, 0))
    specs = [blk, _const_spec((1, D_MODEL)), _const_spec((D_MODEL, 3 * D_MODEL)),
             _const_spec((CONV_WIDTH, D_MODEL)), _const_spec((D_MODEL, D_MODEL))]
    jobs = [_cast_job(src, lead, n, lambda i, j: i * nt + j) for src, lead, n in casts]
    return pl.pallas_call(
        functools.partial(_conv_kernel, n_jobs=len(jobs)), grid=(n_streams, nt),
        in_specs=specs + [j[1] for j in jobs], out_specs=[blk, st] + [j[2] for j in jobs],
        out_shape=[jax.ShapeDtypeStruct(x.shape, f32),
                   jax.ShapeDtypeStruct((n_streams, CONV_WIDTH - 1, D_MODEL), f32)] + [j[3] for j in jobs],
        scratch_shapes=[pltpu.VMEM((SUBLANES, D_MODEL), f32)], input_output_aliases={0: 0},
        name="conv_mixer", compiler_params=_params(2))(x, g, win, wk, wout, *[j[0] for j in jobs])


def _conv_sample_kernel(x_ref, h1_ref, h2_ref, g_ref, win_ref, wk_ref, wout_ref, o_ref, u_ref, *, t):
    rows = x_ref.shape[0]

    def shifted(u):
        pos = lax.rem(lax.broadcasted_iota(jnp.int32, (rows, 1), 0), t)
        h1 = h1_ref[...]
        return (jnp.where(pos == 0, h1, pltpu.roll(u, 1, 0)),
                jnp.where(pos == 0, h2_ref[...], jnp.where(pos == 1, h1, pltpu.roll(u, 2, 0))))

    o_ref[...], u_ref[...] = _conv_body(x_ref[...], g_ref, win_ref, wk_ref, wout_ref, shifted)


def _conv_mixer_sample(x, state, g, win, wk, wout, *, t, row0):
    n = state.shape[0]
    rows = n * t
    blk = pl.BlockSpec((rows, D_MODEL), lambda i: (row0 // rows, 0))
    hist = pl.BlockSpec((rows, D_MODEL), lambda i: (0, 0))
    y, u = pl.pallas_call(
        functools.partial(_conv_sample_kernel, t=t), grid=(1,),
        in_specs=[blk, hist, hist, _const_spec((1, D_MODEL)), _const_spec((D_MODEL, 3 * D_MODEL)),
                  _const_spec((CONV_WIDTH, D_MODEL)), _const_spec((D_MODEL, D_MODEL))],
        out_specs=[blk, hist],
        out_shape=[jax.ShapeDtypeStruct(x.shape, f32), jax.ShapeDtypeStruct((rows, D_MODEL), f32)],
        input_output_aliases={0: 0},
        name="conv_mixer_sample", compiler_params=_params(1))(
            x, jnp.repeat(state[:, 1], t, axis=0), jnp.repeat(state[:, 0], t, axis=0), g, win, wk, wout)
    return y, u.reshape(n, t, D_MODEL)[:, t - (CONV_WIDTH - 1):]


def _head_norm(t, gain, seg_ref, exp_ref):
    ms = _dot((t * t).astype(bf16), seg_ref[...])
    r = lax.rsqrt(ms + NORM_EPS)
    r_hi = r.astype(bf16)
    r_lo = (r - r_hi.astype(f32)).astype(bf16)
    rb = _dot(jnp.concatenate([r_hi, r_lo], axis=1), exp_ref[...])
    return t * rb * gain


def _proj_kernel(x_ref, g_ref, wt_ref, bcol_ref, brow_ref, seg_ref, exp_ref, gq_ref, gk_ref,
                 k_ref, v_ref, lf_ref, lft_ref, qb_ref, kb_ref, vb_ref):
    xn = _rms(x_ref[...], g_ref[...]).astype(bf16)
    p = _dot_nt(xn, wt_ref[...])
    k = _head_norm(p[:, F_ROWS:F_ROWS + D_MODEL], gk_ref[...], seg_ref, exp_ref)
    q = _head_norm(p[:, F_ROWS + D_MODEL:F_ROWS + 2 * D_MODEL], gq_ref[...], seg_ref, exp_ref)
    v = p[:, F_ROWS + 2 * D_MODEL:]
    k_ref[...] = k
    v_ref[...] = v
    lf_ref[...] = _log_sigmoid(p[:, :N_HEADS] + brow_ref[...])
    lft_ref[...] = _log_sigmoid(_dot_nt(wt_ref[:N_HEADS, :], xn) + bcol_ref[...])
    qb_ref[...] = (q * (HEAD_DIM ** -0.5)).astype(bf16)
    kb_ref[...] = k.astype(bf16)
    vb_ref[...] = v.astype(bf16)


def _attn_proj(x, g, wt, bcol, brow, seg, exp, gq, gk, *, rows, row0):
    blk = lambda dt: jax.ShapeDtypeStruct((rows, D_MODEL), dt)
    full = lambda shape: pl.BlockSpec(shape, lambda i: (0, 0))
    return pl.pallas_call(
        _proj_kernel, grid=(1,),
        in_specs=[pl.BlockSpec((rows, D_MODEL), lambda i: (row0 // rows, 0)),
                  _const_spec((1, D_MODEL)), _const_spec((F_ROWS + 3 * D_MODEL, D_MODEL)),
                  _const_spec((N_HEADS, 1)), _const_spec((1, N_HEADS)),
                  _const_spec((D_MODEL, LANES)), _const_spec((2 * LANES, D_MODEL)),
                  _const_spec((1, D_MODEL)), _const_spec((1, D_MODEL))],
        out_specs=[full((rows, D_MODEL)), full((rows, D_MODEL)), full((rows, N_HEADS)), full((N_HEADS, rows)),
                   full((rows, D_MODEL)), full((rows, D_MODEL)), full((rows, D_MODEL))],
        out_shape=[blk(f32), blk(f32), jax.ShapeDtypeStruct((rows, N_HEADS), f32),
                   jax.ShapeDtypeStruct((N_HEADS, rows), f32), blk(bf16), blk(bf16), blk(bf16)],
        name="attn_proj", compiler_params=_params(1))(x, g, wt, bcol, brow, seg, exp, gq, gk)


def _proj_prompt_kernel(x_ref, g_ref, wt_ref, bcol_ref, gqc_ref, gkc_ref, triu_ref,
                        kt_ref, vt_ref, lft_ref, ka_ref, qat_ref, vtb_ref, ccol_ref):
    t = pl.program_id(1)
    tt = x_ref.shape[0]
    n_tiles = tt // ATT_TILE

    @pl.when(t == 0)
    def _():
        ccol_ref[...] = jnp.zeros(ccol_ref.shape, f32)

    xn = _rms(x_ref[...], g_ref[...]).astype(bf16)
    fk = _dot_nt(wt_ref[:F_ROWS + D_MODEL, :], xn)
    qt = _dot_nt(wt_ref[F_ROWS + D_MODEL:F_ROWS + 2 * D_MODEL, :], xn)
    kt = fk[F_ROWS:]

    lft = _log_sigmoid(fk[:N_HEADS] + bcol_ref[...])
    lft_ref[0] = lft
    ct3 = _dot(jnp.concatenate(_split3(lft), axis=0), triu_ref[...])
    ct = ct3[:N_HEADS] + ct3[N_HEADS:2 * N_HEADS] + ct3[2 * N_HEADS:] + ccol_ref[:, 0:1]
    ccol_ref[...] = jnp.broadcast_to(ct[:, tt - 1:tt], ccol_ref.shape)
    ct_pieces = [piece.astype(f32) for piece in _split3(ct * LOG2_E)]

    vt = _dot_nt(wt_ref[F_ROWS + 2 * D_MODEL:, :], xn)
    vt_ref[0] = vt
    vtb = vt.astype(bf16)
    ones_rows = jnp.where(lax.broadcasted_iota(jnp.int32, (V_ROWS - HEAD_DIM, ATT_TILE), 0) == 0, 1.0, 0.0).astype(bf16)
    for j in range(n_tiles):
        for h in range(N_HEADS):
            vtb_ref[0, j, h * V_ROWS:h * V_ROWS + HEAD_DIM, :] = (
                vtb[h * HEAD_DIM:(h + 1) * HEAD_DIM, j * ATT_TILE:(j + 1) * ATT_TILE])
            vtb_ref[0, j, h * V_ROWS + HEAD_DIM:(h + 1) * V_ROWS, :] = ones_rows

    sub = lax.broadcasted_iota(jnp.int32, (SUBLANES, 1), 0)
    ones = jnp.where(sub < 2 * N_AUG, 1.0, 0.0) * jnp.ones((1, tt), f32)
    pad = jnp.zeros((LANES - HEAD_DIM - SUBLANES, tt), f32)
    gqc = gqc_ref[...]
    gkc = gkc_ref[...]

    def head_rms(th, gain):
        return th * lax.rsqrt(jnp.mean(th * th, axis=0, keepdims=True) + NORM_EPS) * gain

    for h in range(N_HEADS):
        rows = slice(h * HEAD_DIM, (h + 1) * HEAD_DIM)
        kn = head_rms(kt[rows, :], gkc)
        kt_ref[0, rows, :] = kn
        aug_k, aug_q = ones, ones
        for n in range(N_AUG):
            piece = ct_pieces[n][h:h + 1, :]
            aug_k = jnp.where(sub == N_AUG + n, -piece, aug_k)
            aug_q = jnp.where(sub == n, piece, aug_q)
        ka_ref[0, h] = jnp.concatenate([kn, aug_k, pad], axis=0).T.astype(bf16)
        blk = jnp.concatenate([head_rms(qt[rows, :], gqc) * (LOG2_E * HEAD_DIM ** -0.5), aug_q, pad],
                              axis=0).astype(bf16)
        for j in range(n_tiles):
            qat_ref[0, j, h * LANES:(h + 1) * LANES, :] = blk[:, j * ATT_TILE:(j + 1) * ATT_TILE]


def _attn_proj_prompt(x, g, wt, bcol, gqc, gkc, triu, *, b, t):
    tt = ROW_TILE
    nt = tt // ATT_TILE
    time_minor = pl.BlockSpec((1, D_MODEL, tt), lambda i, j: (i, 0, j))
    big = jax.ShapeDtypeStruct((b, D_MODEL, t), f32)
    return pl.pallas_call(
        _proj_prompt_kernel, grid=(b, t // tt),
        in_specs=[pl.BlockSpec((tt, D_MODEL), lambda i, j: (i * (t // tt) + j, 0)),
                  _const_spec((1, D_MODEL)), _const_spec((F_ROWS + 3 * D_MODEL, D_MODEL)),
                  _const_spec((N_HEADS, 1)), _const_spec((HEAD_DIM, tt)), _const_spec((HEAD_DIM, tt)),
                  _const_spec((tt, tt))],
        out_specs=[time_minor, time_minor,
                   pl.BlockSpec((1, N_HEADS, tt), lambda i, j: (i, 0, j)),
                   pl.BlockSpec((1, N_HEADS, tt, LANES), lambda i, j: (i, 0, j, 0)),
                   pl.BlockSpec((1, nt, N_HEADS * LANES, ATT_TILE), lambda i, j: (i, j, 0, 0)),
                   pl.BlockSpec((1, nt, N_HEADS * V_ROWS, ATT_TILE), lambda i, j: (i, j, 0, 0))],
        out_shape=[big, big, jax.ShapeDtypeStruct((b, N_HEADS, t), f32),
                   jax.ShapeDtypeStruct((b, N_HEADS, t, LANES), bf16),
                   jax.ShapeDtypeStruct((b, t // ATT_TILE, N_HEADS * LANES, ATT_TILE), bf16),
                   jax.ShapeDtypeStruct((b, t // ATT_TILE, N_HEADS * V_ROWS, ATT_TILE), bf16)],
        scratch_shapes=[pltpu.VMEM((N_HEADS, LANES), f32)],
        name="attn_proj_prompt", compiler_params=_params(2))(x, g, wt, bcol, gqc, gkc, triu)


def _scan_kernel(x_ref, tri_ref, o_ref):
    rows = x_ref.shape[0]
    n = x_ref.shape[1] // SCAN_TILE
    carry = jnp.zeros((rows, 1), f32)
    for c in range(n):
        sl = slice(c * SCAN_TILE, (c + 1) * SCAN_TILE)
        cs = _dot(jnp.concatenate(_split3(x_ref[:, sl]), axis=0), tri_ref[...])
        out = cs[:rows] + cs[rows:2 * rows] + cs[2 * rows:] + carry
        o_ref[:, sl] = out
        carry = out[:, SCAN_TILE - 1:SCAN_TILE]


def _cumsum_lanes(lft, tri):
    b, h, t = lft.shape
    blk = pl.BlockSpec((b * h, t), lambda i: (0, 0))
    return pl.pallas_call(
        _scan_kernel, grid=(1,), in_specs=[blk, _const_spec((SCAN_TILE, SCAN_TILE))], out_specs=blk,
        out_shape=jax.ShapeDtypeStruct((b * h, t), f32), name="cumsum_lanes",
        compiler_params=_params(1))(lft.reshape(b * h, t), tri).reshape(b, h, t)


def _attn_kernel(qat_ref, qnext_ref, ka_ref, vt_ref, o_ref, m_ref, acc_ref, sa_ref, sb_ref):
    i = pl.program_id(2)
    tq = qat_ref.shape[3]
    tk = vt_ref.shape[3]
    causal = (lax.broadcasted_iota(jnp.int32, (tk, tq), 0) <= lax.broadcasted_iota(jnp.int32, (tk, tq), 1))
    m_ref[...] = jnp.full(m_ref.shape, NEG_INF, f32)
    acc_ref[...] = jnp.zeros(acc_ref.shape, f32)

    def scores(q_ref, j, s, buf):
        rows = pl.ds(pl.multiple_of(j * tk, tk), tk)
        buf[s] = _dot(ka_ref[0, s, rows, :], q_ref[0, 0, s * LANES:(s + 1) * LANES, :])

    def consume(j, s, buf, masked):
        sc = buf[s]
        if masked:
            sc = jnp.where(causal, sc, NEG_INF)
        m = m_ref[s]
        m_new = jnp.maximum(m, jnp.max(sc, axis=0, keepdims=True))
        alpha = jnp.exp2(m - m_new)
        pr = jnp.exp2(sc - m_new)
        vt = vt_ref[0, j, s * V_ROWS:(s + 1) * V_ROWS, :]
        acc_ref[s] = alpha * acc_ref[s] + _dot(vt, pr.astype(bf16))
        m_ref[s] = m_new

    heads = range(ATT_HEADS)

    def step(j, cur, nxt):
        for s in heads:
            consume(j, s, cur, False)
            scores(qat_ref, j + 1, s, nxt)

    def last_step(cur, nxt):
        for s in heads:
            scores(qnext_ref, 0, s, nxt)
            consume(i, s, cur, True)

    @pl.when(i == 0)
    def _():
        for s in heads:
            scores(qat_ref, 0, s, sa_ref)

    def sweep(a, b):
        def two_steps(t, carry):
            step(2 * t, a, b)
            step(2 * t + 1, b, a)
            return carry

        lax.fori_loop(0, i // 2, two_steps, 0)

        @pl.when(i % 2 == 0)
        def _():
            last_step(a, b)

        @pl.when(i % 2 == 1)
        def _():
            step(i - 1, a, b)
            last_step(b, a)

    first_in_b = ((i * (i + 1)) // 2) % 2

    @pl.when(first_in_b == 0)
    def _():
        sweep(sa_ref, sb_ref)

    @pl.when(first_in_b == 1)
    def _():
        sweep(sb_ref, sa_ref)

    o_t = jnp.concatenate([acc_ref[s, :HEAD_DIM, :] / acc_ref[s, HEAD_DIM:HEAD_DIM + 1, :]
                           for s in range(ATT_HEADS)], axis=0)
    o_ref[0] = o_t.T.astype(bf16)


def _attn_prompt(qat, ka, vt):
    b, nq, _, tq = qat.shape
    s = nq * tq
    g = ATT_HEADS
    return pl.pallas_call(
        _attn_kernel, grid=(b, N_HEADS // g, nq),
        in_specs=[pl.BlockSpec((1, 1, g * LANES, tq), lambda i, h, j: (i, j, h, 0)),
                  pl.BlockSpec((1, 1, g * LANES, tq), lambda i, h, j: (i, jnp.minimum(j + 1, nq - 1), h, 0)),
                  pl.BlockSpec((1, g, s, LANES), lambda i, h, j: (i, h, 0, 0)),
                  pl.BlockSpec((1, nq, g * V_ROWS, tq), lambda i, h, j: (i, 0, h, 0))],
        out_specs=pl.BlockSpec((1, tq, g * HEAD_DIM), lambda i, h, j: (i, j, h)),
        out_shape=jax.ShapeDtypeStruct((b, s, D_MODEL), bf16),
        scratch_shapes=[pltpu.VMEM((g, 1, tq), f32), pltpu.VMEM((g, V_ROWS, tq), f32),
                        pltpu.VMEM((g, tq, tq), f32), pltpu.VMEM((g, tq, tq), f32)],
        name="attn_prompt",
        compiler_params=_params(3))(qat, qat, ka, vt)


def _attn_sample_kernel(q_ref, kn_ref, vn_ref, kp_ref, vp_ref, cq_ref, ck_ref, o_ref):
    t = q_ref.shape[1]
    past = kp_ref.shape[2]
    low = lax.broadcasted_iota(jnp.int32, (1, LANES), 1) < HEAD_DIM
    causal = (lax.broadcasted_iota(jnp.int32, (t, t), 1) <= lax.broadcasted_iota(jnp.int32, (t, t), 0))
    n_pairs = q_ref.shape[2] // LANES

    def pair_scores(hp):
        lanes = slice(hp * LANES, (hp + 1) * LANES)
        q2 = q_ref[0, :, lanes]
        kp2 = kp_ref[0, lanes, :].astype(bf16)
        kn2 = kn_ref[0, :, lanes]
        out = []
        for sub in range(2):
            qm = jnp.where(low if sub == 0 else jnp.logical_not(low), q2, jnp.zeros_like(q2))
            out.append((_dot(qm, kp2), _dot_nt(qm, kn2)))
        return out

    ahead = pair_scores(0)
    for hp in range(n_pairs):
        raw = ahead
        if hp + 1 < n_pairs:
            ahead = pair_scores(hp + 1)
        lanes = slice(hp * LANES, (hp + 1) * LANES)
        vp2 = vp_ref[0, lanes, :].astype(bf16)
        vn2 = vn_ref[0, :, lanes]
        halves = []
        for sub in range(2):
            h = 2 * hp + sub
            cq = cq_ref[0, 0, :, h:h + 1]
            s_past = raw[sub][0] + cq - ck_ref[0, h:h + 1, :past]
            s_new = raw[sub][1] + cq - ck_ref[0, h:h + 1, past:past + t]
            s_new = jnp.where(causal, s_new, NEG_INF)
            m = jnp.maximum(jnp.max(s_past, axis=1, keepdims=True), jnp.max(s_new, axis=1, keepdims=True))
            p_past = jnp.exp(s_past - m)
            p_new = jnp.exp(s_new - m)
            l = jnp.sum(p_past, axis=1, keepdims=True) + jnp.sum(p_new, axis=1, keepdims=True)
            acc = _dot_nt(p_past.astype(bf16), vp2) + _dot(p_new.astype(bf16), vn2)
            halves.append(acc / l)
        o_ref[0, :, lanes] = jnp.where(low, halves[0], halves[1]).astype(bf16)


def _attn_sample(qb, kb, vb, past_kt, past_vt, c_nat, c_t):
    b, t, _ = qb.shape
    past = past_kt.shape[2]
    g = SAMPLE_HEADS
    width = g * HEAD_DIM
    new = pl.BlockSpec((1, t, width), lambda i, h: (i, 0, h))
    old = pl.BlockSpec((1, width, past), lambda i, h: (i, h, 0))
    return pl.pallas_call(
        _attn_sample_kernel, grid=(b, N_HEADS // g),
        in_specs=[new, new, new, old, old,
                  pl.BlockSpec((1, 1, t, g), lambda i, h: (i, h, 0, 0)),
                  pl.BlockSpec((1, g, c_t.shape[2]), lambda i, h: (i, h, 0))],
        out_specs=new,
        out_shape=jax.ShapeDtypeStruct((b, t, D_MODEL), bf16),
        name="attn_sample", compiler_params=_params(2))(qb, kb, vb, past_kt, past_vt, c_nat, c_t)


def kernel(x_prompt, x_sample, state_conv, cache_k, cache_v, cache_logf, norm_ffn, ffn_w_in, ffn_w_out, norm_mix, conv_w_in, conv_w, conv_w_out, attn_w_in, attn_b_f, q_norm, k_norm, attn_w_out):
    bp, sp, d = x_prompt.shape
    bs, ss, _ = x_sample.shape
    past = cache_k.shape[2]

    n_p, n_s = bp * sp, bs * ss
    def ffn(xs, i, j, w, **kw):
        return _ffn(xs, norm_ffn[i, j].reshape(1, d), *w, **kw)

    w_slabs, wo_slabs = 32, 16
    ffn_casts = lambda i, j: [(ffn_w_in, (i, j), w_slabs), (ffn_w_out, (i, j), wo_slabs)]
    square_cast = lambda w: (w, (0,), w_slabs)

    y, conv_in_b, conv_out_b = ffn((x_prompt.reshape(n_p, d), x_sample.reshape(n_s, d)), 0, 0,
                                   (ffn_w_in[0, 0].astype(bf16), ffn_w_out[0, 0].astype(bf16)),
                                   casts=[square_cast(conv_w_in), square_cast(conv_w_out)])
    cw = (norm_mix[0].reshape(1, d), conv_in_b, conv_w[0], conv_out_b)
    y_mixed, conv_p, *w01 = _conv_mixer(y, bp, *cw, t=sp,
                                        casts=[(ffn_w_in, (0, 1), wo_slabs), (ffn_w_out, (0, 1), wo_slabs)])
    y, conv_s = _conv_mixer_sample(y_mixed, state_conv[0], *cw, t=ss, row0=n_p)
    y, *w10 = ffn(y, 0, 1, w01, casts=ffn_casts(1, 0))

    y, *w11, attn_out_b = ffn(y, 1, 0, w10, casts=ffn_casts(1, 1) + [square_cast(attn_w_out)])

    w_t = attn_w_in[0].T
    wt = jnp.concatenate([w_t[3 * d:], jnp.zeros((F_ROWS - N_HEADS, d), f32), w_t[d:2 * d], w_t[:d], w_t[2 * d:3 * d]],
                         axis=0).astype(bf16)
    head_of = jnp.arange(d) // HEAD_DIM
    seg = ((head_of[:, None] == jnp.arange(LANES)[None, :]).astype(f32) * (1.0 / HEAD_DIM)).astype(bf16)
    expand = (jnp.arange(LANES)[:, None] == head_of[None, :]).astype(bf16)
    expand2 = jnp.concatenate([expand, expand], axis=0)
    g_mix = norm_mix[1].reshape(1, d)
    bcol = attn_b_f[0].reshape(N_HEADS, 1)
    gk = jnp.tile(k_norm[0], N_HEADS).reshape(1, d)

    per_channel = lambda gain: jnp.broadcast_to(gain[:, None], (HEAD_DIM, ROW_TILE))
    kpt, vpt, fpt, ka, qat, vt = _attn_proj_prompt(
        y, g_mix, wt, bcol, per_channel(q_norm[0]), per_channel(k_norm[0]),
        jnp.triu(jnp.ones((ROW_TILE, ROW_TILE), f32)).astype(bf16), b=bp, t=sp)
    ap = _attn_prompt(qat, ka, vt)

    kq, vq, fq, fqt, qsb, ksb, vsb = _attn_proj(
        y, g_mix, wt, bcol, attn_b_f[0].reshape(1, N_HEADS), seg, expand2, jnp.tile(q_norm[0], N_HEADS).reshape(1, d), gk,
        rows=n_s, row0=n_p)
    pad = (-(past + ss)) % SCAN_TILE
    lf_all = jnp.concatenate([jnp.swapaxes(cache_logf[0], 1, 2),
                              jnp.swapaxes(fqt.reshape(N_HEADS, bs, ss), 0, 1),
                              jnp.zeros((bs, N_HEADS, pad), f32)], axis=2)
    cst = _cumsum_lanes(lf_all, jnp.triu(jnp.ones((SCAN_TILE, SCAN_TILE), f32)).astype(bf16))
    per_stream = lambda a: a.reshape(bs, ss, d)
    channel_major = lambda a: jnp.transpose(a, (0, 2, 3, 1)).reshape(bs, d, past)
    a_s = _attn_sample(per_stream(qsb), per_stream(ksb), per_stream(vsb),
                       channel_major(cache_k[0]), channel_major(cache_v[0]),
                       jnp.swapaxes(cst[:, :, past:past + ss].reshape(bs, -1, SAMPLE_HEADS, ss), 2, 3), cst)

    yp, ys = ffn(y, 1, 1, w11, mix=(ap.reshape(n_p, d), a_s.reshape(n_s, d), attn_out_b))

    hd = (N_HEADS, HEAD_DIM)
    token_major = lambda a: jnp.transpose(a.reshape(bp, *hd, sp), (0, 3, 1, 2))[None]
    return (yp.reshape(bp, sp, d), ys.reshape(bs, ss, d),
            conv_p[None], conv_s[None],
            token_major(kpt), token_major(vpt), jnp.swapaxes(fpt, 1, 2)[None],
            kq.reshape(1, bs, ss, *hd), vq.reshape(1, bs, ss, *hd), fq.reshape(1, bs, ss, N_HEADS))
```
